```python
import math
import jax, jax.numpy as jnp
from jax import lax
import numpy as np

D_MODEL = 1024
BATCH = 4
SEQ = 8192
DEPTH = 2
DEC_BATCH = 8
DEC_SEQ = 2048
PAST_LEN = 128

RWKV_HEAD = 64
RWKV_HEADS = D_MODEL // RWKV_HEAD
DECAY_LORA = 64
AAA_LORA = 64
GATE_LORA = 128
GN_EPS = 64e-5
ATTN_GROUPS = ((128, 1), (512, 4), (2048, 16))
N_GROUPS = 3
GROUP_HEADS = 6
HEAD_DIM = 64
ATTN_WIDTH = N_GROUPS * GROUP_HEADS * HEAD_DIM
Q_BLOCK = 128
REL_BUCKETS = 32
REL_MAX_DIST = 1024
D_FF = 2816
N_EXPERTS = 8
TOP_K = 2
D_FF_EXPERT = 3584
EPS = 1e-6
N_RWKV = (DEPTH + 1) // 2
N_ATTN = DEPTH // 2

kernel_name = "hybrid_rwkv7_dilated_attn_encoder"


def rms_norm(x, g):
    x32 = x.astype(jnp.float32)
    y = x32 * lax.rsqrt(jnp.mean(x32 * x32, -1, keepdims=True) + EPS)
    return (y * g).astype(x.dtype)


def rms_head(t, gain):
    t32 = t.astype(jnp.float32)
    y = t32 * lax.rsqrt(jnp.mean(t32 * t32, -1, keepdims=True) + EPS)
    return (y * gain[:, None, :]).astype(t.dtype)


def t5_bucket(rel):
    nb = REL_BUCKETS // 2
    max_exact = nb // 2
    n = jnp.abs(rel)
    large = max_exact + (jnp.log(jnp.maximum(n, 1).astype(jnp.float32) / max_exact)
                         / math.log(REL_MAX_DIST / max_exact) * (nb - max_exact)).astype(jnp.int32)
    large = jnp.minimum(large, nb - 1)
    return jnp.where(rel > 0, nb, 0) + jnp.where(n < max_exact, n, large)


def wkv_scan(r, decay, k, v, kk, b, reverse):
    B, S, H, N = r.shape
    tm = lambda t: jnp.moveaxis(t, 1, 0)

    def step(state, inp):
        r_t, w_t, k_t, v_t, kk_t, b_t = inp
        sa = jnp.einsum('bhij,bhj->bhi', state, kk_t)
        state = (state * w_t[:, :, None, :] - sa[..., None] * b_t[:, :, None, :]
                 + v_t[..., None] * k_t[:, :, None, :])
        return state, jnp.einsum('bhij,bhj->bhi', state, r_t)

    s0 = jnp.zeros((B, H, N, N), jnp.float32)
    _, y = lax.scan(step, s0, (tm(r), tm(decay), tm(k), tm(v), tm(kk), tm(b)), reverse=reverse)
    return jnp.moveaxis(y, 0, 1)


def rwkv7_time_mix(x, mu, w_rkv, w0, w1, w2, a0, a1, a2, g1, g2, k_k, k_a, r_k, ln_g, ln_b, w_o):
    B, S, D = x.shape
    H, N = RWKV_HEADS, RWKV_HEAD
    f32 = jnp.float32
    pad = jnp.zeros_like(x[:, :1])
    xx = 0.5 * (jnp.concatenate([pad, x[:, :-1]], 1) + jnp.concatenate([x[:, 1:], pad], 1)) - x
    xm = x[None] + xx[None] * mu[:, None, None, :]
    rkv = jnp.einsum('nbsd,nde->nbse', xm[:3], w_rkv)
    r, k, v = rkv[0], rkv[1], rkv[2]
    xw, xa, xg = xm[3], xm[4], xm[5]
    lw = (w0[:, None, None, :] + jnp.einsum('zbsr,zre->zbse',
          jnp.tanh(jnp.einsum('bsd,zdr->zbsr', xw, w1)), w2)).astype(f32)
    decay = jnp.exp(-jnp.exp(-jax.nn.softplus(-lw) - 0.5))
    a = jax.nn.sigmoid((a0[:, None, None, :] + jnp.einsum('zbsr,zre->zbse',
          jnp.einsum('bsd,zdr->zbsr', xa, a1), a2)).astype(f32))
    g = jnp.einsum('bsr,re->bse', jax.nn.sigmoid(jnp.einsum('bsd,dr->bsr', xg, g1)), g2)
    heads = lambda t: t.reshape(t.shape[:-1] + (H, N))
    r32, k32, v32 = heads(r.astype(f32)), heads(k.astype(f32)), heads(v.astype(f32))
    kk = heads((k * k_k).astype(f32))
    kk = kk * lax.rsqrt(jnp.maximum(jnp.sum(kk * kk, -1, keepdims=True), 1e-12))
    a_h, decay_h = heads(a), heads(decay)
    k_dir = k32[None] * (1.0 + (a_h - 1.0) * heads(k_a.astype(f32)))
    b_dir = kk[None] * a_h
    y = (wkv_scan(r32, decay_h[0], k_dir[0], v32, kk, b_dir[0], False)
         + wkv_scan(r32, decay_h[1], k_dir[1], v32, kk, b_dir[1], True))
    mean = jnp.mean(y, -1, keepdims=True)
    var = jnp.mean(jnp.square(y - mean), -1, keepdims=True)
    y = ((y - mean) * lax.rsqrt(var + GN_EPS)).reshape(B, S, D) * ln_g + ln_b
    bonus = (jnp.sum(r32 * k32 * r_k, -1, keepdims=True) * v32).reshape(B, S, D)
    out = ((y + bonus) * g.astype(f32)).astype(x.dtype)
    return out @ w_o


def dilated_attention(x, w_in, q_gain, k_gain, w_o, rel_bias):
    B, S, D = x.shape
    qkv = jnp.einsum('bsd,de->bse', x, w_in).reshape(B, S, 3, N_GROUPS, GROUP_HEADS, HEAD_DIM)
    q = rms_head(qkv[:, :, 0], q_gain)
    k = rms_head(qkv[:, :, 1], k_gain)
    v = qkv[:, :, 2]
    q_g = [q[:, :, g] for g in range(N_GROUPS)]
    k_g = [k[:, :, g] for g in range(N_GROUPS)]
    v_g = [v[:, :, g] for g in range(N_GROUPS)]
    bias = rel_bias.reshape(REL_BUCKETS, N_GROUPS, GROUP_HEADS)
    scale = HEAD_DIM ** -0.5
    pos = jnp.arange(Q_BLOCK)

    def block(q0):
        outs, lses = [], []
        for g, (window, dil) in enumerate(ATTN_GROUPS):
            side = window // (2 * dil)
            offs = dil * jnp.arange(-side, side + 1)
            b_g = bias[t5_bucket(offs), g].T
            idx = (q0 + pos)[:, None] + offs[None, :]
            valid = (idx >= 0) & (idx < S)
            idx = jnp.clip(idx, 0, S - 1)
            qb = lax.dynamic_slice_in_dim(q_g[g], q0, Q_BLOCK, axis=1)
            kb = jnp.take(k_g[g], idx, axis=1)
            vb = jnp.take(v_g[g], idx, axis=1)
            s = jnp.einsum('bqhd,bqjhd->bhqj', qb, kb).astype(jnp.float32) * scale + b_g[None, :, None, :].astype(jnp.float32)
            s = jnp.where(valid[None, None], s, -1e30)
            m = jnp.max(s, -1, keepdims=True)
            p = jnp.exp(s - m)
            den = jnp.sum(p, -1, keepdims=True)
            outs.append(jnp.einsum('bhqj,bqjhd->bqhd', (p / den).astype(vb.dtype), vb))
            lses.append((m + jnp.log(den))[..., 0])
        alpha = jax.nn.softmax(jnp.stack(lses, 0), axis=0)
        o = jnp.stack(outs, axis=2)
        o = o * jnp.transpose(alpha, (1, 3, 0, 2))[..., None].astype(o.dtype)
        return o.reshape(B, Q_BLOCK, ATTN_WIDTH)

    o = lax.map(block, jnp.arange(0, S, Q_BLOCK))
    o = jnp.moveaxis(o, 0, 1).reshape(B, S, ATTN_WIDTH)
    return o @ w_o


def swiglu(x, w_gate, w_up, w_down):
    return (jax.nn.silu(x @ w_gate) * (x @ w_up)) @ w_down


def moe_swiglu(x, router, w_gate, w_up, w_down):
    B, S, D = x.shape
    t = x.reshape(B * S, D)
    logits = (t @ router).astype(jnp.float32)
    top_val, top_idx = lax.top_k(logits, TOP_K)
    gates = jax.nn.softmax(top_val, axis=-1)
    combine = jnp.sum(jax.nn.one_hot(top_idx, N_EXPERTS, dtype=jnp.float32) * gates[..., None], axis=1)
    y = jnp.zeros_like(t)
    for e in range(N_EXPERTS):
        y = y + combine[:, e:e + 1].astype(t.dtype) * swiglu(t, w_gate[e], w_up[e], w_down[e])
    return y.reshape(B, S, D)


def trunk(x, norm_mix, norm_ffn, rwkv_mu, rwkv_w_rkv, rwkv_w0, rwkv_w1, rwkv_w2, rwkv_a0, rwkv_a1, rwkv_a2,
          rwkv_g1, rwkv_g2, rwkv_k_k, rwkv_k_a, rwkv_r_k, rwkv_ln_g, rwkv_ln_b, rwkv_w_o,
          attn_w_in, attn_q_gain, attn_k_gain, attn_w_o, rel_bias,
          ffn_w_gate, ffn_w_up, ffn_w_down, moe_router, moe_w_gate, moe_w_up, moe_w_down):
    for i in range(DEPTH):
        j = i // 2
        h = rms_norm(x, norm_mix[i])
        if i % 2 == 0:
            x = x + rwkv7_time_mix(h, rwkv_mu[j], rwkv_w_rkv[j], rwkv_w0[j], rwkv_w1[j], rwkv_w2[j],
                                   rwkv_a0[j], rwkv_a1[j], rwkv_a2[j], rwkv_g1[j], rwkv_g2[j],
                                   rwkv_k_k[j], rwkv_k_a[j], rwkv_r_k[j], rwkv_ln_g[j], rwkv_ln_b[j], rwkv_w_o[j])
        else:
            x = x + dilated_attention(h, attn_w_in[j], attn_q_gain[j], attn_k_gain[j], attn_w_o[j], rel_bias)
        h = rms_norm(x, norm_ffn[i])
        if i % 2 == 0:
            x = x + swiglu(h, ffn_w_gate[j], ffn_w_up[j], ffn_w_down[j])
        else:
            x = x + moe_swiglu(h, moe_router[j], moe_w_gate[j], moe_w_up[j], moe_w_down[j])
    return x


def setup_inputs(seed: int = 0) -> dict:
    key = jax.random.key(seed)
    ks = jax.random.split(key, 32)
    f32 = jnp.float32
    nrm = lambda k, shape, s: jax.random.normal(k, shape, f32) * s
    D, H, N = D_MODEL, RWKV_HEADS, RWKV_HEAD
    return {
        "x_prompt": nrm(ks[0], (BATCH, SEQ, D), 1.0),
        "x_sample": nrm(ks[1], (DEC_BATCH, DEC_SEQ, D), 1.0),
        "norm_mix": 1.0 + nrm(ks[2], (DEPTH, D), 0.1),
        "norm_ffn": 1.0 + nrm(ks[3], (DEPTH, D), 0.1),
        "rwkv_mu": jax.random.uniform(ks[4], (N_RWKV, 6, D), f32),
        "rwkv_w_rkv": nrm(ks[5], (N_RWKV, 3, D, D), D ** -0.5),
        "rwkv_w0": nrm(ks[6], (N_RWKV, 2, D), 1.0),
        "rwkv_w1": nrm(ks[7], (N_RWKV, 2, D, DECAY_LORA), D ** -0.5),
        "rwkv_w2": nrm(ks[8], (N_RWKV, 2, DECAY_LORA, D), 0.5 * DECAY_LORA ** -0.5),
        "rwkv_a0": nrm(ks[9], (N_RWKV, 2, D), 0.5),
        "rwkv_a1": nrm(ks[10], (N_RWKV, 2, D, AAA_LORA), D ** -0.5),
        "rwkv_a2": nrm(ks[11], (N_RWKV, 2, AAA_LORA, D), 0.5 * AAA_LORA ** -0.5),
        "rwkv_g1": nrm(ks[12], (N_RWKV, D, GATE_LORA), D ** -0.5),
        "rwkv_g2": nrm(ks[13], (N_RWKV, GATE_LORA, D), GATE_LORA ** -0.5),
        "rwkv_k_k": 1.0 + nrm(ks[14], (N_RWKV, D), 0.1),
        "rwkv_k_a": 1.0 + nrm(ks[15], (N_RWKV, D), 0.1),
        "rwkv_r_k": nrm(ks[16], (N_RWKV, H, N), 0.1),
        "rwkv_ln_g": 1.0 + nrm(ks[17], (N_RWKV, D), 0.1),
        "rwkv_ln_b": nrm(ks[18], (N_RWKV, D), 0.1),
        "rwkv_w_o": nrm(ks[19], (N_RWKV, D, D), D ** -0.5),
        "attn_w_in": nrm(ks[20], (N_ATTN, D, 3 * ATTN_WIDTH), D ** -0.5),
        "attn_q_gain": 1.0 + nrm(ks[21], (N_ATTN, N_GROUPS, HEAD_DIM), 0.1),
        "attn_k_gain": 1.0 + nrm(ks[22], (N_ATTN, N_GROUPS, HEAD_DIM), 0.1),
        "attn_w_o": nrm(ks[23], (N_ATTN, ATTN_WIDTH, D), ATTN_WIDTH ** -0.5),
        "rel_bias": nrm(ks[24], (REL_BUCKETS, N_GROUPS * GROUP_HEADS), 0.5),
        "ffn_w_gate": nrm(ks[25], (N_RWKV, D, D_FF), D ** -0.5),
        "ffn_w_up": nrm(ks[26], (N_RWKV, D, D_FF), D ** -0.5),
        "ffn_w_down": nrm(ks[27], (N_RWKV, D_FF, D), D_FF ** -0.5),
        "moe_router": nrm(ks[28], (N_ATTN, D, N_EXPERTS), D ** -0.5),
        "moe_w_gate": nrm(ks[29], (N_ATTN, N_EXPERTS, D, D_FF_EXPERT), D ** -0.5),
        "moe_w_up": nrm(ks[30], (N_ATTN, N_EXPERTS, D, D_FF_EXPERT), D ** -0.5),
        "moe_w_down": nrm(ks[31], (N_ATTN, N_EXPERTS, D_FF_EXPERT, D), D_FF_EXPERT ** -0.5),
    }


def reference(x_prompt, x_sample, norm_mix, norm_ffn, rwkv_mu, rwkv_w_rkv, rwkv_w0, rwkv_w1, rwkv_w2,
              rwkv_a0, rwkv_a1, rwkv_a2, rwkv_g1, rwkv_g2, rwkv_k_k, rwkv_k_a, rwkv_r_k, rwkv_ln_g, rwkv_ln_b,
              rwkv_w_o, attn_w_in, attn_q_gain, attn_k_gain, attn_w_o, rel_bias,
              ffn_w_gate, ffn_w_up, ffn_w_down, moe_router, moe_w_gate, moe_w_up, moe_w_down):
    weights = (norm_mix, norm_ffn, rwkv_mu, rwkv_w_rkv, rwkv_w0, rwkv_w1, rwkv_w2, rwkv_a0, rwkv_a1, rwkv_a2,
               rwkv_g1, rwkv_g2, rwkv_k_k, rwkv_k_a, rwkv_r_k, rwkv_ln_g, rwkv_ln_b, rwkv_w_o,
               attn_w_in, attn_q_gain, attn_k_gain, attn_w_o, rel_bias,
               ffn_w_gate, ffn_w_up, ffn_w_down, moe_router, moe_w_gate, moe_w_up, moe_w_down)
    y_prompt = trunk(x_prompt, *weights)
    y_sample = trunk(x_sample, *weights)
    return (y_prompt, y_sample)
```

```python
import functools
import math

import numpy as np
import jax
import jax.numpy as jnp
from jax import lax
from jax.experimental import pallas as pl
from jax.experimental.pallas import tpu as pltpu

F32 = jnp.float32
BF16 = jnp.bfloat16

LANES = 128
VMEM_LIMIT_BYTES = 56 * 1024 * 1024

HEAD = 64
RMS_EPS = 1e-6
GN_EPS = 64e-5
ATTN_GROUPS = ((128, 1), (512, 4), (2048, 16))
GROUP_HEADS = 6
GROUP_WIDTH = GROUP_HEADS * HEAD
REL_BUCKETS = 32
REL_MAX_DIST = 1024
Q_BLOCK = 128
KV_HALO = 64
SCAN_CHUNK = 64
NEG_INF = -1e30


def _cparams(*sem):
    return pltpu.CompilerParams(dimension_semantics=sem, vmem_limit_bytes=VMEM_LIMIT_BYTES)


def _const_spec(shape):
    nd = len(shape)
    return pl.BlockSpec(shape, lambda *_: (0,) * nd, pipeline_mode=pl.Buffered(1))


def _split_dot(x, w):
    hi = x.astype(BF16)
    lo = (x - hi.astype(F32)).astype(BF16)
    return (jnp.dot(hi, w, preferred_element_type=F32)
            + jnp.dot(lo, w, preferred_element_type=F32))


def _head_sum(x, e_ref, et_ref):
    return _split_dot(_split_dot(x, e_ref[...]), et_ref[...])


def _rms(x, gain):
    return x * lax.rsqrt(jnp.mean(x * x, axis=-1, keepdims=True) + RMS_EPS) * gain


def _seq_len_at(row0, tp, sp, ss):
    return jnp.where(row0 < tp, sp, ss)


def _head_indicator(width):
    e = np.zeros((width, LANES), np.float32)
    e[np.arange(width), np.arange(width) // HEAD] = 1.0
    return jnp.asarray(e, BF16), jnp.asarray(e.T, BF16)


def _rwkv_pre_kernel(x_ref, xp_ref, xn_ref, gain_ref, mu_ref, wrkv_ref, wl1_ref, w2_ref, a2_ref,
                     g2_ref, w0_ref, a0_ref, kk_gain_ref, ka_ref, rk_ref, e_ref, et_ref,
                     r_out, v_out, kk_out, g_out, bonus_out, logw_out, kdir_out, bdir_out,
                     *, tm, regions):
    tp, sp, ss = regions
    d = x_ref.shape[-1]
    row0 = pl.program_id(0) * tm
    seq = _seq_len_at(row0, tp, sp, ss)
    at_start = lax.rem(row0, seq) == 0
    at_end = lax.rem(row0 + tm, seq) == 0

    gain = gain_ref[...]
    h = _rms(x_ref[...], gain)
    hp = _rms(xp_ref[...], gain)[7:8, :] * jnp.where(at_start, 0.0, 1.0)
    hn = _rms(xn_ref[...], gain)[0:1, :] * jnp.where(at_end, 0.0, 1.0)
    rows = lax.broadcasted_iota(jnp.int32, (tm, d), 0)
    h_prev = jnp.where(rows == 0, hp, pltpu.roll(h, 1, 0))
    h_next = jnp.where(rows == tm - 1, hn, pltpu.roll(h, tm - 1, 0))
    xx = 0.5 * (h_prev + h_next) - h

    def mixed(n):
        return (h + xx * mu_ref[n:n + 1, :]).astype(BF16)

    r = jnp.dot(mixed(0), wrkv_ref[0], preferred_element_type=F32)
    k = jnp.dot(mixed(1), wrkv_ref[1], preferred_element_type=F32)
    v = jnp.dot(mixed(2), wrkv_ref[2], preferred_element_type=F32)

    t_w = jnp.tanh(jnp.dot(mixed(3), wl1_ref[0], preferred_element_type=F32))
    lw = w0_ref[...] + jnp.dot(t_w.astype(BF16), w2_ref[...], preferred_element_type=F32)
    logw = -math.exp(-0.5) * jax.nn.sigmoid(lw)
    t_a = jnp.dot(mixed(4), wl1_ref[1], preferred_element_type=F32)
    a = jax.nn.sigmoid(a0_ref[...] + jnp.dot(t_a.astype(BF16), a2_ref[...], preferred_element_type=F32))
    t_g = jax.nn.sigmoid(jnp.dot(mixed(5), wl1_ref[2], preferred_element_type=F32))
    g = jnp.dot(t_g.astype(BF16), g2_ref[...], preferred_element_type=F32)

    kk = k * kk_gain_ref[...]
    kk = kk * lax.rsqrt(jnp.maximum(_head_sum(kk * kk, e_ref, et_ref), 1e-12))
    bonus = _head_sum(r * k * rk_ref[...], e_ref, et_ref) * v

    r_out[...] = r
    v_out[...] = v
    kk_out[...] = kk
    g_out[...] = g
    bonus_out[...] = bonus
    ka = ka_ref[...]
    for z in range(2):
        a_z = a[:, z * d:(z + 1) * d]
        logw_out[z] = logw[:, z * d:(z + 1) * d]
        kdir_out[z] = k * (1.0 + (a_z - 1.0) * ka)
        bdir_out[z] = kk * a_z


def _block_diag2(m):
    z = jnp.zeros_like(m[0])
    return jnp.concatenate([jnp.concatenate([m[0], z], 1), jnp.concatenate([z, m[1]], 1)], 0)


def _rwkv_pre(x, regions, gain, mu, w_rkv, w0, w1, w2, a0, a1, a2, g1, g2, k_k, k_a, r_k, tm=256):
    t, d = x.shape
    e, et = _head_indicator(d)
    wl1 = jnp.stack([jnp.concatenate([w1[0], w1[1]], 1), jnp.concatenate([a1[0], a1[1]], 1), g1]).astype(BF16)
    w2c = _block_diag2(w2).astype(BF16)
    a2c = _block_diag2(a2).astype(BF16)
    row = lambda p: p.reshape(1, -1).astype(F32)
    tok = pl.BlockSpec((tm, d), lambda i: (i, 0))
    halo_blocks = t // 8
    prev = pl.BlockSpec((8, d), lambda i: (jnp.maximum(i * (tm // 8) - 1, 0), 0))
    nxt = pl.BlockSpec((8, d), lambda i: (jnp.minimum((i + 1) * (tm // 8), halo_blocks - 1), 0))
    dir_tok = pl.BlockSpec((2, tm, d), lambda i: (0, i, 0))
    consts = [row(gain), mu.astype(F32), w_rkv.astype(BF16), wl1, w2c, a2c, g2.astype(BF16),
              row(w0), row(a0), row(k_k), row(k_a), row(r_k), e, et]
    tok_shape = jax.ShapeDtypeStruct((t, d), F32)
    dir_shape = jax.ShapeDtypeStruct((2, t, d), F32)
    return pl.pallas_call(
        functools.partial(_rwkv_pre_kernel, tm=tm, regions=regions),
        grid=(t // tm,),
        in_specs=[tok, prev, nxt] + [_const_spec(c.shape) for c in consts],
        out_specs=[tok] * 5 + [dir_tok] * 3,
        out_shape=[tok_shape] * 5 + [dir_shape] * 3,
        compiler_params=_cparams("arbitrary"),
        name="rwkv_pre",
    )(x, x, x, *consts)


def _dot_hi(a, b, dims):
    return lax.dot_general(a, b, (dims, ((), ())), precision=lax.Precision.HIGHEST,
                           preferred_element_type=F32)


_NN = ((1,), (0,))
_NT = ((1,), (1,))
_TN = ((0,), (0,))


def _wkv_kernel(r_ref, v_ref, kk_ref, logw_ref, kdir_ref, bdir_ref, y_ref, state_ref,
                *, rb, regions, reverse):
    tp, sp, ss = regions
    c = SCAN_CHUNK
    nsteps = pl.num_programs(1)
    step = pl.program_id(1)
    blk = (nsteps - 1 - step) if reverse else step
    row0 = blk * rb
    seq = _seq_len_at(row0, tp, sp, ss)
    first = (lax.rem(row0 + rb, seq) == 0) if reverse else (lax.rem(row0, seq) == 0)

    @pl.when(first)
    def _():
        state_ref[...] = jnp.zeros_like(state_ref)

    ti = lax.broadcasted_iota(jnp.int32, (c, c), 0)
    tj = lax.broadcasted_iota(jnp.int32, (c, c), 1)
    incl = (tj >= ti) if reverse else (tj <= ti)
    incl_f = incl.astype(F32)
    ti2 = lax.broadcasted_iota(jnp.int32, (c, 2 * c), 0)
    tj2 = lax.broadcasted_iota(jnp.int32, (c, 2 * c), 1) & (c - 1)
    incl2 = (tj2 >= ti2) if reverse else (tj2 <= ti2)
    strict2 = (tj2 > ti2) if reverse else (tj2 < ti2)
    lane = lax.broadcasted_iota(jnp.int32, (c, LANES), 1)
    head0 = lane < HEAD
    rr = lax.broadcasted_iota(jnp.int32, (LANES, LANES), 0)
    cc = lax.broadcasted_iota(jnp.int32, (LANES, LANES), 1)
    same_head = (rr < HEAD) == (cc < HEAD)
    eye = (rr == cc).astype(F32)

    def stack_heads(m):
        return jnp.concatenate([jnp.where(head0, m, 0.0), jnp.where(head0, 0.0, m)], 0)

    state = state_ref[...]
    order = range(rb // c - 1, -1, -1) if reverse else range(rb // c)
    for ci in order:
        sl = pl.ds(ci * c, c)
        r, v, kk = r_ref[sl, :], v_ref[sl, :], kk_ref[sl, :]
        lw, kd, bd = logw_ref[sl, :], kdir_ref[sl, :], bdir_ref[sl, :]
        cum = _dot_hi(incl_f, lw, _NN)
        total = jnp.sum(lw, axis=0, keepdims=True)
        r_s = r * jnp.exp(cum)
        kk_s = kk * jnp.exp(cum - lw)
        inv = jnp.exp(-cum)
        k_u, b_u = kd * inv, bd * inv
        tail = jnp.exp(total - cum)
        k_t, b_t = kd * tail, bd * tail

        lhs = jnp.concatenate([r_s, kk_s], 0)
        rhs = jnp.concatenate([stack_heads(k_u), stack_heads(b_u)], 0)
        q = _dot_hi(lhs, rhs, _NT)
        a_rk = jnp.where(incl2, q[0:c, 0:2 * c], 0.0)
        a_rb = jnp.where(incl2, q[0:c, 2 * c:4 * c], 0.0)
        a_kk = jnp.where(strict2, q[c:2 * c, 0:2 * c], 0.0)
        low = jnp.where(strict2, q[c:2 * c, 2 * c:4 * c], 0.0)
        n = -stack_heads(low)
        inv_l = eye + n
        p = n
        for _ in range(int(math.log2(c)) - 1):
            p = _dot_hi(p, p, _NN)
            inv_l = inv_l + _dot_hi(inv_l, p, _NN)

        v_bd = stack_heads(v)
        sk = _dot_hi(lhs, state, _NT)
        rhs_u = sk[c:2 * c] + _dot_hi(a_kk, v_bd, _NN)
        u_bd = _dot_hi(inv_l, stack_heads(rhs_u), _NN)
        y = sk[0:c] + _dot_hi(jnp.concatenate([a_rk, a_rb], 1),
                              jnp.concatenate([v_bd, -u_bd], 0), _NN)
        y_ref[sl, :] = y
        u = u_bd[0:c] + u_bd[c:2 * c]
        upd = _dot_hi(jnp.concatenate([v, u], 0), jnp.concatenate([k_t, -b_t], 0), _TN)
        state = state * jnp.exp(total) + jnp.where(same_head, upd, 0.0)
    state_ref[...] = state


def _wkv(r, v, kk, logw, kdir, bdir, regions, reverse, rb=256):
    t, d = r.shape
    z = 1 if reverse else 0
    nsteps = t // rb

    def row_block(s):
        return (nsteps - 1 - s) if reverse else s

    tok = pl.BlockSpec((rb, LANES), lambda j, s: (row_block(s), j))
    dir_tok = pl.BlockSpec((None, rb, LANES), lambda j, s: (z, row_block(s), j))
    return pl.pallas_call(
        functools.partial(_wkv_kernel, rb=rb, regions=regions, reverse=reverse),
        grid=(d // LANES, nsteps),
        in_specs=[tok, tok, tok, dir_tok, dir_tok, dir_tok],
        out_specs=tok,
        out_shape=jax.ShapeDtypeStruct((t, d), F32),
        scratch_shapes=[pltpu.VMEM((LANES, LANES), F32)],
        compiler_params=_cparams("arbitrary", "arbitrary"),
        name="wkv_bwd" if reverse else "wkv_fwd",
    )(r, v, kk, logw, kdir, bdir)


def _rwkv_post_kernel(yf_ref, yb_ref, bonus_ref, g_ref, x_ref, lng_ref, lnb_ref, wo_ref, e_ref, et_ref, out_ref):
    y = yf_ref[...] + yb_ref[...]
    mean = _head_sum(y, e_ref, et_ref) * (1.0 / HEAD)
    dlt = y - mean
    var = _head_sum(dlt * dlt, e_ref, et_ref) * (1.0 / HEAD)
    yn = dlt * lax.rsqrt(var + GN_EPS) * lng_ref[...] + lnb_ref[...]
    mixed = ((yn + bonus_ref[...]) * g_ref[...]).astype(BF16)
    out_ref[...] = x_ref[...] + jnp.dot(mixed, wo_ref[...], preferred_element_type=F32)


def _rwkv_post(yf, yb, bonus, g, x, ln_g, ln_b, w_o, tm=256):
    t, d = x.shape
    e, et = _head_indicator(d)
    tok = pl.BlockSpec((tm, d), lambda i: (i, 0))
    consts = [ln_g.reshape(1, d), ln_b.reshape(1, d), w_o.astype(BF16), e, et]
    return pl.pallas_call(
        _rwkv_post_kernel,
        grid=(t // tm,),
        in_specs=[tok] * 5 + [_const_spec(c.shape) for c in consts],
        out_specs=tok,
        out_shape=jax.ShapeDtypeStruct((t, d), F32),
        compiler_params=_cparams("arbitrary"),
        name="rwkv_post",
    )(yf, yb, bonus, g, x, *consts)


def _ffn_kernel(x_ref, gain_ref, wg_ref, wu_ref, wd_ref, out_ref, h_ref, acc_ref):
    f = pl.program_id(1)

    @pl.when(f == 0)
    def _():
        h_ref[...] = _rms(x_ref[...], gain_ref[...]).astype(BF16)
        acc_ref[...] = jnp.zeros_like(acc_ref)

    h = h_ref[...]
    gate = jnp.dot(h, wg_ref[...], preferred_element_type=F32)
    up = jnp.dot(h, wu_ref[...], preferred_element_type=F32)
    hidden = (jax.nn.silu(gate) * up).astype(BF16)
    acc_ref[...] += jnp.dot(hidden, wd_ref[...], preferred_element_type=F32)

    @pl.when(f == pl.num_programs(1) - 1)
    def _():
        out_ref[...] = x_ref[...] + acc_ref[...]


def _ffn(x, gain, w_gate, w_up, w_down, tm=512, tf=1408):
    t, d = x.shape
    dff = w_gate.shape[1]
    return pl.pallas_call(
        _ffn_kernel,
        grid=(t // tm, dff // tf),
        in_specs=[pl.BlockSpec((tm, d), lambda i, f: (i, 0)),
                  _const_spec((1, d)),
                  pl.BlockSpec((d, tf), lambda i, f: (0, f)),
                  pl.BlockSpec((d, tf), lambda i, f: (0, f)),
                  pl.BlockSpec((tf, d), lambda i, f: (f, 0))],
        out_specs=pl.BlockSpec((tm, d), lambda i, f: (i, 0)),
        out_shape=jax.ShapeDtypeStruct((t, d), F32),
        scratch_shapes=[pltpu.VMEM((tm, d), BF16), pltpu.VMEM((tm, d), F32)],
        compiler_params=_cparams("arbitrary", "arbitrary"),
        name="ffn",
    )(x, gain.reshape(1, d), w_gate.astype(BF16), w_up.astype(BF16), w_down.astype(BF16))


def _router_kernel(x_ref, gain_ref, wr_ref, h_out, comb_out, *, n_experts):
    h = _rms(x_ref[...], gain_ref[...])
    h_out[...] = h.astype(BF16)
    logits = jnp.dot(h.astype(BF16), wr_ref[...], preferred_element_type=F32)
    lane = lax.broadcasted_iota(jnp.int32, logits.shape, 1)
    logits = jnp.where(lane < n_experts, logits, -jnp.inf)
    m1 = jnp.max(logits, axis=-1, keepdims=True)
    i1 = jnp.min(jnp.where(logits == m1, lane, LANES), axis=-1, keepdims=True)
    rest = jnp.where(lane == i1, -jnp.inf, logits)
    m2 = jnp.max(rest, axis=-1, keepdims=True)
    i2 = jnp.min(jnp.where(rest == m2, lane, LANES), axis=-1, keepdims=True)
    e2 = jnp.exp(m2 - m1)
    g1 = 1.0 / (1.0 + e2)
    g2 = e2 / (1.0 + e2)
    comb_out[...] = jnp.where(lane == i1, g1, 0.0) + jnp.where(lane == i2, g2, 0.0)


def _router(x, gain, w_router, tm=512):
    t, d = x.shape
    n_experts = w_router.shape[1]
    wr = jnp.zeros((d, LANES), F32).at[:, :n_experts].set(w_router).astype(BF16)
    tok = pl.BlockSpec((tm, d), lambda i: (i, 0))
    return pl.pallas_call(
        functools.partial(_router_kernel, n_experts=n_experts),
        grid=(t // tm,),
        in_specs=[tok, _const_spec((1, d)), _const_spec((d, LANES))],
        out_specs=[tok, pl.BlockSpec((tm, LANES), lambda i: (i, 0))],
        out_shape=[jax.ShapeDtypeStruct((t, d), BF16), jax.ShapeDtypeStruct((t, LANES), F32)],
        compiler_params=_cparams("arbitrary"),
        name="moe_router",
    )(x, gain.reshape(1, d), wr)


def _moe_kernel(x_ref, h_ref, comb_ref, wg_ref, wu_ref, wd_ref, out_ref, acc_ref):
    e, f = pl.program_id(1), pl.program_id(2)

    @pl.when((e == 0) & (f == 0))
    def _():
        acc_ref[...] = jnp.zeros_like(acc_ref)

    comb = comb_ref[...]
    lane = lax.broadcasted_iota(jnp.int32, comb.shape, 1)
    weight = jnp.sum(jnp.where(lane == e, comb, 0.0), axis=-1, keepdims=True)
    h = h_ref[...]
    gate = jnp.dot(h, wg_ref[...], preferred_element_type=F32)
    up = jnp.dot(h, wu_ref[...], preferred_element_type=F32)
    hidden = (jax.nn.silu(gate) * up * weight).astype(BF16)
    acc_ref[...] += jnp.dot(hidden, wd_ref[...], preferred_element_type=F32)

    @pl.when((e == pl.num_programs(1) - 1) & (f == pl.num_programs(2) - 1))
    def _():
        out_ref[...] = x_ref[...] + acc_ref[...]


def _moe(x, h, comb, w_gate, w_up, w_down, tm=512, tf=512):
    t, d = x.shape
    n_experts, _, dff = w_gate.shape
    tok = lambda w: pl.BlockSpec((tm, w), lambda i, e, f: (i, 0))
    return pl.pallas_call(
        _moe_kernel,
        grid=(t // tm, n_experts, dff // tf),
        in_specs=[tok(d), tok(d), tok(LANES),
                  pl.BlockSpec((None, d, tf), lambda i, e, f: (e, 0, f)),
                  pl.BlockSpec((None, d, tf), lambda i, e, f: (e, 0, f)),
                  pl.BlockSpec((None, tf, d), lambda i, e, f: (e, f, 0))],
        out_specs=tok(d),
        out_shape=jax.ShapeDtypeStruct((t, d), F32),
        scratch_shapes=[pltpu.VMEM((tm, d), F32)],
        compiler_params=_cparams("arbitrary", "arbitrary", "arbitrary"),
        name="moe_experts",
    )(x, h, comb, w_gate.astype(BF16), w_up.astype(BF16), w_down.astype(BF16))


def _attn_in_kernel(x_ref, gain_ref, w_ref, hg_ref, e_ref, et_ref, out_ref, h_ref, *, n_norm):
    j = pl.program_id(1)

    @pl.when(j == 0)
    def _():
        h_ref[...] = _rms(x_ref[...], gain_ref[...]).astype(BF16)

    y = jnp.dot(h_ref[...], w_ref[...], preferred_element_type=F32)

    @pl.when(j < n_norm)
    def _():
        ms = _head_sum(y * y, e_ref, et_ref) * (1.0 / HEAD)
        out_ref[...] = (y * lax.rsqrt(ms + RMS_EPS) * hg_ref[...]).astype(out_ref.dtype)

    @pl.when(j >= n_norm)
    def _():
        out_ref[...] = y.astype(out_ref.dtype)


def _attn_in(x, gain, w_in, q_gain, k_gain, tm=512):
    t, d = x.shape
    width = w_in.shape[1]
    ntiles = width // GROUP_WIDTH
    n_groups = q_gain.shape[0]
    head_gain = jnp.concatenate([jnp.tile(q_gain, (1, GROUP_HEADS)), jnp.tile(k_gain, (1, GROUP_HEADS)),
                                 jnp.ones((n_groups, GROUP_WIDTH), F32)], 0).reshape(ntiles, 1, GROUP_WIDTH)
    e, et = _head_indicator(GROUP_WIDTH)
    return pl.pallas_call(
        functools.partial(_attn_in_kernel, n_norm=2 * n_groups),
        grid=(t // tm, ntiles),
        in_specs=[pl.BlockSpec((tm, d), lambda i, j: (i, 0)),
                  _const_spec((1, d)),
                  pl.BlockSpec((d, GROUP_WIDTH), lambda i, j: (0, j)),
                  pl.BlockSpec((None, 1, GROUP_WIDTH), lambda i, j: (j, 0, 0)),
                  _const_spec(e.shape), _const_spec(et.shape)],
        out_specs=pl.BlockSpec((tm, GROUP_WIDTH), lambda i, j: (i, j)),
        out_shape=jax.ShapeDtypeStruct((t, width), BF16),
        scratch_shapes=[pltpu.VMEM((tm, d), BF16)],
        compiler_params=_cparams("arbitrary", "arbitrary"),
        name="attn_in",
    )(x, gain.reshape(1, d), w_in.astype(BF16), head_gain, e, et)


def _t5_bucket_np(rel):
    nb = REL_BUCKETS // 2
    max_exact = nb // 2
    n = np.abs(rel)
    large = max_exact + (np.log(np.maximum(n, 1).astype(np.float32) / max_exact)
                         / math.log(REL_MAX_DIST / max_exact) * (nb - max_exact)).astype(np.int32)
    large = np.minimum(large, nb - 1)
    return np.where(rel > 0, nb, 0) + np.where(n < max_exact, n, large)


def _band_bias(rel_bias, group, window, dil):
    side = window // (2 * dil)
    assert side == KV_HALO
    offs = dil * np.arange(-side, side + 1)
    buckets = _t5_bucket_np(offs)
    table = rel_bias.reshape(REL_BUCKETS, -1, GROUP_HEADS)[buckets, group].T
    qi = np.arange(Q_BLOCK)[:, None]
    kj = np.arange(Q_BLOCK + 2 * KV_HALO)[None, :]
    o = kj - qi
    inside = (o >= 0) & (o <= 2 * side)
    return jnp.where(jnp.asarray(inside)[None], table[:, np.clip(o, 0, 2 * side)], NEG_INF).astype(F32)


def _attn_kernel(q_ref, kp_ref, kc_ref, kn_ref, vp_ref, vc_ref, vn_ref, bias_ref, o_ref, lse_ref,
                 *, regions, dil):
    tp, sp, ss = regions
    row0 = pl.program_id(1) * Q_BLOCK
    seq = _seq_len_at(row0, tp // dil, sp // dil, ss // dil)
    seq_start = row0 - lax.rem(row0, seq)
    nk = Q_BLOCK + 2 * KV_HALO
    key_row = row0 - KV_HALO + lax.broadcasted_iota(jnp.int32, (1, nk), 1)
    valid = (key_row >= seq_start) & (key_row < seq_start + seq)
    lane = lax.broadcasted_iota(jnp.int32, (Q_BLOCK, LANES), 1)
    scale = HEAD ** -0.5

    q = q_ref[...]
    k = jnp.concatenate([kp_ref[...], kc_ref[...], kn_ref[...]], 0)
    v = jnp.concatenate([vp_ref[...], vc_ref[...], vn_ref[...]], 0)
    for pair in range(GROUP_HEADS // 2):
        ls = slice(pair * LANES, (pair + 1) * LANES)
        qp, kpair, vpair = q[:, ls], k[:, ls], v[:, ls]
        o_pair = jnp.zeros((Q_BLOCK, LANES), F32)
        lse_pair = jnp.zeros((Q_BLOCK, LANES), F32)
        for sub in range(2):
            mine = (lane < HEAD) if sub == 0 else (lane >= HEAD)
            qm = jnp.where(mine, qp, jnp.zeros_like(qp))
            s = lax.dot_general(qm, kpair, (_NT, ((), ())), preferred_element_type=F32)
            s = s * scale + bias_ref[2 * pair + sub]
            s = jnp.where(valid, s, NEG_INF)
            m = jnp.max(s, axis=-1, keepdims=True)
            p = jnp.exp(s - m)
            den = jnp.sum(p, axis=-1, keepdims=True)
            pn = (p / den).astype(v.dtype)
            o = jnp.dot(pn, vpair, preferred_element_type=F32)
            o_pair = jnp.where(mine, o, o_pair)
            lse_pair = jnp.where(mine, m + jnp.log(den), lse_pair)
        o_ref[:, ls] = o_pair.astype(o_ref.dtype)
        lse_ref[:, ls] = lse_pair


def _attn_group(qkv, rel_bias, group, regions):
    window, dil = ATTN_GROUPS[group]
    t, width = qkv.shape
    n_groups = len(ATTN_GROUPS)
    rows = t // dil
    qkv_d = qkv.reshape(rows, dil * width)
    tiles = width // GROUP_WIDTH
    bias = _band_bias(rel_bias, group, window, dil)
    halo_per_q = Q_BLOCK // KV_HALO
    last_halo = rows // KV_HALO - 1

    def cur(part):
        return pl.BlockSpec((Q_BLOCK, GROUP_WIDTH), lambda rho, m: (m, rho * tiles + part * n_groups + group))

    def prev(part):
        return pl.BlockSpec((KV_HALO, GROUP_WIDTH),
                            lambda rho, m: (jnp.maximum(m * halo_per_q - 1, 0), rho * tiles + part * n_groups + group))

    def nxt(part):
        return pl.BlockSpec((KV_HALO, GROUP_WIDTH),
                            lambda rho, m: (jnp.minimum((m + 1) * halo_per_q, last_halo),
                                            rho * tiles + part * n_groups + group))

    out_spec = pl.BlockSpec((Q_BLOCK, GROUP_WIDTH), lambda rho, m: (m, rho))
    o, lse = pl.pallas_call(
        functools.partial(_attn_kernel, regions=regions, dil=dil),
        grid=(dil, rows // Q_BLOCK),
        in_specs=[cur(0), prev(1), cur(1), nxt(1), prev(2), cur(2), nxt(2), _const_spec(bias.shape)],
        out_specs=[out_spec, out_spec],
        out_shape=[jax.ShapeDtypeStruct((rows, dil * GROUP_WIDTH), BF16),
                   jax.ShapeDtypeStruct((rows, dil * GROUP_WIDTH), F32)],
        compiler_params=_cparams("arbitrary", "arbitrary"),
        name=f"attn_g{group}",
    )(qkv_d, qkv_d, qkv_d, qkv_d, qkv_d, qkv_d, qkv_d, bias)
    return o.reshape(t, GROUP_WIDTH), lse.reshape(t, GROUP_WIDTH)


def _attn_out_kernel(o0, o1, o2, l0, l1, l2, x_ref, wo_ref, out_ref):
    lses = [l0[...], l1[...], l2[...]]
    mx = jnp.maximum(jnp.maximum(lses[0], lses[1]), lses[2])
    ws = [jnp.exp(l - mx) for l in lses]
    tot = ws[0] + ws[1] + ws[2]
    acc = x_ref[...]
    for g, o_ref in enumerate((o0, o1, o2)):
        scaled = (o_ref[...].astype(F32) * (ws[g] / tot)).astype(BF16)
        acc = acc + jnp.dot(scaled, wo_ref[g], preferred_element_type=F32)
    out_ref[...] = acc


def _attn_out(os_, lses, x, w_o, tm=512):
    t, d = x.shape
    n_groups = len(os_)
    grp = pl.BlockSpec((tm, GROUP_WIDTH), lambda i: (i, 0))
    tok = pl.BlockSpec((tm, d), lambda i: (i, 0))
    wo = w_o.reshape(n_groups, GROUP_WIDTH, d).astype(BF16)
    return pl.pallas_call(
        _attn_out_kernel,
        grid=(t // tm,),
        in_specs=[grp] * (2 * n_groups) + [tok, _const_spec(wo.shape)],
        out_specs=tok,
        out_shape=jax.ShapeDtypeStruct((t, d), F32),
        compiler_params=_cparams("arbitrary"),
        name="attn_out",
    )(*os_, *lses, x, wo)


def kernel(x_prompt, x_sample, norm_mix, norm_ffn, rwkv_mu, rwkv_w_rkv, rwkv_w0, rwkv_w1, rwkv_w2, rwkv_a0, rwkv_a1, rwkv_a2, rwkv_g1, rwkv_g2, rwkv_k_k, rwkv_k_a, rwkv_r_k, rwkv_ln_g, rwkv_ln_b, rwkv_w_o, attn_w_in, attn_q_gain, attn_k_gain, attn_w_o, rel_bias, ffn_w_gate, ffn_w_up, ffn_w_down, moe_router, moe_w_gate, moe_w_up, moe_w_down):
    d = x_prompt.shape[-1]
    tp = x_prompt.shape[0] * x_prompt.shape[1]
    regions = (tp, x_prompt.shape[1], x_sample.shape[1])
    x = jnp.concatenate([x_prompt.reshape(-1, d), x_sample.reshape(-1, d)], 0)
    depth = norm_mix.shape[0]
    for i in range(depth):
        j = i // 2
        if i % 2 == 0:
            r, v, kk, g, bonus, logw, kdir, bdir = _rwkv_pre(
                x, regions, norm_mix[i], rwkv_mu[j], rwkv_w_rkv[j], rwkv_w0[j], rwkv_w1[j], rwkv_w2[j],
                rwkv_a0[j], rwkv_a1[j], rwkv_a2[j], rwkv_g1[j], rwkv_g2[j], rwkv_k_k[j], rwkv_k_a[j], rwkv_r_k[j])
            yf = _wkv(r, v, kk, logw, kdir, bdir, regions, reverse=False)
            yb = _wkv(r, v, kk, logw, kdir, bdir, regions, reverse=True)
            x = _rwkv_post(yf, yb, bonus, g, x, rwkv_ln_g[j], rwkv_ln_b[j], rwkv_w_o[j])
            x = _ffn(x, norm_ffn[i], ffn_w_gate[j], ffn_w_up[j], ffn_w_down[j])
        else:
            qkv = _attn_in(x, norm_mix[i], attn_w_in[j], attn_q_gain[j], attn_k_gain[j])
            outs = [_attn_group(qkv, rel_bias, g, regions) for g in range(len(ATTN_GROUPS))]
            x = _attn_out([o for o, _ in outs], [l for _, l in outs], x, attn_w_o[j])
            h, comb = _router(x, norm_ffn[i], moe_router[j])
            x = _moe(x, h, comb, moe_w_gate[j], moe_w_up[j], moe_w_down[j])
    y_prompt = x[:tp].reshape(x_prompt.shape)
    y_sample = x[tp:].reshape(x_sample.shape)
    return (y_prompt, y_sample)
```

```python
import functools
import math

import numpy as np
import jax
import jax.numpy as jnp
from jax import lax
from jax.experimental import pallas as pl
from jax.experimental.pallas import tpu as pltpu

F32 = jnp.float32
BF16 = jnp.bfloat16

LANES = 128
VMEM_LIMIT_BYTES = 56 * 1024 * 1024

HEAD = 64
RMS_EPS = 1e-6
GN_EPS = 64e-5
ATTN_GROUPS = ((128, 1), (512, 4), (2048, 16))
GROUP_HEADS = 6
GROUP_WIDTH = GROUP_HEADS * HEAD
REL_BUCKETS = 32
REL_MAX_DIST = 1024
Q_BLOCK = 128
KV_HALO = 64
SCAN_CHUNK = 64
NEG_INF = -1e30


def _cparams(*sem):
    return pltpu.CompilerParams(dimension_semantics=sem, vmem_limit_bytes=VMEM_LIMIT_BYTES)


def _const_spec(shape):
    nd = len(shape)
    return pl.BlockSpec(shape, lambda *_: (0,) * nd, pipeline_mode=pl.Buffered(1))


def _split_dot(x, w):
    hi = x.astype(BF16)
    lo = (x - hi.astype(F32)).astype(BF16)
    return (jnp.dot(hi, w, preferred_element_type=F32)
            + jnp.dot(lo, w, preferred_element_type=F32))


def _head_sum(x, e_ref, et_ref):
    return _split_dot(_split_dot(x, e_ref[...]), et_ref[...])


def _rms(x, gain):
    return x * lax.rsqrt(jnp.mean(x * x, axis=-1, keepdims=True) + RMS_EPS) * gain


def _seq_len_at(row0, tp, sp, ss):
    return jnp.where(row0 < tp, sp, ss)


def _head_indicator(width):
    e = np.zeros((width, LANES), np.float32)
    e[np.arange(width), np.arange(width) // HEAD] = 1.0
    return jnp.asarray(e, BF16), jnp.asarray(e.T, BF16)


def _rwkv_pre_kernel(x_ref, xp_ref, xn_ref, gain_ref, mu_ref, wrkv_ref, wl1_ref, w2_ref, a2_ref,
                     g2_ref, w0_ref, a0_ref, kk_gain_ref, ka_ref, rk_ref, e_ref, et_ref,
                     r_out, v_out, kk_out, g_out, bonus_out, logw_out, kdir_out, bdir_out,
                     *, tm, regions):
    tp, sp, ss = regions
    d = x_ref.shape[-1]
    row0 = pl.program_id(0) * tm
    seq = _seq_len_at(row0, tp, sp, ss)
    at_start = lax.rem(row0, seq) == 0
    at_end = lax.rem(row0 + tm, seq) == 0

    gain = gain_ref[...]
    h = _rms(x_ref[...], gain)
    hp = _rms(xp_ref[...], gain)[7:8, :] * jnp.where(at_start, 0.0, 1.0)
    hn = _rms(xn_ref[...], gain)[0:1, :] * jnp.where(at_end, 0.0, 1.0)
    rows = lax.broadcasted_iota(jnp.int32, (tm, d), 0)
    h_prev = jnp.where(rows == 0, hp, pltpu.roll(h, 1, 0))
    h_next = jnp.where(rows == tm - 1, hn, pltpu.roll(h, tm - 1, 0))
    xx = 0.5 * (h_prev + h_next) - h

    def mixed(n):
        return (h + xx * mu_ref[n:n + 1, :]).astype(BF16)

    r = jnp.dot(mixed(0), wrkv_ref[0], preferred_element_type=F32)
    k = jnp.dot(mixed(1), wrkv_ref[1], preferred_element_type=F32)
    v = jnp.dot(mixed(2), wrkv_ref[2], preferred_element_type=F32)

    t_w = jnp.tanh(jnp.dot(mixed(3), wl1_ref[0], preferred_element_type=F32))
    lw = w0_ref[...] + jnp.dot(t_w.astype(BF16), w2_ref[...], preferred_element_type=F32)
    logw = -math.exp(-0.5) * jax.nn.sigmoid(lw)
    t_a = jnp.dot(mixed(4), wl1_ref[1], preferred_element_type=F32)
    a = jax.nn.sigmoid(a0_ref[...] + jnp.dot(t_a.astype(BF16), a2_ref[...], preferred_element_type=F32))
    t_g = jax.nn.sigmoid(jnp.dot(mixed(5), wl1_ref[2], preferred_element_type=F32))
    g = jnp.dot(t_g.astype(BF16), g2_ref[...], preferred_element_type=F32)

    kk = k * kk_gain_ref[...]
    kk = kk * lax.rsqrt(jnp.maximum(_head_sum(kk * kk, e_ref, et_ref), 1e-12))
    bonus = _head_sum(r * k * rk_ref[...], e_ref, et_ref) * v

    r_out[...] = r
    v_out[...] = v
    kk_out[...] = kk
    g_out[...] = g
    bonus_out[...] = bonus
    ka = ka_ref[...]
    for z in range(2):
        a_z = a[:, z * d:(z + 1) * d]
        logw_out[z] = logw[:, z * d:(z + 1) * d]
        kdir_out[z] = k * (1.0 + (a_z - 1.0) * ka)
        bdir_out[z] = kk * a_z


def _block_diag2(m):
    z = jnp.zeros_like(m[0])
    return jnp.concatenate([jnp.concatenate([m[0], z], 1), jnp.concatenate([z, m[1]], 1)], 0)


def _rwkv_pre(x, regions, gain, mu, w_rkv, w0, w1, w2, a0, a1, a2, g1, g2, k_k, k_a, r_k, tm=256):
    t, d = x.shape
    e, et = _head_indicator(d)
    wl1 = jnp.stack([jnp.concatenate([w1[0], w1[1]], 1), jnp.concatenate([a1[0], a1[1]], 1), g1]).astype(BF16)
    w2c = _block_diag2(w2).astype(BF16)
    a2c = _block_diag2(a2).astype(BF16)
    row = lambda p: p.reshape(1, -1).astype(F32)
    tok = pl.BlockSpec((tm, d), lambda i: (i, 0))
    halo_blocks = t // 8
    prev = pl.BlockSpec((8, d), lambda i: (jnp.maximum(i * (tm // 8) - 1, 0), 0))
    nxt = pl.BlockSpec((8, d), lambda i: (jnp.minimum((i + 1) * (tm // 8), halo_blocks - 1), 0))
    dir_tok = pl.BlockSpec((2, tm, d), lambda i: (0, i, 0))
    consts = [row(gain), mu.astype(F32), w_rkv.astype(BF16), wl1, w2c, a2c, g2.astype(BF16),
              row(w0), row(a0), row(k_k), row(k_a), row(r_k), e, et]
    tok_shape = jax.ShapeDtypeStruct((t, d), F32)
    dir_shape = jax.ShapeDtypeStruct((2, t, d), F32)
    return pl.pallas_call(
        functools.partial(_rwkv_pre_kernel, tm=tm, regions=regions),
        grid=(t // tm,),
        in_specs=[tok, prev, nxt] + [_const_spec(c.shape) for c in consts],
        out_specs=[tok] * 5 + [dir_tok] * 3,
        out_shape=[tok_shape] * 5 + [dir_shape] * 3,
        compiler_params=_cparams("arbitrary"),
        name="rwkv_pre",
    )(x, x, x, *consts)


_NN = ((1,), (0,))
_NT = ((1,), (1,))
_TN = ((0,), (0,))


def _mm(a, b, dims=_NN):
    return lax.dot_general(a.astype(BF16), b.astype(BF16), (dims, ((), ())), preferred_element_type=F32)


def _cumsum_mm(tri, x):
    x0 = x.astype(BF16)
    r1 = x - x0.astype(F32)
    x1 = r1.astype(BF16)
    x2 = (r1 - x1.astype(F32)).astype(BF16)
    return (jnp.dot(tri, x0, preferred_element_type=F32) + jnp.dot(tri, x1, preferred_element_type=F32)
            + jnp.dot(tri, x2, preferred_element_type=F32))


def _wkv_kernel(r_ref, v_ref, kk_ref, logw_ref, kdir_ref, bdir_ref, y_ref, state_ref,
                *, rb, regions, reverse):
    tp, sp, ss = regions
    c = SCAN_CHUNK
    npairs = r_ref.shape[-1] // LANES
    nsteps = pl.num_programs(1)
    step = pl.program_id(1)
    blk = (nsteps - 1 - step) if reverse else step
    row0 = blk * rb
    seq = _seq_len_at(row0, tp, sp, ss)
    first = (lax.rem(row0 + rb, seq) == 0) if reverse else (lax.rem(row0, seq) == 0)

    @pl.when(first)
    def _():
        state_ref[...] = jnp.zeros_like(state_ref)

    ti = lax.broadcasted_iota(jnp.int32, (c, c), 0)
    tj = lax.broadcasted_iota(jnp.int32, (c, c), 1)
    incl = (tj >= ti) if reverse else (tj <= ti)
    incl_b = incl.astype(BF16)
    ti2 = lax.broadcasted_iota(jnp.int32, (c, 2 * c), 0)
    tj2 = lax.broadcasted_iota(jnp.int32, (c, 2 * c), 1) & (c - 1)
    ti4 = lax.broadcasted_iota(jnp.int32, (c, 4 * c), 0)
    tj4 = lax.broadcasted_iota(jnp.int32, (c, 4 * c), 1) & (c - 1)
    incl4 = (tj4 >= ti4) if reverse else (tj4 <= ti4)
    strict2 = (tj2 > ti2) if reverse else (tj2 < ti2)
    lane = lax.broadcasted_iota(jnp.int32, (c, LANES), 1)
    head0 = lane < HEAD
    rr = lax.broadcasted_iota(jnp.int32, (LANES, LANES), 0)
    cc = lax.broadcasted_iota(jnp.int32, (LANES, LANES), 1)
    same_head = (rr < HEAD) == (cc < HEAD)
    eye = (rr == cc).astype(F32)

    def stack_heads(m):
        return jnp.concatenate([jnp.where(head0, m, 0.0), jnp.where(head0, 0.0, m)], 0)

    order = list(range(rb // c - 1, -1, -1) if reverse else range(rb // c))
    units = [(pl.ds(ci * c, c), pl.ds(p * LANES, LANES)) for ci in order for p in range(npairs)]
    each = lambda fn, *lists: [fn(*xs) for xs in zip(*lists)]

    lw = [logw_ref[u] for u in units]
    cum = each(lambda x: _cumsum_mm(incl_b, x), lw)
    total = each(lambda x: jnp.sum(x, axis=0, keepdims=True), lw)
    lhs = [jnp.concatenate([r_ref[u] * jnp.exp(cm), kk_ref[u] * jnp.exp(cm - x)], 0).astype(BF16)
           for u, cm, x in zip(units, cum, lw)]
    rhs = []
    kb_tail = []
    for u, cm, tt in zip(units, cum, total):
        kd, bd = kdir_ref[u], bdir_ref[u]
        inv = jnp.exp(-cm)
        tail = jnp.exp(tt - cm)
        rhs.append(jnp.concatenate([stack_heads(kd * inv), stack_heads(bd * inv)], 0).astype(BF16))
        kb_tail.append(jnp.concatenate([kd * tail, -(bd * tail)], 0).astype(BF16))
    q = each(lambda a, b: _mm(a, b, _NT), lhs, rhs)
    a_out = [jnp.where(incl4, x[0:c], 0.0).astype(BF16) for x in q]
    a_kk = [jnp.where(strict2, x[c:2 * c, 0:2 * c], 0.0).astype(BF16) for x in q]
    pw = [-stack_heads(jnp.where(strict2, x[c:2 * c, 2 * c:4 * c], 0.0)) for x in q]
    inv_l = [eye + x for x in pw]
    for _ in range(int(math.log2(c)) - 1):
        pw = each(lambda x: _mm(x, x), pw)
        inv_l = each(lambda t, x: t + _mm(t, x), inv_l, pw)
    inv_l = [x.astype(BF16) for x in inv_l]
    v_all = [v_ref[u] for u in units]
    v_bd = [stack_heads(x).astype(BF16) for x in v_all]
    akv = each(_mm, a_kk, v_bd)
    decay = [jnp.exp(x) for x in total]

    states = [state_ref[p] for p in range(npairs)]
    for k in range(len(order)):
        idx = list(range(k * npairs, (k + 1) * npairs))
        sk = [_mm(lhs[i], states[p], _NT) for p, i in enumerate(idx)]
        u_bd = [_mm(inv_l[i], stack_heads(s[c:2 * c] + akv[i])) for s, i in zip(sk, idx)]
        y = [s[0:c] + _mm(a_out[i], jnp.concatenate([v_bd[i], (-ub).astype(BF16)], 0))
             for s, ub, i in zip(sk, u_bd, idx)]
        for yy, i in zip(y, idx):
            y_ref[units[i]] = yy
        upd = [_mm(jnp.concatenate([v_all[i], ub[0:c] + ub[c:2 * c]], 0), kb_tail[i], _TN)
               for ub, i in zip(u_bd, idx)]
        states = [states[p] * decay[i] + jnp.where(same_head, upd[p], 0.0) for p, i in enumerate(idx)]
    for p in range(npairs):
        state_ref[p] = states[p]


def _wkv(r, v, kk, logw, kdir, bdir, regions, reverse, rb=256, width=512):
    t, d = r.shape
    z = 1 if reverse else 0
    nsteps = t // rb

    def row_block(s):
        return (nsteps - 1 - s) if reverse else s

    tok = pl.BlockSpec((rb, width), lambda j, s: (row_block(s), j))
    dir_tok = pl.BlockSpec((None, rb, width), lambda j, s: (z, row_block(s), j))
    return pl.pallas_call(
        functools.partial(_wkv_kernel, rb=rb, regions=regions, reverse=reverse),
        grid=(d // width, nsteps),
        in_specs=[tok, tok, tok, dir_tok, dir_tok, dir_tok],
        out_specs=tok,
        out_shape=jax.ShapeDtypeStruct((t, d), F32),
        scratch_shapes=[pltpu.VMEM((width // LANES, LANES, LANES), F32)],
        compiler_params=_cparams("arbitrary", "arbitrary"),
        name="wkv_bwd" if reverse else "wkv_fwd",
    )(r, v, kk, logw, kdir, bdir)


def _rwkv_post_kernel(yf_ref, yb_ref, bonus_ref, g_ref, x_ref, lng_ref, lnb_ref, wo_ref, e_ref, et_ref, out_ref):
    y = yf_ref[...] + yb_ref[...]
    mean = _head_sum(y, e_ref, et_ref) * (1.0 / HEAD)
    dlt = y - mean
    var = _head_sum(dlt * dlt, e_ref, et_ref) * (1.0 / HEAD)
    yn = dlt * lax.rsqrt(var + GN_EPS) * lng_ref[...] + lnb_ref[...]
    mixed = ((yn + bonus_ref[...]) * g_ref[...]).astype(BF16)
    out_ref[...] = x_ref[...] + jnp.dot(mixed, wo_ref[...], preferred_element_type=F32)


def _rwkv_post(yf, yb, bonus, g, x, ln_g, ln_b, w_o, tm=256):
    t, d = x.shape
    e, et = _head_indicator(d)
    tok = pl.BlockSpec((tm, d), lambda i: (i, 0))
    consts = [ln_g.reshape(1, d), ln_b.reshape(1, d), w_o.astype(BF16), e, et]
    return pl.pallas_call(
        _rwkv_post_kernel,
        grid=(t // tm,),
        in_specs=[tok] * 5 + [_const_spec(c.shape) for c in consts],
        out_specs=tok,
        out_shape=jax.ShapeDtypeStruct((t, d), F32),
        compiler_params=_cparams("arbitrary"),
        name="rwkv_post",
    )(yf, yb, bonus, g, x, *consts)


def _ffn_kernel(x_ref, gain_ref, wg_ref, wu_ref, wd_ref, out_ref, h_ref, acc_ref):
    f = pl.program_id(1)

    @pl.when(f == 0)
    def _():
        h_ref[...] = _rms(x_ref[...], gain_ref[...]).astype(BF16)
        acc_ref[...] = jnp.zeros_like(acc_ref)

    h = h_ref[...]
    gate = jnp.dot(h, wg_ref[...], preferred_element_type=F32)
    up = jnp.dot(h, wu_ref[...], preferred_element_type=F32)
    hidden = (jax.nn.silu(gate) * up).astype(BF16)
    acc_ref[...] += jnp.dot(hidden, wd_ref[...], preferred_element_type=F32)

    @pl.when(f == pl.num_programs(1) - 1)
    def _():
        out_ref[...] = x_ref[...] + acc_ref[...]


def _ffn(x, gain, w_gate, w_up, w_down, tm=512, tf=1408):
    t, d = x.shape
    dff = w_gate.shape[1]
    return pl.pallas_call(
        _ffn_kernel,
        grid=(t // tm, dff // tf),
        in_specs=[pl.BlockSpec((tm, d), lambda i, f: (i, 0)),
                  _const_spec((1, d)),
                  pl.BlockSpec((d, tf), lambda i, f: (0, f)),
                  pl.BlockSpec((d, tf), lambda i, f: (0, f)),
                  pl.BlockSpec((tf, d), lambda i, f: (f, 0))],
        out_specs=pl.BlockSpec((tm, d), lambda i, f: (i, 0)),
        out_shape=jax.ShapeDtypeStruct((t, d), F32),
        scratch_shapes=[pltpu.VMEM((tm, d), BF16), pltpu.VMEM((tm, d), F32)],
        compiler_params=_cparams("arbitrary", "arbitrary"),
        name="ffn",
    )(x, gain.reshape(1, d), w_gate.astype(BF16), w_up.astype(BF16), w_down.astype(BF16))


def _router_kernel(x_ref, gain_ref, wr_ref, h_out, comb_out, *, n_experts):
    h = _rms(x_ref[...], gain_ref[...])
    h_out[...] = h.astype(BF16)
    logits = jnp.dot(h.astype(BF16), wr_ref[...], preferred_element_type=F32)
    lane = lax.broadcasted_iota(jnp.int32, logits.shape, 1)
    logits = jnp.where(lane < n_experts, logits, -jnp.inf)
    m1 = jnp.max(logits, axis=-1, keepdims=True)
    i1 = jnp.min(jnp.where(logits == m1, lane, LANES), axis=-1, keepdims=True)
    rest = jnp.where(lane == i1, -jnp.inf, logits)
    m2 = jnp.max(rest, axis=-1, keepdims=True)
    i2 = jnp.min(jnp.where(rest == m2, lane, LANES), axis=-1, keepdims=True)
    e2 = jnp.exp(m2 - m1)
    g1 = 1.0 / (1.0 + e2)
    g2 = e2 / (1.0 + e2)
    comb_out[...] = jnp.where(lane == i1, g1, 0.0) + jnp.where(lane == i2, g2, 0.0)


def _router(x, gain, w_router, tm=512):
    t, d = x.shape
    n_experts = w_router.shape[1]
    wr = jnp.zeros((d, LANES), F32).at[:, :n_experts].set(w_router).astype(BF16)
    tok = pl.BlockSpec((tm, d), lambda i: (i, 0))
    return pl.pallas_call(
        functools.partial(_router_kernel, n_experts=n_experts),
        grid=(t // tm,),
        in_specs=[tok, _const_spec((1, d)), _const_spec((d, LANES))],
        out_specs=[tok, pl.BlockSpec((tm, LANES), lambda i: (i, 0))],
        out_shape=[jax.ShapeDtypeStruct((t, d), BF16), jax.ShapeDtypeStruct((t, LANES), F32)],
        compiler_params=_cparams("arbitrary"),
        name="moe_router",
    )(x, gain.reshape(1, d), wr)


def _moe_kernel(x_ref, h_ref, comb_ref, wg_ref, wu_ref, wd_ref, out_ref, acc_ref):
    e, f = pl.program_id(1), pl.program_id(2)

    @pl.when((e == 0) & (f == 0))
    def _():
        acc_ref[...] = jnp.zeros_like(acc_ref)

    comb = comb_ref[...]
    lane = lax.broadcasted_iota(jnp.int32, comb.shape, 1)
    weight = jnp.sum(jnp.where(lane == e, comb, 0.0), axis=-1, keepdims=True)
    h = h_ref[...]
    gate = jnp.dot(h, wg_ref[...], preferred_element_type=F32)
    up = jnp.dot(h, wu_ref[...], preferred_element_type=F32)
    hidden = (jax.nn.silu(gate) * up * weight).astype(BF16)
    acc_ref[...] += jnp.dot(hidden, wd_ref[...], preferred_element_type=F32)

    @pl.when((e == pl.num_programs(1) - 1) & (f == pl.num_programs(2) - 1))
    def _():
        out_ref[...] = x_ref[...] + acc_ref[...]


def _moe(x, h, comb, w_gate, w_up, w_down, tm=512, tf=512):
    t, d = x.shape
    n_experts, _, dff = w_gate.shape
    tok = lambda w: pl.BlockSpec((tm, w), lambda i, e, f: (i, 0))
    return pl.pallas_call(
        _moe_kernel,
        grid=(t // tm, n_experts, dff // tf),
        in_specs=[tok(d), tok(d), tok(LANES),
                  pl.BlockSpec((None, d, tf), lambda i, e, f: (e, 0, f)),
                  pl.BlockSpec((None, d, tf), lambda i, e, f: (e, 0, f)),
                  pl.BlockSpec((None, tf, d), lambda i, e, f: (e, f, 0))],
        out_specs=tok(d),
        out_shape=jax.ShapeDtypeStruct((t, d), F32),
        scratch_shapes=[pltpu.VMEM((tm, d), F32)],
        compiler_params=_cparams("arbitrary", "arbitrary", "arbitrary"),
        name="moe_experts",
    )(x, h, comb, w_gate.astype(BF16), w_up.astype(BF16), w_down.astype(BF16))


def _attn_in_kernel(x_ref, gain_ref, w_ref, hg_ref, e_ref, et_ref, out_ref, h_ref, *, n_norm):
    j = pl.program_id(1)

    @pl.when(j == 0)
    def _():
        h_ref[...] = _rms(x_ref[...], gain_ref[...]).astype(BF16)

    y = jnp.dot(h_ref[...], w_ref[...], preferred_element_type=F32)

    @pl.when(j < n_norm)
    def _():
        ms = _head_sum(y * y, e_ref, et_ref) * (1.0 / HEAD)
        out_ref[...] = (y * lax.rsqrt(ms + RMS_EPS) * hg_ref[...]).astype(out_ref.dtype)

    @pl.when(j >= n_norm)
    def _():
        out_ref[...] = y.astype(out_ref.dtype)


def _attn_in(x, gain, w_in, q_gain, k_gain, tm=512):
    t, d = x.shape
    width = w_in.shape[1]
    ntiles = width // GROUP_WIDTH
    n_groups = q_gain.shape[0]
    head_gain = jnp.concatenate([jnp.tile(q_gain, (1, GROUP_HEADS)), jnp.tile(k_gain, (1, GROUP_HEADS)),
                                 jnp.ones((n_groups, GROUP_WIDTH), F32)], 0).reshape(ntiles, 1, GROUP_WIDTH)
    e, et = _head_indicator(GROUP_WIDTH)
    return pl.pallas_call(
        functools.partial(_attn_in_kernel, n_norm=2 * n_groups),
        grid=(t // tm, ntiles),
        in_specs=[pl.BlockSpec((tm, d), lambda i, j: (i, 0)),
                  _const_spec((1, d)),
                  pl.BlockSpec((d, GROUP_WIDTH), lambda i, j: (0, j)),
                  pl.BlockSpec((None, 1, GROUP_WIDTH), lambda i, j: (j, 0, 0)),
                  _const_spec(e.shape), _const_spec(et.shape)],
        out_specs=pl.BlockSpec((tm, GROUP_WIDTH), lambda i, j: (i, j)),
        out_shape=jax.ShapeDtypeStruct((t, width), BF16),
        scratch_shapes=[pltpu.VMEM((tm, d), BF16)],
        compiler_params=_cparams("arbitrary", "arbitrary"),
        name="attn_in",
    )(x, gain.reshape(1, d), w_in.astype(BF16), head_gain, e, et)


def _t5_bucket_np(rel):
    nb = REL_BUCKETS // 2
    max_exact = nb // 2
    n = np.abs(rel)
    large = max_exact + (np.log(np.maximum(n, 1).astype(np.float32) / max_exact)
                         / math.log(REL_MAX_DIST / max_exact) * (nb - max_exact)).astype(np.int32)
    large = np.minimum(large, nb - 1)
    return np.where(rel > 0, nb, 0) + np.where(n < max_exact, n, large)


def _band_bias(rel_bias, group, window, dil):
    side = window // (2 * dil)
    assert side == KV_HALO
    offs = dil * np.arange(-side, side + 1)
    buckets = _t5_bucket_np(offs)
    table = rel_bias.reshape(REL_BUCKETS, -1, GROUP_HEADS)[buckets, group].T
    qi = np.arange(Q_BLOCK)[:, None]
    kj = np.arange(Q_BLOCK + 2 * KV_HALO)[None, :]
    o = kj - qi
    inside = (o >= 0) & (o <= 2 * side)
    return jnp.where(jnp.asarray(inside)[None], table[:, np.clip(o, 0, 2 * side)], NEG_INF).astype(F32)


def _attn_kernel(q_ref, kp_ref, kc_ref, kn_ref, vp_ref, vc_ref, vn_ref, bias_ref, o_ref, lse_ref,
                 *, regions, dil):
    tp, sp, ss = regions
    row0 = pl.program_id(1) * Q_BLOCK
    seq = _seq_len_at(row0, tp // dil, sp // dil, ss // dil)
    seq_start = row0 - lax.rem(row0, seq)
    nk = Q_BLOCK + 2 * KV_HALO
    key_row = row0 - KV_HALO + lax.broadcasted_iota(jnp.int32, (1, nk), 1)
    valid = (key_row >= seq_start) & (key_row < seq_start + seq)
    lane = lax.broadcasted_iota(jnp.int32, (Q_BLOCK, LANES), 1)
    scale = HEAD ** -0.5

    q = q_ref[...]
    k = jnp.concatenate([kp_ref[...], kc_ref[...], kn_ref[...]], 0)
    v = jnp.concatenate([vp_ref[...], vc_ref[...], vn_ref[...]], 0)
    for pair in range(GROUP_HEADS // 2):
        ls = slice(pair * LANES, (pair + 1) * LANES)
        qp, kpair, vpair = q[:, ls], k[:, ls], v[:, ls]
        o_pair = jnp.zeros((Q_BLOCK, LANES), F32)
        lse_pair = jnp.zeros((Q_BLOCK, LANES), F32)
        for sub in range(2):
            mine = (lane < HEAD) if sub == 0 else (lane >= HEAD)
            qm = jnp.where(mine, qp, jnp.zeros_like(qp))
            s = lax.dot_general(qm, kpair, (_NT, ((), ())), preferred_element_type=F32)
            s = s * scale + bias_ref[2 * pair + sub]
            s = jnp.where(valid, s, NEG_INF)
            m = jnp.max(s, axis=-1, keepdims=True)
            p = jnp.exp(s - m)
            den = jnp.sum(p, axis=-1, keepdims=True)
            pn = (p / den).astype(v.dtype)
            o = jnp.dot(pn, vpair, preferred_element_type=F32)
            o_pair = jnp.where(mine, o, o_pair)
            lse_pair = jnp.where(mine, m + jnp.log(den), lse_pair)
        o_ref[:, ls] = o_pair.astype(o_ref.dtype)
        lse_ref[:, ls] = lse_pair


def _attn_group(qkv, rel_bias, group, regions):
    window, dil = ATTN_GROUPS[group]
    t, width = qkv.shape
    n_groups = len(ATTN_GROUPS)
    rows = t // dil
    qkv_d = qkv.reshape(rows, dil * width)
    tiles = width // GROUP_WIDTH
    bias = _band_bias(rel_bias, group, window, dil)
    halo_per_q = Q_BLOCK // KV_HALO
    last_halo = rows // KV_HALO - 1

    def cur(part):
        return pl.BlockSpec((Q_BLOCK, GROUP_WIDTH), lambda rho, m: (m, rho * tiles + part * n_groups + group))

    def prev(part):
        return pl.BlockSpec((KV_HALO, GROUP_WIDTH),
                            lambda rho, m: (jnp.maximum(m * halo_per_q - 1, 0), rho * tiles + part * n_groups + group))

    def nxt(part):
        return pl.BlockSpec((KV_HALO, GROUP_WIDTH),
                            lambda rho, m: (jnp.minimum((m + 1) * halo_per_q, last_halo),
                                            rho * tiles + part * n_groups + group))

    out_spec = pl.BlockSpec((Q_BLOCK, GROUP_WIDTH), lambda rho, m: (m, rho))
    o, lse = pl.pallas_call(
        functools.partial(_attn_kernel, regions=regions, dil=dil),
        grid=(dil, rows // Q_BLOCK),
        in_specs=[cur(0), prev(1), cur(1), nxt(1), prev(2), cur(2), nxt(2), _const_spec(bias.shape)],
        out_specs=[out_spec, out_spec],
        out_shape=[jax.ShapeDtypeStruct((rows, dil * GROUP_WIDTH), BF16),
                   jax.ShapeDtypeStruct((rows, dil * GROUP_WIDTH), F32)],
        compiler_params=_cparams("arbitrary", "arbitrary"),
        name=f"attn_g{group}",
    )(qkv_d, qkv_d, qkv_d, qkv_d, qkv_d, qkv_d, qkv_d, bias)
    return o.reshape(t, GROUP_WIDTH), lse.reshape(t, GROUP_WIDTH)


def _attn_out_kernel(o0, o1, o2, l0, l1, l2, x_ref, wo_ref, out_ref):
    lses = [l0[...], l1[...], l2[...]]
    mx = jnp.maximum(jnp.maximum(lses[0], lses[1]), lses[2])
    ws = [jnp.exp(l - mx) for l in lses]
    tot = ws[0] + ws[1] + ws[2]
    acc = x_ref[...]
    for g, o_ref in enumerate((o0, o1, o2)):
        scaled = (o_ref[...].astype(F32) * (ws[g] / tot)).astype(BF16)
        acc = acc + jnp.dot(scaled, wo_ref[g], preferred_element_type=F32)
    out_ref[...] = acc


def _attn_out(os_, lses, x, w_o, tm=512):
    t, d = x.shape
    n_groups = len(os_)
    grp = pl.BlockSpec((tm, GROUP_WIDTH), lambda i: (i, 0))
    tok = pl.BlockSpec((tm, d), lambda i: (i, 0))
    wo = w_o.reshape(n_groups, GROUP_WIDTH, d).astype(BF16)
    return pl.pallas_call(
        _attn_out_kernel,
        grid=(t // tm,),
        in_specs=[grp] * (2 * n_groups) + [tok, _const_spec(wo.shape)],
        out_specs=tok,
        out_shape=jax.ShapeDtypeStruct((t, d), F32),
        compiler_params=_cparams("arbitrary"),
        name="attn_out",
    )(*os_, *lses, x, wo)


def kernel(x_prompt, x_sample, norm_mix, norm_ffn, rwkv_mu, rwkv_w_rkv, rwkv_w0, rwkv_w1, rwkv_w2, rwkv_a0, rwkv_a1, rwkv_a2, rwkv_g1, rwkv_g2, rwkv_k_k, rwkv_k_a, rwkv_r_k, rwkv_ln_g, rwkv_ln_b, rwkv_w_o, attn_w_in, attn_q_gain, attn_k_gain, attn_w_o, rel_bias, ffn_w_gate, ffn_w_up, ffn_w_down, moe_router, moe_w_gate, moe_w_up, moe_w_down):
    d = x_prompt.shape[-1]
    tp = x_prompt.shape[0] * x_prompt.shape[1]
    regions = (tp, x_prompt.shape[1], x_sample.shape[1])
    x = jnp.concatenate([x_prompt.reshape(-1, d), x_sample.reshape(-1, d)], 0)
    depth = norm_mix.shape[0]
    for i in range(depth):
        j = i // 2
        if i % 2 == 0:
            r, v, kk, g, bonus, logw, kdir, bdir = _rwkv_pre(
                x, regions, norm_mix[i], rwkv_mu[j], rwkv_w_rkv[j], rwkv_w0[j], rwkv_w1[j], rwkv_w2[j],
                rwkv_a0[j], rwkv_a1[j], rwkv_a2[j], rwkv_g1[j], rwkv_g2[j], rwkv_k_k[j], rwkv_k_a[j], rwkv_r_k[j])
            yf = _wkv(r, v, kk, logw, kdir, bdir, regions, reverse=False)
            yb = _wkv(r, v, kk, logw, kdir, bdir, regions, reverse=True)
            x = _rwkv_post(yf, yb, bonus, g, x, rwkv_ln_g[j], rwkv_ln_b[j], rwkv_w_o[j])
            x = _ffn(x, norm_ffn[i], ffn_w_gate[j], ffn_w_up[j], ffn_w_down[j])
        else:
            qkv = _attn_in(x, norm_mix[i], attn_w_in[j], attn_q_gain[j], attn_k_gain[j])
            outs = [_attn_group(qkv, rel_bias, g, regions) for g in range(len(ATTN_GROUPS))]
            x = _attn_out([o for o, _ in outs], [l for _, l in outs], x, attn_w_o[j])
            h, comb = _router(x, norm_ffn[i], moe_router[j])
            x = _moe(x, h, comb, moe_w_gate[j], moe_w_up[j], moe_w_down[j])
    y_prompt = x[:tp].reshape(x_prompt.shape)
    y_sample = x[tp:].reshape(x_sample.shape)
    return (y_prompt, y_sample)
```

```python
import functools
import math

import numpy as np
import jax
import jax.numpy as jnp
from jax import lax
from jax.experimental import pallas as pl
from jax.experimental.pallas import tpu as pltpu

F32 = jnp.float32
BF16 = jnp.bfloat16

LANES = 128
VMEM_LIMIT_BYTES = 56 * 1024 * 1024

HEAD = 64
RMS_EPS = 1e-6
GN_EPS = 64e-5
ATTN_GROUPS = ((128, 1), (512, 4), (2048, 16))
GROUP_HEADS = 6
GROUP_WIDTH = GROUP_HEADS * HEAD
REL_BUCKETS = 32
REL_MAX_DIST = 1024
Q_BLOCK = 128
KV_HALO = 64
SCAN_CHUNK = 64
NEG_INF = -1e30


def _cparams(*sem):
    return pltpu.CompilerParams(dimension_semantics=sem, vmem_limit_bytes=VMEM_LIMIT_BYTES)


def _const_spec(shape):
    nd = len(shape)
    return pl.BlockSpec(shape, lambda *_: (0,) * nd, pipeline_mode=pl.Buffered(1))


def _split_dot(x, w):
    hi = x.astype(BF16)
    lo = (x - hi.astype(F32)).astype(BF16)
    return (jnp.dot(hi, w, preferred_element_type=F32)
            + jnp.dot(lo, w, preferred_element_type=F32))


def _head_sum(x, e_ref, et_ref):
    return _split_dot(_split_dot(x, e_ref[...]), et_ref[...])


def _rms(x, gain):
    return x * lax.rsqrt(jnp.mean(x * x, axis=-1, keepdims=True) + RMS_EPS) * gain


def _seq_len_at(row0, tp, sp, ss):
    return jnp.where(row0 < tp, sp, ss)


def _head_indicator(width):
    e = np.zeros((width, LANES), np.float32)
    e[np.arange(width), np.arange(width) // HEAD] = 1.0
    return jnp.asarray(e, BF16), jnp.asarray(e.T, BF16)


def _rwkv_pre_kernel(x_ref, xp_ref, xn_ref, gain_ref, mu_ref, wrkv_ref, wl1_ref, w2_ref, a2_ref,
                     g2_ref, w0_ref, a0_ref, kk_gain_ref, ka_ref, rk_ref, e_ref, et_ref,
                     r_out, v_out, kk_out, g_out, bonus_out, logw_out, kdir_out, bdir_out,
                     *, tm, regions):
    tp, sp, ss = regions
    d = x_ref.shape[-1]
    row0 = pl.program_id(0) * tm
    seq = _seq_len_at(row0, tp, sp, ss)
    at_start = lax.rem(row0, seq) == 0
    at_end = lax.rem(row0 + tm, seq) == 0

    gain = gain_ref[...]
    h = _rms(x_ref[...], gain)
    hp = _rms(xp_ref[...], gain)[7:8, :] * jnp.where(at_start, 0.0, 1.0)
    hn = _rms(xn_ref[...], gain)[0:1, :] * jnp.where(at_end, 0.0, 1.0)
    rows = lax.broadcasted_iota(jnp.int32, (tm, d), 0)
    h_prev = jnp.where(rows == 0, hp, pltpu.roll(h, 1, 0))
    h_next = jnp.where(rows == tm - 1, hn, pltpu.roll(h, tm - 1, 0))
    xx = 0.5 * (h_prev + h_next) - h

    def mixed(n):
        return (h + xx * mu_ref[n:n + 1, :]).astype(BF16)

    r = jnp.dot(mixed(0), wrkv_ref[0], preferred_element_type=F32)
    k = jnp.dot(mixed(1), wrkv_ref[1], preferred_element_type=F32)
    v = jnp.dot(mixed(2), wrkv_ref[2], preferred_element_type=F32)

    t_w = jnp.tanh(jnp.dot(mixed(3), wl1_ref[0], preferred_element_type=F32))
    lw = w0_ref[...] + jnp.dot(t_w.astype(BF16), w2_ref[...], preferred_element_type=F32)
    logw = -math.exp(-0.5) * jax.nn.sigmoid(lw)
    t_a = jnp.dot(mixed(4), wl1_ref[1], preferred_element_type=F32)
    a = jax.nn.sigmoid(a0_ref[...] + jnp.dot(t_a.astype(BF16), a2_ref[...], preferred_element_type=F32))
    t_g = jax.nn.sigmoid(jnp.dot(mixed(5), wl1_ref[2], preferred_element_type=F32))
    g = jnp.dot(t_g.astype(BF16), g2_ref[...], preferred_element_type=F32)

    kk = k * kk_gain_ref[...]
    kk = kk * lax.rsqrt(jnp.maximum(_head_sum(kk * kk, e_ref, et_ref), 1e-12))
    bonus = _head_sum(r * k * rk_ref[...], e_ref, et_ref) * v

    r_out[...] = r
    v_out[...] = v
    kk_out[...] = kk
    g_out[...] = g
    bonus_out[...] = bonus
    ka = ka_ref[...]
    for z in range(2):
        a_z = a[:, z * d:(z + 1) * d]
        logw_out[z] = logw[:, z * d:(z + 1) * d]
        kdir_out[z] = k * (1.0 + (a_z - 1.0) * ka)
        bdir_out[z] = kk * a_z


def _block_diag2(m):
    z = jnp.zeros_like(m[0])
    return jnp.concatenate([jnp.concatenate([m[0], z], 1), jnp.concatenate([z, m[1]], 1)], 0)


def _rwkv_pre(x, regions, gain, mu, w_rkv, w0, w1, w2, a0, a1, a2, g1, g2, k_k, k_a, r_k, tm=256):
    t, d = x.shape
    e, et = _head_indicator(d)
    wl1 = jnp.stack([jnp.concatenate([w1[0], w1[1]], 1), jnp.concatenate([a1[0], a1[1]], 1), g1]).astype(BF16)
    w2c = _block_diag2(w2).astype(BF16)
    a2c = _block_diag2(a2).astype(BF16)
    row = lambda p: p.reshape(1, -1).astype(F32)
    tok = pl.BlockSpec((tm, d), lambda i: (i, 0))
    halo_blocks = t // 8
    prev = pl.BlockSpec((8, d), lambda i: (jnp.maximum(i * (tm // 8) - 1, 0), 0))
    nxt = pl.BlockSpec((8, d), lambda i: (jnp.minimum((i + 1) * (tm // 8), halo_blocks - 1), 0))
    dir_tok = pl.BlockSpec((2, tm, d), lambda i: (0, i, 0))
    consts = [row(gain), mu.astype(F32), w_rkv.astype(BF16), wl1, w2c, a2c, g2.astype(BF16),
              row(w0), row(a0), row(k_k), row(k_a), row(r_k), e, et]
    tok_shape = jax.ShapeDtypeStruct((t, d), F32)
    dir_shape = jax.ShapeDtypeStruct((2, t, d), F32)
    return pl.pallas_call(
        functools.partial(_rwkv_pre_kernel, tm=tm, regions=regions),
        grid=(t // tm,),
        in_specs=[tok, prev, nxt] + [_const_spec(c.shape) for c in consts],
        out_specs=[tok] * 5 + [dir_tok] * 3,
        out_shape=[tok_shape] * 5 + [dir_shape] * 3,
        compiler_params=_cparams("arbitrary"),
        name="rwkv_pre",
    )(x, x, x, *consts)


_NN = ((1,), (0,))
_NT = ((1,), (1,))
_TN = ((0,), (0,))


def _mm(a, b, dims=_NN):
    return lax.dot_general(a.astype(BF16), b.astype(BF16), (dims, ((), ())), preferred_element_type=F32)


def _cumsum_mm(tri, x):
    x0 = x.astype(BF16)
    r1 = x - x0.astype(F32)
    x1 = r1.astype(BF16)
    x2 = (r1 - x1.astype(F32)).astype(BF16)
    return (jnp.dot(tri, x0, preferred_element_type=F32) + jnp.dot(tri, x1, preferred_element_type=F32)
            + jnp.dot(tri, x2, preferred_element_type=F32))


def _wkv_kernel(r_ref, v_ref, kk_ref, logw_ref, kdir_ref, bdir_ref, y_ref, state_ref,
                *, rb, regions, reverse):
    tp, sp, ss = regions
    c = SCAN_CHUNK
    npairs = r_ref.shape[-1] // LANES
    nsteps = pl.num_programs(1)
    step = pl.program_id(1)
    blk = (nsteps - 1 - step) if reverse else step
    row0 = blk * rb
    seq = _seq_len_at(row0, tp, sp, ss)
    first = (lax.rem(row0 + rb, seq) == 0) if reverse else (lax.rem(row0, seq) == 0)

    @pl.when(first)
    def _():
        state_ref[...] = jnp.zeros_like(state_ref)

    ti = lax.broadcasted_iota(jnp.int32, (c, c), 0)
    tj = lax.broadcasted_iota(jnp.int32, (c, c), 1)
    incl = (tj >= ti) if reverse else (tj <= ti)
    incl_b = incl.astype(BF16)
    ti2 = lax.broadcasted_iota(jnp.int32, (c, 2 * c), 0)
    tj2 = lax.broadcasted_iota(jnp.int32, (c, 2 * c), 1) & (c - 1)
    ti4 = lax.broadcasted_iota(jnp.int32, (c, 4 * c), 0)
    tj4 = lax.broadcasted_iota(jnp.int32, (c, 4 * c), 1) & (c - 1)
    incl4 = (tj4 >= ti4) if reverse else (tj4 <= ti4)
    strict2 = (tj2 > ti2) if reverse else (tj2 < ti2)
    lane = lax.broadcasted_iota(jnp.int32, (c, LANES), 1)
    head0 = lane < HEAD
    rr = lax.broadcasted_iota(jnp.int32, (LANES, LANES), 0)
    cc = lax.broadcasted_iota(jnp.int32, (LANES, LANES), 1)
    same_head = (rr < HEAD) == (cc < HEAD)
    eye = (rr == cc).astype(F32)

    def stack_heads(m):
        return jnp.concatenate([jnp.where(head0, m, 0.0), jnp.where(head0, 0.0, m)], 0)

    order = list(range(rb // c - 1, -1, -1) if reverse else range(rb // c))
    units = [(pl.ds(ci * c, c), pl.ds(p * LANES, LANES)) for ci in order for p in range(npairs)]
    each = lambda fn, *lists: [fn(*xs) for xs in zip(*lists)]

    lw = [logw_ref[u] for u in units]
    cum = each(lambda x: _cumsum_mm(incl_b, x), lw)
    total = each(lambda x: jnp.sum(x, axis=0, keepdims=True), lw)
    lhs = [jnp.concatenate([r_ref[u] * jnp.exp(cm), kk_ref[u] * jnp.exp(cm - x)], 0).astype(BF16)
           for u, cm, x in zip(units, cum, lw)]
    rhs = []
    kb_tail = []
    for u, cm, tt in zip(units, cum, total):
        kd, bd = kdir_ref[u], bdir_ref[u]
        inv = jnp.exp(-cm)
        tail = jnp.exp(tt - cm)
        rhs.append(jnp.concatenate([stack_heads(kd * inv), stack_heads(bd * inv)], 0).astype(BF16))
        kb_tail.append(jnp.concatenate([kd * tail, -(bd * tail)], 0).astype(BF16))
    q = each(lambda a, b: _mm(a, b, _NT), lhs, rhs)
    a_out = [jnp.where(incl4, x[0:c], 0.0).astype(BF16) for x in q]
    a_kk = [jnp.where(strict2, x[c:2 * c, 0:2 * c], 0.0).astype(BF16) for x in q]
    pw = [-stack_heads(jnp.where(strict2, x[c:2 * c, 2 * c:4 * c], 0.0)) for x in q]
    inv_l = [eye + x for x in pw]
    for _ in range(int(math.log2(c)) - 1):
        pw = each(lambda x: _mm(x, x), pw)
        inv_l = each(lambda t, x: t + _mm(t, x), inv_l, pw)
    inv_l = [x.astype(BF16) for x in inv_l]
    v_all = [v_ref[u] for u in units]
    v_bd = [stack_heads(x).astype(BF16) for x in v_all]
    akv = each(_mm, a_kk, v_bd)
    decay = [jnp.exp(x) for x in total]

    states = [state_ref[p] for p in range(npairs)]
    for k in range(len(order)):
        idx = list(range(k * npairs, (k + 1) * npairs))
        sk = [_mm(lhs[i], states[p], _NT) for p, i in enumerate(idx)]
        u_bd = [_mm(inv_l[i], stack_heads(s[c:2 * c] + akv[i])) for s, i in zip(sk, idx)]
        y = [s[0:c] + _mm(a_out[i], jnp.concatenate([v_bd[i], (-ub).astype(BF16)], 0))
             for s, ub, i in zip(sk, u_bd, idx)]
        for yy, i in zip(y, idx):
            y_ref[units[i]] = yy
        upd = [_mm(jnp.concatenate([v_all[i], ub[0:c] + ub[c:2 * c]], 0), kb_tail[i], _TN)
               for ub, i in zip(u_bd, idx)]
        states = [states[p] * decay[i] + jnp.where(same_head, upd[p], 0.0) for p, i in enumerate(idx)]
    for p in range(npairs):
        state_ref[p] = states[p]


def _wkv(r, v, kk, logw, kdir, bdir, regions, reverse, rb=256, width=512):
    t, d = r.shape
    z = 1 if reverse else 0
    nsteps = t // rb

    def row_block(s):
        return (nsteps - 1 - s) if reverse else s

    tok = pl.BlockSpec((rb, width), lambda j, s: (row_block(s), j))
    dir_tok = pl.BlockSpec((None, rb, width), lambda j, s: (z, row_block(s), j))
    return pl.pallas_call(
        functools.partial(_wkv_kernel, rb=rb, regions=regions, reverse=reverse),
        grid=(d // width, nsteps),
        in_specs=[tok, tok, tok, dir_tok, dir_tok, dir_tok],
        out_specs=tok,
        out_shape=jax.ShapeDtypeStruct((t, d), F32),
        scratch_shapes=[pltpu.VMEM((width // LANES, LANES, LANES), F32)],
        compiler_params=_cparams("arbitrary", "arbitrary"),
        name="wkv_bwd" if reverse else "wkv_fwd",
    )(r, v, kk, logw, kdir, bdir)


def _rwkv_post_kernel(yf_ref, yb_ref, bonus_ref, g_ref, x_ref, lng_ref, lnb_ref, wo_ref, e_ref, et_ref, out_ref):
    y = yf_ref[...] + yb_ref[...]
    mean = _head_sum(y, e_ref, et_ref) * (1.0 / HEAD)
    dlt = y - mean
    var = _head_sum(dlt * dlt, e_ref, et_ref) * (1.0 / HEAD)
    yn = dlt * lax.rsqrt(var + GN_EPS) * lng_ref[...] + lnb_ref[...]
    mixed = ((yn + bonus_ref[...]) * g_ref[...]).astype(BF16)
    out_ref[...] = x_ref[...] + jnp.dot(mixed, wo_ref[...], preferred_element_type=F32)


def _rwkv_post(yf, yb, bonus, g, x, ln_g, ln_b, w_o, tm=256):
    t, d = x.shape
    e, et = _head_indicator(d)
    tok = pl.BlockSpec((tm, d), lambda i: (i, 0))
    consts = [ln_g.reshape(1, d), ln_b.reshape(1, d), w_o.astype(BF16), e, et]
    return pl.pallas_call(
        _rwkv_post_kernel,
        grid=(t // tm,),
        in_specs=[tok] * 5 + [_const_spec(c.shape) for c in consts],
        out_specs=tok,
        out_shape=jax.ShapeDtypeStruct((t, d), F32),
        compiler_params=_cparams("arbitrary"),
        name="rwkv_post",
    )(yf, yb, bonus, g, x, *consts)


def _ffn_kernel(x_ref, gain_ref, wg_ref, wu_ref, wd_ref, out_ref, h_ref, acc_ref):
    f = pl.program_id(1)

    @pl.when(f == 0)
    def _():
        h_ref[...] = _rms(x_ref[...], gain_ref[...]).astype(BF16)
        acc_ref[...] = jnp.zeros_like(acc_ref)

    h = h_ref[...]
    gate = jnp.dot(h, wg_ref[...], preferred_element_type=F32)
    up = jnp.dot(h, wu_ref[...], preferred_element_type=F32)
    hidden = (jax.nn.silu(gate) * up).astype(BF16)
    acc_ref[...] += jnp.dot(hidden, wd_ref[...], preferred_element_type=F32)

    @pl.when(f == pl.num_programs(1) - 1)
    def _():
        out_ref[...] = x_ref[...] + acc_ref[...]


def _ffn(x, gain, w_gate, w_up, w_down, tm=512, tf=1408):
    t, d = x.shape
    dff = w_gate.shape[1]
    return pl.pallas_call(
        _ffn_kernel,
        grid=(t // tm, dff // tf),
        in_specs=[pl.BlockSpec((tm, d), lambda i, f: (i, 0)),
                  _const_spec((1, d)),
                  pl.BlockSpec((d, tf), lambda i, f: (0, f)),
                  pl.BlockSpec((d, tf), lambda i, f: (0, f)),
                  pl.BlockSpec((tf, d), lambda i, f: (f, 0))],
        out_specs=pl.BlockSpec((tm, d), lambda i, f: (i, 0)),
        out_shape=jax.ShapeDtypeStruct((t, d), F32),
        scratch_shapes=[pltpu.VMEM((tm, d), BF16), pltpu.VMEM((tm, d), F32)],
        compiler_params=_cparams("arbitrary", "arbitrary"),
        name="ffn",
    )(x, gain.reshape(1, d), w_gate.astype(BF16), w_up.astype(BF16), w_down.astype(BF16))


def _router_kernel(x_ref, gain_ref, wr_ref, tri_ref, h_out, comb_out, rank_out, count_out, carry_ref,
                   *, n_experts):
    @pl.when(pl.program_id(0) == 0)
    def _():
        carry_ref[...] = jnp.zeros_like(carry_ref)

    h = _rms(x_ref[...], gain_ref[...])
    h_out[...] = h
    logits = jnp.dot(h.astype(BF16), wr_ref[...], preferred_element_type=F32)
    lane = lax.broadcasted_iota(jnp.int32, logits.shape, 1)
    logits = jnp.where(lane < n_experts, logits, -jnp.inf)
    m1 = jnp.max(logits, axis=-1, keepdims=True)
    i1 = jnp.min(jnp.where(logits == m1, lane, LANES), axis=-1, keepdims=True)
    rest = jnp.where(lane == i1, -jnp.inf, logits)
    m2 = jnp.max(rest, axis=-1, keepdims=True)
    i2 = jnp.min(jnp.where(rest == m2, lane, LANES), axis=-1, keepdims=True)
    e2 = jnp.exp(m2 - m1)
    g1 = 1.0 / (1.0 + e2)
    g2 = e2 / (1.0 + e2)
    comb_out[...] = jnp.where(lane == i1, g1, 0.0) + jnp.where(lane == i2, g2, 0.0)
    sel = jnp.where((lane == i1) | (lane == i2), 1.0, 0.0)
    earlier = carry_ref[0:1, :] + jnp.dot(tri_ref[...], sel.astype(BF16), preferred_element_type=F32)
    rank_out[...] = sel * (earlier + 1.0)
    carry_ref[...] = carry_ref[...] + jnp.sum(sel, axis=0, keepdims=True)
    count_out[...] = carry_ref[...]


def _router(x, gain, w_router, tm=512):
    t, d = x.shape
    n_experts = w_router.shape[1]
    wr = jnp.zeros((d, LANES), F32).at[:, :n_experts].set(w_router).astype(BF16)
    tri = jnp.asarray(np.tril(np.ones((tm, tm), np.float32), -1), BF16)
    tok = pl.BlockSpec((tm, d), lambda i: (i, 0))
    lanes = pl.BlockSpec((tm, LANES), lambda i: (i, 0))
    return pl.pallas_call(
        functools.partial(_router_kernel, n_experts=n_experts),
        grid=(t // tm,),
        in_specs=[tok, _const_spec((1, d)), _const_spec((d, LANES)), _const_spec((tm, tm))],
        out_specs=[tok, lanes, lanes, pl.BlockSpec((8, LANES), lambda i: (0, 0))],
        out_shape=[jax.ShapeDtypeStruct((t, d), F32), jax.ShapeDtypeStruct((t, LANES), F32),
                   jax.ShapeDtypeStruct((t, LANES), F32), jax.ShapeDtypeStruct((8, LANES), F32)],
        scratch_shapes=[pltpu.VMEM((8, LANES), F32)],
        compiler_params=_cparams("arbitrary"),
        name="moe_router",
    )(x, gain.reshape(1, d), wr, tri)


def _route_plan(comb, rank, counts, n_experts, tile):
    t = comb.shape[0]
    n_tiles = (2 * t) // tile + n_experts
    cnt = counts[0, :n_experts].astype(jnp.int32)
    padded = ((cnt + tile - 1) // tile) * tile
    ends = jnp.cumsum(padded)
    starts = jnp.zeros((LANES,), jnp.int32).at[:n_experts].set(ends - padded)
    slot = jnp.where(rank > 0, starts[None, :] + rank.astype(jnp.int32) - 1, -1)
    slot_hi = jnp.max(slot, axis=1)
    slot_lo = jnp.min(jnp.where(slot >= 0, slot, jnp.iinfo(jnp.int32).max), axis=1)
    gate_lo = jnp.sum(jnp.where(slot == slot_lo[:, None], comb, 0.0), axis=1)
    gate_hi = jnp.sum(jnp.where(slot == slot_hi[:, None], comb, 0.0), axis=1)
    slots = jnp.stack([slot_lo, slot_hi], 1)
    gates = jnp.stack([gate_lo, gate_hi], 1)
    tile_expert = jnp.sum(jnp.arange(n_tiles, dtype=jnp.int32)[:, None] * tile >= ends[None, :], axis=1)
    return slots, gates, jnp.minimum(tile_expert, n_experts - 1).astype(jnp.int32), n_tiles


def _row_copy(src, src_row, dst, dst_row, sem):
    return pltpu.make_async_copy(src.at[pl.ds(src_row, 1)], dst.at[pl.ds(dst_row, 1)], sem)


def _dispatch_kernel(slots_ref, h_hbm, init_hbm, xs_hbm, sem, *, tm):
    del init_hbm
    row0 = pl.program_id(0) * tm

    def issue(i, carry):
        for k in range(2):
            _row_copy(h_hbm, row0 + i, xs_hbm, slots_ref[0, 2 * i + k], sem).start()
        return carry

    lax.fori_loop(0, tm, issue, 0)
    pltpu.make_async_copy(h_hbm.at[pl.ds(0, 2 * tm)], xs_hbm.at[pl.ds(0, 2 * tm)], sem).wait()


def _dispatch(h, slots, n_rows, tm=512):
    t, d = h.shape
    return pl.pallas_call(
        functools.partial(_dispatch_kernel, tm=tm),
        grid=(t // tm,),
        in_specs=[pl.BlockSpec((None, 1, 2 * tm), lambda i: (i, 0, 0), memory_space=pltpu.SMEM),
                  pl.BlockSpec(memory_space=pl.ANY), pl.BlockSpec(memory_space=pl.ANY)],
        out_specs=pl.BlockSpec(memory_space=pl.ANY),
        out_shape=jax.ShapeDtypeStruct((n_rows, d), h.dtype),
        scratch_shapes=[pltpu.SemaphoreType.DMA(())],
        input_output_aliases={2: 0},
        compiler_params=pltpu.CompilerParams(dimension_semantics=("arbitrary",), has_side_effects=True),
        name="moe_dispatch",
    )(slots.reshape(t // tm, 1, 2 * tm), h, jnp.zeros((n_rows, d), h.dtype))


def _experts_kernel(te_ref, xs_ref, wg_ref, wu_ref, wd_ref, ys_ref, xb_ref, acc_ref):
    del te_ref
    f = pl.program_id(1)

    @pl.when(f == 0)
    def _():
        xb_ref[...] = xs_ref[...].astype(BF16)
        acc_ref[...] = jnp.zeros_like(acc_ref)

    xb = xb_ref[...]
    gate = jnp.dot(xb, wg_ref[...], preferred_element_type=F32)
    up = jnp.dot(xb, wu_ref[...], preferred_element_type=F32)
    hidden = (jax.nn.silu(gate) * up).astype(BF16)
    acc_ref[...] += jnp.dot(hidden, wd_ref[...], preferred_element_type=F32)

    @pl.when(f == pl.num_programs(1) - 1)
    def _():
        ys_ref[...] = acc_ref[...]


def _experts(xs, tile_expert, w_gate, w_up, w_down, tile, tf=512):
    n_rows, d = xs.shape
    dff = w_gate.shape[2]
    grid_spec = pltpu.PrefetchScalarGridSpec(
        num_scalar_prefetch=1,
        grid=(n_rows // tile, dff // tf),
        in_specs=[pl.BlockSpec((tile, d), lambda i, f, te: (i, 0)),
                  pl.BlockSpec((None, d, tf), lambda i, f, te: (te[i], 0, f)),
                  pl.BlockSpec((None, d, tf), lambda i, f, te: (te[i], 0, f)),
                  pl.BlockSpec((None, tf, d), lambda i, f, te: (te[i], f, 0))],
        out_specs=pl.BlockSpec((tile, d), lambda i, f, te: (i, 0)),
        scratch_shapes=[pltpu.VMEM((tile, d), BF16), pltpu.VMEM((tile, d), F32)])
    return pl.pallas_call(
        _experts_kernel,
        grid_spec=grid_spec,
        out_shape=jax.ShapeDtypeStruct((n_rows, d), F32),
        compiler_params=_cparams("arbitrary", "arbitrary"),
        name="moe_experts",
    )(tile_expert, xs, w_gate.astype(BF16), w_up.astype(BF16), w_down.astype(BF16))


def _combine_kernel(slots_ref, x_ref, gates_ref, ys_hbm, out_ref, buf_ref, sem, *, tm):
    def issue(i, carry):
        for k in range(2):
            _row_copy(ys_hbm, slots_ref[0, 2 * i + k], buf_ref.at[k], i, sem).start()
        return carry

    lax.fori_loop(0, tm, issue, 0)
    for k in range(2):
        pltpu.make_async_copy(ys_hbm.at[pl.ds(0, tm)], buf_ref.at[k], sem).wait()
    gates = gates_ref[...]
    out_ref[...] = x_ref[...] + gates[:, 0:1] * buf_ref[0] + gates[:, 1:2] * buf_ref[1]


def _combine(x, ys, slots, gates, tm=256):
    t, d = x.shape
    tok = pl.BlockSpec((tm, d), lambda i: (i, 0))
    return pl.pallas_call(
        functools.partial(_combine_kernel, tm=tm),
        grid=(t // tm,),
        in_specs=[pl.BlockSpec((None, 1, 2 * tm), lambda i: (i, 0, 0), memory_space=pltpu.SMEM),
                  tok, pl.BlockSpec((tm, 2), lambda i: (i, 0)), pl.BlockSpec(memory_space=pl.ANY)],
        out_specs=tok,
        out_shape=jax.ShapeDtypeStruct((t, d), F32),
        scratch_shapes=[pltpu.VMEM((2, tm, d), F32), pltpu.SemaphoreType.DMA(())],
        compiler_params=_cparams("arbitrary"),
        name="moe_combine",
    )(slots.reshape(t // tm, 1, 2 * tm), x, gates, ys)


def _moe(x, gain, w_router, w_gate, w_up, w_down, tile=512):
    n_experts = w_gate.shape[0]
    h, comb, rank, counts = _router(x, gain, w_router)
    slots, gates, tile_expert, n_tiles = _route_plan(comb, rank, counts, n_experts, tile)
    xs = _dispatch(h, slots, n_tiles * tile)
    ys = _experts(xs, tile_expert, w_gate, w_up, w_down, tile)
    return _combine(x, ys, slots, gates)


def _attn_in_kernel(x_ref, gain_ref, w_ref, hg_ref, e_ref, et_ref, out_ref, h_ref, *, n_norm):
    j = pl.program_id(1)

    @pl.when(j == 0)
    def _():
        h_ref[...] = _rms(x_ref[...], gain_ref[...]).astype(BF16)

    y = jnp.dot(h_ref[...], w_ref[...], preferred_element_type=F32)

    @pl.when(j < n_norm)
    def _():
        ms = _head_sum(y * y, e_ref, et_ref) * (1.0 / HEAD)
        out_ref[...] = (y * lax.rsqrt(ms + RMS_EPS) * hg_ref[...]).astype(out_ref.dtype)

    @pl.when(j >= n_norm)
    def _():
        out_ref[...] = y.astype(out_ref.dtype)


def _attn_in(x, gain, w_in, q_gain, k_gain, tm=512):
    t, d = x.shape
    width = w_in.shape[1]
    ntiles = width // GROUP_WIDTH
    n_groups = q_gain.shape[0]
    head_gain = jnp.concatenate([jnp.tile(q_gain, (1, GROUP_HEADS)), jnp.tile(k_gain, (1, GROUP_HEADS)),
                                 jnp.ones((n_groups, GROUP_WIDTH), F32)], 0).reshape(ntiles, 1, GROUP_WIDTH)
    e, et = _head_indicator(GROUP_WIDTH)
    return pl.pallas_call(
        functools.partial(_attn_in_kernel, n_norm=2 * n_groups),
        grid=(t // tm, ntiles),
        in_specs=[pl.BlockSpec((tm, d), lambda i, j: (i, 0)),
                  _const_spec((1, d)),
                  pl.BlockSpec((d, GROUP_WIDTH), lambda i, j: (0, j)),
                  pl.BlockSpec((None, 1, GROUP_WIDTH), lambda i, j: (j, 0, 0)),
                  _const_spec(e.shape), _const_spec(et.shape)],
        out_specs=pl.BlockSpec((tm, GROUP_WIDTH), lambda i, j: (i, j)),
        out_shape=jax.ShapeDtypeStruct((t, width), BF16),
        scratch_shapes=[pltpu.VMEM((tm, d), BF16)],
        compiler_params=_cparams("arbitrary", "arbitrary"),
        name="attn_in",
    )(x, gain.reshape(1, d), w_in.astype(BF16), head_gain, e, et)


def _t5_bucket_np(rel):
    nb = REL_BUCKETS // 2
    max_exact = nb // 2
    n = np.abs(rel)
    large = max_exact + (np.log(np.maximum(n, 1).astype(np.float32) / max_exact)
                         / math.log(REL_MAX_DIST / max_exact) * (nb - max_exact)).astype(np.int32)
    large = np.minimum(large, nb - 1)
    return np.where(rel > 0, nb, 0) + np.where(n < max_exact, n, large)


def _band_bias(rel_bias, group, window, dil):
    side = window // (2 * dil)
    assert side == KV_HALO
    offs = dil * np.arange(-side, side + 1)
    buckets = _t5_bucket_np(offs)
    table = rel_bias.reshape(REL_BUCKETS, -1, GROUP_HEADS)[buckets, group].T
    qi = np.arange(Q_BLOCK)[:, None]
    kj = np.arange(Q_BLOCK + 2 * KV_HALO)[None, :]
    o = kj - qi
    inside = (o >= 0) & (o <= 2 * side)
    return jnp.where(jnp.asarray(inside)[None], table[:, np.clip(o, 0, 2 * side)], NEG_INF).astype(F32)


def _attn_kernel(q_ref, kp_ref, kc_ref, kn_ref, vp_ref, vc_ref, vn_ref, bias_ref, o_ref, lse_ref,
                 *, regions, dil):
    tp, sp, ss = regions
    row0 = pl.program_id(1) * Q_BLOCK
    seq = _seq_len_at(row0, tp // dil, sp // dil, ss // dil)
    seq_start = row0 - lax.rem(row0, seq)
    nk = Q_BLOCK + 2 * KV_HALO
    key_row = row0 - KV_HALO + lax.broadcasted_iota(jnp.int32, (1, nk), 1)
    valid = (key_row >= seq_start) & (key_row < seq_start + seq)
    lane = lax.broadcasted_iota(jnp.int32, (Q_BLOCK, LANES), 1)
    scale = HEAD ** -0.5

    q = q_ref[...]
    k = jnp.concatenate([kp_ref[...], kc_ref[...], kn_ref[...]], 0)
    v = jnp.concatenate([vp_ref[...], vc_ref[...], vn_ref[...]], 0)
    for pair in range(GROUP_HEADS // 2):
        ls = slice(pair * LANES, (pair + 1) * LANES)
        qp, kpair, vpair = q[:, ls], k[:, ls], v[:, ls]
        o_pair = jnp.zeros((Q_BLOCK, LANES), F32)
        lse_pair = jnp.zeros((Q_BLOCK, LANES), F32)
        for sub in range(2):
            mine = (lane < HEAD) if sub == 0 else (lane >= HEAD)
            qm = jnp.where(mine, qp, jnp.zeros_like(qp))
            s = lax.dot_general(qm, kpair, (_NT, ((), ())), preferred_element_type=F32)
            s = s * scale + bias_ref[2 * pair + sub]
            s = jnp.where(valid, s, NEG_INF)
            m = jnp.max(s, axis=-1, keepdims=True)
            p = jnp.exp(s - m)
            den = jnp.sum(p, axis=-1, keepdims=True)
            pn = (p / den).astype(v.dtype)
            o = jnp.dot(pn, vpair, preferred_element_type=F32)
            o_pair = jnp.where(mine, o, o_pair)
            lse_pair = jnp.where(mine, m + jnp.log(den), lse_pair)
        o_ref[:, ls] = o_pair.astype(o_ref.dtype)
        lse_ref[:, ls] = lse_pair


def _attn_group(qkv, rel_bias, group, regions):
    window, dil = ATTN_GROUPS[group]
    t, width = qkv.shape
    n_groups = len(ATTN_GROUPS)
    rows = t // dil
    qkv_d = qkv.reshape(rows, dil * width)
    tiles = width // GROUP_WIDTH
    bias = _band_bias(rel_bias, group, window, dil)
    halo_per_q = Q_BLOCK // KV_HALO
    last_halo = rows // KV_HALO - 1

    def cur(part):
        return pl.BlockSpec((Q_BLOCK, GROUP_WIDTH), lambda rho, m: (m, rho * tiles + part * n_groups + group))

    def prev(part):
        return pl.BlockSpec((KV_HALO, GROUP_WIDTH),
                            lambda rho, m: (jnp.maximum(m * halo_per_q - 1, 0), rho * tiles + part * n_groups + group))

    def nxt(part):
        return pl.BlockSpec((KV_HALO, GROUP_WIDTH),
                            lambda rho, m: (jnp.minimum((m + 1) * halo_per_q, last_halo),
                                            rho * tiles + part * n_groups + group))

    out_spec = pl.BlockSpec((Q_BLOCK, GROUP_WIDTH), lambda rho, m: (m, rho))
    o, lse = pl.pallas_call(
        functools.partial(_attn_kernel, regions=regions, dil=dil),
        grid=(dil, rows // Q_BLOCK),
        in_specs=[cur(0), prev(1), cur(1), nxt(1), prev(2), cur(2), nxt(2), _const_spec(bias.shape)],
        out_specs=[out_spec, out_spec],
        out_shape=[jax.ShapeDtypeStruct((rows, dil * GROUP_WIDTH), BF16),
                   jax.ShapeDtypeStruct((rows, dil * GROUP_WIDTH), F32)],
        compiler_params=_cparams("arbitrary", "arbitrary"),
        name=f"attn_g{group}",
    )(qkv_d, qkv_d, qkv_d, qkv_d, qkv_d, qkv_d, qkv_d, bias)
    return o.reshape(t, GROUP_WIDTH), lse.reshape(t, GROUP_WIDTH)


def _attn_out_kernel(o0, o1, o2, l0, l1, l2, x_ref, wo_ref, out_ref):
    lses = [l0[...], l1[...], l2[...]]
    mx = jnp.maximum(jnp.maximum(lses[0], lses[1]), lses[2])
    ws = [jnp.exp(l - mx) for l in lses]
    tot = ws[0] + ws[1] + ws[2]
    acc = x_ref[...]
    for g, o_ref in enumerate((o0, o1, o2)):
        scaled = (o_ref[...].astype(F32) * (ws[g] / tot)).astype(BF16)
        acc = acc + jnp.dot(scaled, wo_ref[g], preferred_element_type=F32)
    out_ref[...] = acc


def _attn_out(os_, lses, x, w_o, tm=512):
    t, d = x.shape
    n_groups = len(os_)
    grp = pl.BlockSpec((tm, GROUP_WIDTH), lambda i: (i, 0))
    tok = pl.BlockSpec((tm, d), lambda i: (i, 0))
    wo = w_o.reshape(n_groups, GROUP_WIDTH, d).astype(BF16)
    return pl.pallas_call(
        _attn_out_kernel,
        grid=(t // tm,),
        in_specs=[grp] * (2 * n_groups) + [tok, _const_spec(wo.shape)],
        out_specs=tok,
        out_shape=jax.ShapeDtypeStruct((t, d), F32),
        compiler_params=_cparams("arbitrary"),
        name="attn_out",
    )(*os_, *lses, x, wo)


def kernel(x_prompt, x_sample, norm_mix, norm_ffn, rwkv_mu, rwkv_w_rkv, rwkv_w0, rwkv_w1, rwkv_w2, rwkv_a0, rwkv_a1, rwkv_a2, rwkv_g1, rwkv_g2, rwkv_k_k, rwkv_k_a, rwkv_r_k, rwkv_ln_g, rwkv_ln_b, rwkv_w_o, attn_w_in, attn_q_gain, attn_k_gain, attn_w_o, rel_bias, ffn_w_gate, ffn_w_up, ffn_w_down, moe_router, moe_w_gate, moe_w_up, moe_w_down):
    d = x_prompt.shape[-1]
    tp = x_prompt.shape[0] * x_prompt.shape[1]
    regions = (tp, x_prompt.shape[1], x_sample.shape[1])
    x = jnp.concatenate([x_prompt.reshape(-1, d), x_sample.reshape(-1, d)], 0)
    depth = norm_mix.shape[0]
    for i in range(depth):
        j = i // 2
        if i % 2 == 0:
            r, v, kk, g, bonus, logw, kdir, bdir = _rwkv_pre(
                x, regions, norm_mix[i], rwkv_mu[j], rwkv_w_rkv[j], rwkv_w0[j], rwkv_w1[j], rwkv_w2[j],
                rwkv_a0[j], rwkv_a1[j], rwkv_a2[j], rwkv_g1[j], rwkv_g2[j], rwkv_k_k[j], rwkv_k_a[j], rwkv_r_k[j])
            yf = _wkv(r, v, kk, logw, kdir, bdir, regions, reverse=False)
            yb = _wkv(r, v, kk, logw, kdir, bdir, regions, reverse=True)
            x = _rwkv_post(yf, yb, bonus, g, x, rwkv_ln_g[j], rwkv_ln_b[j], rwkv_w_o[j])
            x = _ffn(x, norm_ffn[i], ffn_w_gate[j], ffn_w_up[j], ffn_w_down[j])
        else:
            qkv = _attn_in(x, norm_mix[i], attn_w_in[j], attn_q_gain[j], attn_k_gain[j])
            outs = [_attn_group(qkv, rel_bias, g, regions) for g in range(len(ATTN_GROUPS))]
            x = _attn_out([o for o, _ in outs], [l for _, l in outs], x, attn_w_o[j])
            x = _moe(x, norm_ffn[i], moe_router[j], moe_w_gate[j], moe_w_up[j], moe_w_down[j])
    y_prompt = x[:tp].reshape(x_prompt.shape)
    y_sample = x[tp:].reshape(x_sample.shape)
    return (y_prompt, y_sample)
```

```python
import functools
import math

import numpy as np
import jax
import jax.numpy as jnp
from jax import lax
from jax.experimental import pallas as pl
from jax.experimental.pallas import tpu as pltpu

F32 = jnp.float32
BF16 = jnp.bfloat16

LANES = 128
VMEM_LIMIT_BYTES = 56 * 1024 * 1024

HEAD = 64
RMS_EPS = 1e-6
GN_EPS = 64e-5
ATTN_GROUPS = ((128, 1), (512, 4), (2048, 16))
GROUP_HEADS = 6
GROUP_WIDTH = GROUP_HEADS * HEAD
REL_BUCKETS = 32
REL_MAX_DIST = 1024
Q_BLOCK = 128
KV_HALO = 64
SCAN_CHUNK = 64
NEG_INF = -1e30


def _cparams(*sem):
    return pltpu.CompilerParams(dimension_semantics=sem, vmem_limit_bytes=VMEM_LIMIT_BYTES)


def _const_spec(shape):
    nd = len(shape)
    return pl.BlockSpec(shape, lambda *_: (0,) * nd, pipeline_mode=pl.Buffered(1))


def _split_dot(x, w):
    hi = x.astype(BF16)
    lo = (x - hi.astype(F32)).astype(BF16)
    return (jnp.dot(hi, w, preferred_element_type=F32)
            + jnp.dot(lo, w, preferred_element_type=F32))


def _head_sum(x, e_ref, et_ref):
    return _split_dot(_split_dot(x, e_ref[...]), et_ref[...])


def _rms(x, gain):
    return x * lax.rsqrt(jnp.mean(x * x, axis=-1, keepdims=True) + RMS_EPS) * gain


def _seq_len_at(row0, tp, sp, ss):
    return jnp.where(row0 < tp, sp, ss)


def _head_indicator(width):
    e = np.zeros((width, LANES), np.float32)
    e[np.arange(width), np.arange(width) // HEAD] = 1.0
    return jnp.asarray(e, BF16), jnp.asarray(e.T, BF16)


def _rwkv_pre_kernel(x_ref, xp_ref, xn_ref, gain_ref, mu_ref, wrkv_ref, wl1_ref, w2_ref, a2_ref,
                     g2_ref, w0_ref, a0_ref, kk_gain_ref, ka_ref, rk_ref, e_ref, et_ref,
                     r_out, v_out, kk_out, g_out, bonus_out, logw_out, kdir_out, bdir_out,
                     *, tm, regions):
    tp, sp, ss = regions
    d = x_ref.shape[-1]
    row0 = pl.program_id(0) * tm
    seq = _seq_len_at(row0, tp, sp, ss)
    at_start = lax.rem(row0, seq) == 0
    at_end = lax.rem(row0 + tm, seq) == 0

    gain = gain_ref[...]
    h = _rms(x_ref[...], gain)
    hp = _rms(xp_ref[...], gain)[7:8, :] * jnp.where(at_start, 0.0, 1.0)
    hn = _rms(xn_ref[...], gain)[0:1, :] * jnp.where(at_end, 0.0, 1.0)
    rows = lax.broadcasted_iota(jnp.int32, (tm, d), 0)
    h_prev = jnp.where(rows == 0, hp, pltpu.roll(h, 1, 0))
    h_next = jnp.where(rows == tm - 1, hn, pltpu.roll(h, tm - 1, 0))
    xx = 0.5 * (h_prev + h_next) - h

    def mixed(n):
        return (h + xx * mu_ref[n:n + 1, :]).astype(BF16)

    r = jnp.dot(mixed(0), wrkv_ref[0], preferred_element_type=F32)
    k = jnp.dot(mixed(1), wrkv_ref[1], preferred_element_type=F32)
    v = jnp.dot(mixed(2), wrkv_ref[2], preferred_element_type=F32)

    t_w = jnp.tanh(jnp.dot(mixed(3), wl1_ref[0], preferred_element_type=F32))
    lw = w0_ref[...] + jnp.dot(t_w.astype(BF16), w2_ref[...], preferred_element_type=F32)
    logw = -math.exp(-0.5) * jax.nn.sigmoid(lw)
    t_a = jnp.dot(mixed(4), wl1_ref[1], preferred_element_type=F32)
    a = jax.nn.sigmoid(a0_ref[...] + jnp.dot(t_a.astype(BF16), a2_ref[...], preferred_element_type=F32))
    t_g = jax.nn.sigmoid(jnp.dot(mixed(5), wl1_ref[2], preferred_element_type=F32))
    g = jnp.dot(t_g.astype(BF16), g2_ref[...], preferred_element_type=F32)

    kk = k * kk_gain_ref[...]
    kk = kk * lax.rsqrt(jnp.maximum(_head_sum(kk * kk, e_ref, et_ref), 1e-12))
    bonus = _head_sum(r * k * rk_ref[...], e_ref, et_ref) * v

    r_out[...] = r
    v_out[...] = v
    kk_out[...] = kk
    g_out[...] = g
    bonus_out[...] = bonus
    ka = ka_ref[...]
    for z in range(2):
        a_z = a[:, z * d:(z + 1) * d]
        logw_out[z] = logw[:, z * d:(z + 1) * d]
        kdir_out[z] = k * (1.0 + (a_z - 1.0) * ka)
        bdir_out[z] = kk * a_z


def _block_diag2(m):
    z = jnp.zeros_like(m[0])
    return jnp.concatenate([jnp.concatenate([m[0], z], 1), jnp.concatenate([z, m[1]], 1)], 0)


def _rwkv_pre(x, regions, gain, mu, w_rkv, w0, w1, w2, a0, a1, a2, g1, g2, k_k, k_a, r_k, tm=256):
    t, d = x.shape
    e, et = _head_indicator(d)
    wl1 = jnp.stack([jnp.concatenate([w1[0], w1[1]], 1), jnp.concatenate([a1[0], a1[1]], 1), g1]).astype(BF16)
    w2c = _block_diag2(w2).astype(BF16)
    a2c = _block_diag2(a2).astype(BF16)
    row = lambda p: p.reshape(1, -1).astype(F32)
    tok = pl.BlockSpec((tm, d), lambda i: (i, 0))
    halo_blocks = t // 8
    prev = pl.BlockSpec((8, d), lambda i: (jnp.maximum(i * (tm // 8) - 1, 0), 0))
    nxt = pl.BlockSpec((8, d), lambda i: (jnp.minimum((i + 1) * (tm // 8), halo_blocks - 1), 0))
    dir_tok = pl.BlockSpec((2, tm, d), lambda i: (0, i, 0))
    consts = [row(gain), mu.astype(F32), w_rkv.astype(BF16), wl1, w2c, a2c, g2.astype(BF16),
              row(w0), row(a0), row(k_k), row(k_a), row(r_k), e, et]
    tok_shape = jax.ShapeDtypeStruct((t, d), F32)
    dir_shape = jax.ShapeDtypeStruct((2, t, d), F32)
    return pl.pallas_call(
        functools.partial(_rwkv_pre_kernel, tm=tm, regions=regions),
        grid=(t // tm,),
        in_specs=[tok, prev, nxt] + [_const_spec(c.shape) for c in consts],
        out_specs=[tok] * 5 + [dir_tok] * 3,
        out_shape=[tok_shape] * 5 + [dir_shape] * 3,
        compiler_params=_cparams("arbitrary"),
        name="rwkv_pre",
    )(x, x, x, *consts)


_NN = ((1,), (0,))
_NT = ((1,), (1,))
_TN = ((0,), (0,))


def _mm(a, b, dims=_NN):
    return lax.dot_general(a.astype(BF16), b.astype(BF16), (dims, ((), ())), preferred_element_type=F32)


def _cumsum_mm(tri, x):
    x0 = x.astype(BF16)
    r1 = x - x0.astype(F32)
    x1 = r1.astype(BF16)
    x2 = (r1 - x1.astype(F32)).astype(BF16)
    return (jnp.dot(tri, x0, preferred_element_type=F32) + jnp.dot(tri, x1, preferred_element_type=F32)
            + jnp.dot(tri, x2, preferred_element_type=F32))


def _wkv_kernel(r_ref, v_ref, kk_ref, logw_ref, kdir_ref, bdir_ref, y_ref, state_ref,
                *, rb, regions, reverse):
    tp, sp, ss = regions
    c = SCAN_CHUNK
    npairs = r_ref.shape[-1] // LANES
    nsteps = pl.num_programs(1)
    step = pl.program_id(1)
    blk = (nsteps - 1 - step) if reverse else step
    row0 = blk * rb
    seq = _seq_len_at(row0, tp, sp, ss)
    first = (lax.rem(row0 + rb, seq) == 0) if reverse else (lax.rem(row0, seq) == 0)

    @pl.when(first)
    def _():
        state_ref[...] = jnp.zeros_like(state_ref)

    ti = lax.broadcasted_iota(jnp.int32, (c, c), 0)
    tj = lax.broadcasted_iota(jnp.int32, (c, c), 1)
    incl = (tj >= ti) if reverse else (tj <= ti)
    incl_b = incl.astype(BF16)
    ti2 = lax.broadcasted_iota(jnp.int32, (c, 2 * c), 0)
    tj2 = lax.broadcasted_iota(jnp.int32, (c, 2 * c), 1) & (c - 1)
    ti4 = lax.broadcasted_iota(jnp.int32, (c, 4 * c), 0)
    tj4 = lax.broadcasted_iota(jnp.int32, (c, 4 * c), 1) & (c - 1)
    incl4 = (tj4 >= ti4) if reverse else (tj4 <= ti4)
    strict2 = (tj2 > ti2) if reverse else (tj2 < ti2)
    lane = lax.broadcasted_iota(jnp.int32, (c, LANES), 1)
    head0 = lane < HEAD
    rr = lax.broadcasted_iota(jnp.int32, (LANES, LANES), 0)
    cc = lax.broadcasted_iota(jnp.int32, (LANES, LANES), 1)
    same_head = (rr < HEAD) == (cc < HEAD)
    eye = (rr == cc).astype(F32)

    def stack_heads(m):
        return jnp.concatenate([jnp.where(head0, m, 0.0), jnp.where(head0, 0.0, m)], 0)

    order = list(range(rb // c - 1, -1, -1) if reverse else range(rb // c))
    units = [(pl.ds(ci * c, c), pl.ds(p * LANES, LANES)) for ci in order for p in range(npairs)]
    each = lambda fn, *lists: [fn(*xs) for xs in zip(*lists)]

    lw = [logw_ref[u] for u in units]
    cum = each(lambda x: _cumsum_mm(incl_b, x), lw)
    total = each(lambda x: jnp.sum(x, axis=0, keepdims=True), lw)
    lhs = [jnp.concatenate([r_ref[u] * jnp.exp(cm), kk_ref[u] * jnp.exp(cm - x)], 0).astype(BF16)
           for u, cm, x in zip(units, cum, lw)]
    rhs = []
    kb_tail = []
    for u, cm, tt in zip(units, cum, total):
        kd, bd = kdir_ref[u], bdir_ref[u]
        inv = jnp.exp(-cm)
        tail = jnp.exp(tt - cm)
        rhs.append(jnp.concatenate([stack_heads(kd * inv), stack_heads(bd * inv)], 0).astype(BF16))
        kb_tail.append(jnp.concatenate([kd * tail, -(bd * tail)], 0).astype(BF16))
    q = each(lambda a, b: _mm(a, b, _NT), lhs, rhs)
    a_out = [jnp.where(incl4, x[0:c], 0.0).astype(BF16) for x in q]
    a_kk = [jnp.where(strict2, x[c:2 * c, 0:2 * c], 0.0).astype(BF16) for x in q]
    pw = [-stack_heads(jnp.where(strict2, x[c:2 * c, 2 * c:4 * c], 0.0)) for x in q]
    inv_l = [eye + x for x in pw]
    for _ in range(int(math.log2(c)) - 1):
        pw = each(lambda x: _mm(x, x), pw)
        inv_l = each(lambda t, x: t + _mm(t, x), inv_l, pw)
    inv_l = [x.astype(BF16) for x in inv_l]
    v_all = [v_ref[u] for u in units]
    v_bd = [stack_heads(x).astype(BF16) for x in v_all]
    akv = each(_mm, a_kk, v_bd)
    decay = [jnp.exp(x) for x in total]

    states = [state_ref[p] for p in range(npairs)]
    for k in range(len(order)):
        idx = list(range(k * npairs, (k + 1) * npairs))
        sk = [_mm(lhs[i], states[p], _NT) for p, i in enumerate(idx)]
        u_bd = [_mm(inv_l[i], stack_heads(s[c:2 * c] + akv[i])) for s, i in zip(sk, idx)]
        y = [s[0:c] + _mm(a_out[i], jnp.concatenate([v_bd[i], (-ub).astype(BF16)], 0))
             for s, ub, i in zip(sk, u_bd, idx)]
        for yy, i in zip(y, idx):
            y_ref[units[i]] = yy
        upd = [_mm(jnp.concatenate([v_all[i], ub[0:c] + ub[c:2 * c]], 0), kb_tail[i], _TN)
               for ub, i in zip(u_bd, idx)]
        states = [states[p] * decay[i] + jnp.where(same_head, upd[p], 0.0) for p, i in enumerate(idx)]
    for p in range(npairs):
        state_ref[p] = states[p]


def _wkv(r, v, kk, logw, kdir, bdir, regions, reverse, rb=256, width=512):
    t, d = r.shape
    z = 1 if reverse else 0
    nsteps = t // rb

    def row_block(s):
        return (nsteps - 1 - s) if reverse else s

    tok = pl.BlockSpec((rb, width), lambda j, s: (row_block(s), j))
    dir_tok = pl.BlockSpec((None, rb, width), lambda j, s: (z, row_block(s), j))
    return pl.pallas_call(
        functools.partial(_wkv_kernel, rb=rb, regions=regions, reverse=reverse),
        grid=(d // width, nsteps),
        in_specs=[tok, tok, tok, dir_tok, dir_tok, dir_tok],
        out_specs=tok,
        out_shape=jax.ShapeDtypeStruct((t, d), F32),
        scratch_shapes=[pltpu.VMEM((width // LANES, LANES, LANES), F32)],
        compiler_params=_cparams("arbitrary", "arbitrary"),
        name="wkv_bwd" if reverse else "wkv_fwd",
    )(r, v, kk, logw, kdir, bdir)


def _rwkv_post_kernel(yf_ref, yb_ref, bonus_ref, g_ref, x_ref, lng_ref, lnb_ref, wo_ref, e_ref, et_ref, out_ref):
    y = yf_ref[...] + yb_ref[...]
    mean = _head_sum(y, e_ref, et_ref) * (1.0 / HEAD)
    dlt = y - mean
    var = _head_sum(dlt * dlt, e_ref, et_ref) * (1.0 / HEAD)
    yn = dlt * lax.rsqrt(var + GN_EPS) * lng_ref[...] + lnb_ref[...]
    mixed = ((yn + bonus_ref[...]) * g_ref[...]).astype(BF16)
    out_ref[...] = x_ref[...] + jnp.dot(mixed, wo_ref[...], preferred_element_type=F32)


def _rwkv_post(yf, yb, bonus, g, x, ln_g, ln_b, w_o, tm=256):
    t, d = x.shape
    e, et = _head_indicator(d)
    tok = pl.BlockSpec((tm, d), lambda i: (i, 0))
    consts = [ln_g.reshape(1, d), ln_b.reshape(1, d), w_o.astype(BF16), e, et]
    return pl.pallas_call(
        _rwkv_post_kernel,
        grid=(t // tm,),
        in_specs=[tok] * 5 + [_const_spec(c.shape) for c in consts],
        out_specs=tok,
        out_shape=jax.ShapeDtypeStruct((t, d), F32),
        compiler_params=_cparams("arbitrary"),
        name="rwkv_post",
    )(yf, yb, bonus, g, x, *consts)


def _ffn_kernel(x_ref, gain_ref, wg_ref, wu_ref, wd_ref, out_ref, h_ref, acc_ref):
    f = pl.program_id(1)

    @pl.when(f == 0)
    def _():
        h_ref[...] = _rms(x_ref[...], gain_ref[...]).astype(BF16)
        acc_ref[...] = jnp.zeros_like(acc_ref)

    h = h_ref[...]
    gate = jnp.dot(h, wg_ref[...], preferred_element_type=F32)
    up = jnp.dot(h, wu_ref[...], preferred_element_type=F32)
    hidden = (jax.nn.silu(gate) * up).astype(BF16)
    acc_ref[...] += jnp.dot(hidden, wd_ref[...], preferred_element_type=F32)

    @pl.when(f == pl.num_programs(1) - 1)
    def _():
        out_ref[...] = x_ref[...] + acc_ref[...]


def _ffn(x, gain, w_gate, w_up, w_down, tm=512, tf=1408):
    t, d = x.shape
    dff = w_gate.shape[1]
    return pl.pallas_call(
        _ffn_kernel,
        grid=(t // tm, dff // tf),
        in_specs=[pl.BlockSpec((tm, d), lambda i, f: (i, 0)),
                  _const_spec((1, d)),
                  pl.BlockSpec((d, tf), lambda i, f: (0, f)),
                  pl.BlockSpec((d, tf), lambda i, f: (0, f)),
                  pl.BlockSpec((tf, d), lambda i, f: (f, 0))],
        out_specs=pl.BlockSpec((tm, d), lambda i, f: (i, 0)),
        out_shape=jax.ShapeDtypeStruct((t, d), F32),
        scratch_shapes=[pltpu.VMEM((tm, d), BF16), pltpu.VMEM((tm, d), F32)],
        compiler_params=_cparams("arbitrary", "arbitrary"),
        name="ffn",
    )(x, gain.reshape(1, d), w_gate.astype(BF16), w_up.astype(BF16), w_down.astype(BF16))


def _router_kernel(x_ref, gain_ref, wr_ref, tri_ref, h_out, comb_out, rank_out, count_out, carry_ref,
                   *, n_experts):
    @pl.when(pl.program_id(0) == 0)
    def _():
        carry_ref[...] = jnp.zeros_like(carry_ref)

    h = _rms(x_ref[...], gain_ref[...])
    h_out[...] = h
    logits = jnp.dot(h.astype(BF16), wr_ref[...], preferred_element_type=F32)
    lane = lax.broadcasted_iota(jnp.int32, logits.shape, 1)
    logits = jnp.where(lane < n_experts, logits, -jnp.inf)
    m1 = jnp.max(logits, axis=-1, keepdims=True)
    i1 = jnp.min(jnp.where(logits == m1, lane, LANES), axis=-1, keepdims=True)
    rest = jnp.where(lane == i1, -jnp.inf, logits)
    m2 = jnp.max(rest, axis=-1, keepdims=True)
    i2 = jnp.min(jnp.where(rest == m2, lane, LANES), axis=-1, keepdims=True)
    e2 = jnp.exp(m2 - m1)
    g1 = 1.0 / (1.0 + e2)
    g2 = e2 / (1.0 + e2)
    comb_out[...] = jnp.where(lane == i1, g1, 0.0) + jnp.where(lane == i2, g2, 0.0)
    sel = jnp.where((lane == i1) | (lane == i2), 1.0, 0.0)
    earlier = carry_ref[0:1, :] + jnp.dot(tri_ref[...], sel.astype(BF16), preferred_element_type=F32)
    rank_out[...] = sel * (earlier + 1.0)
    carry_ref[...] = carry_ref[...] + jnp.sum(sel, axis=0, keepdims=True)
    count_out[...] = carry_ref[...]


def _router(x, gain, w_router, tm=512):
    t, d = x.shape
    n_experts = w_router.shape[1]
    wr = jnp.zeros((d, LANES), F32).at[:, :n_experts].set(w_router).astype(BF16)
    tri = jnp.asarray(np.tril(np.ones((tm, tm), np.float32), -1), BF16)
    tok = pl.BlockSpec((tm, d), lambda i: (i, 0))
    lanes = pl.BlockSpec((tm, LANES), lambda i: (i, 0))
    return pl.pallas_call(
        functools.partial(_router_kernel, n_experts=n_experts),
        grid=(t // tm,),
        in_specs=[tok, _const_spec((1, d)), _const_spec((d, LANES)), _const_spec((tm, tm))],
        out_specs=[tok, lanes, lanes, pl.BlockSpec((8, LANES), lambda i: (0, 0))],
        out_shape=[jax.ShapeDtypeStruct((t, d), F32), jax.ShapeDtypeStruct((t, LANES), F32),
                   jax.ShapeDtypeStruct((t, LANES), F32), jax.ShapeDtypeStruct((8, LANES), F32)],
        scratch_shapes=[pltpu.VMEM((8, LANES), F32)],
        compiler_params=_cparams("arbitrary"),
        name="moe_router",
    )(x, gain.reshape(1, d), wr, tri)


def _route_plan(comb, rank, counts, n_experts, tile):
    t = comb.shape[0]
    n_tiles = (2 * t) // tile + n_experts
    cnt = counts[0, :n_experts].astype(jnp.int32)
    padded = ((cnt + tile - 1) // tile) * tile
    ends = jnp.cumsum(padded)
    starts = jnp.zeros((LANES,), jnp.int32).at[:n_experts].set(ends - padded)
    slot = jnp.where(rank > 0, starts[None, :] + rank.astype(jnp.int32) - 1, -1)
    slot_hi = jnp.max(slot, axis=1)
    slot_lo = jnp.min(jnp.where(slot >= 0, slot, jnp.iinfo(jnp.int32).max), axis=1)
    gate_lo = jnp.sum(jnp.where(slot == slot_lo[:, None], comb, 0.0), axis=1)
    gate_hi = jnp.sum(jnp.where(slot == slot_hi[:, None], comb, 0.0), axis=1)
    slots = jnp.stack([slot_lo, slot_hi], 1)
    gates = jnp.stack([gate_lo, gate_hi], 1)
    tile_expert = jnp.sum(jnp.arange(n_tiles, dtype=jnp.int32)[:, None] * tile >= ends[None, :], axis=1)
    return slots, gates, jnp.minimum(tile_expert, n_experts - 1).astype(jnp.int32), n_tiles


def _row_copy(src, src_row, dst, dst_row, sem):
    return pltpu.make_async_copy(src.at[pl.ds(src_row, 1)], dst.at[pl.ds(dst_row, 1)], sem)


def _dispatch_kernel(slots_ref, h_ref, init_hbm, xs_hbm, sem, *, tm):
    del init_hbm

    def issue(i, carry):
        for k in range(2):
            _row_copy(h_ref, i, xs_hbm, slots_ref[0, 2 * i + k], sem).start()
        return carry

    lax.fori_loop(0, tm, issue, 0)
    for _ in range(2):
        pltpu.make_async_copy(h_ref, xs_hbm.at[pl.ds(0, tm)], sem).wait()


def _dispatch(h, slots, n_rows, tm=512):
    t, d = h.shape
    return pl.pallas_call(
        functools.partial(_dispatch_kernel, tm=tm),
        grid=(t // tm,),
        in_specs=[pl.BlockSpec((None, 1, 2 * tm), lambda i: (i, 0, 0), memory_space=pltpu.SMEM),
                  pl.BlockSpec((tm, d), lambda i: (i, 0)), pl.BlockSpec(memory_space=pl.ANY)],
        out_specs=pl.BlockSpec(memory_space=pl.ANY),
        out_shape=jax.ShapeDtypeStruct((n_rows, d), h.dtype),
        scratch_shapes=[pltpu.SemaphoreType.DMA(())],
        input_output_aliases={2: 0},
        compiler_params=pltpu.CompilerParams(dimension_semantics=("arbitrary",), has_side_effects=True),
        name="moe_dispatch",
    )(slots.reshape(t // tm, 1, 2 * tm), h, jnp.zeros((n_rows, d), h.dtype))


def _experts_kernel(te_ref, xs_ref, wg_ref, wu_ref, wd_ref, ys_ref, xb_ref, acc_ref):
    del te_ref
    f = pl.program_id(1)

    @pl.when(f == 0)
    def _():
        xb_ref[...] = xs_ref[...].astype(BF16)
        acc_ref[...] = jnp.zeros_like(acc_ref)

    xb = xb_ref[...]
    gate = jnp.dot(xb, wg_ref[...], preferred_element_type=F32)
    up = jnp.dot(xb, wu_ref[...], preferred_element_type=F32)
    hidden = (jax.nn.silu(gate) * up).astype(BF16)
    acc_ref[...] += jnp.dot(hidden, wd_ref[...], preferred_element_type=F32)

    @pl.when(f == pl.num_programs(1) - 1)
    def _():
        ys_ref[...] = acc_ref[...]


def _experts(xs, tile_expert, w_gate, w_up, w_down, tile, tf=512):
    n_rows, d = xs.shape
    dff = w_gate.shape[2]
    grid_spec = pltpu.PrefetchScalarGridSpec(
        num_scalar_prefetch=1,
        grid=(n_rows // tile, dff // tf),
        in_specs=[pl.BlockSpec((tile, d), lambda i, f, te: (i, 0)),
                  pl.BlockSpec((None, d, tf), lambda i, f, te: (te[i], 0, f)),
                  pl.BlockSpec((None, d, tf), lambda i, f, te: (te[i], 0, f)),
                  pl.BlockSpec((None, tf, d), lambda i, f, te: (te[i], f, 0))],
        out_specs=pl.BlockSpec((tile, d), lambda i, f, te: (i, 0)),
        scratch_shapes=[pltpu.VMEM((tile, d), BF16), pltpu.VMEM((tile, d), F32)])
    return pl.pallas_call(
        _experts_kernel,
        grid_spec=grid_spec,
        out_shape=jax.ShapeDtypeStruct((n_rows, d), F32),
        compiler_params=_cparams("arbitrary", "arbitrary"),
        name="moe_experts",
    )(tile_expert, xs, w_gate.astype(BF16), w_up.astype(BF16), w_down.astype(BF16))


def _combine_kernel(slots_ref, x_ref, gates_ref, ys_hbm, out_ref, buf_ref, sem, *, tm):
    def issue(i, carry):
        for k in range(2):
            _row_copy(ys_hbm, slots_ref[0, 2 * i + k], buf_ref.at[k], i, sem).start()
        return carry

    lax.fori_loop(0, tm, issue, 0)
    for k in range(2):
        pltpu.make_async_copy(ys_hbm.at[pl.ds(0, tm)], buf_ref.at[k], sem).wait()
    gates = gates_ref[...]
    out_ref[...] = x_ref[...] + gates[:, 0:1] * buf_ref[0] + gates[:, 1:2] * buf_ref[1]


def _combine(x, ys, slots, gates, tm=256):
    t, d = x.shape
    tok = pl.BlockSpec((tm, d), lambda i: (i, 0))
    return pl.pallas_call(
        functools.partial(_combine_kernel, tm=tm),
        grid=(t // tm,),
        in_specs=[pl.BlockSpec((None, 1, 2 * tm), lambda i: (i, 0, 0), memory_space=pltpu.SMEM),
                  tok, pl.BlockSpec((tm, 2), lambda i: (i, 0)), pl.BlockSpec(memory_space=pl.ANY)],
        out_specs=tok,
        out_shape=jax.ShapeDtypeStruct((t, d), F32),
        scratch_shapes=[pltpu.VMEM((2, tm, d), F32), pltpu.SemaphoreType.DMA(())],
        compiler_params=_cparams("arbitrary"),
        name="moe_combine",
    )(slots.reshape(t // tm, 1, 2 * tm), x, gates, ys)


def _moe(x, gain, w_router, w_gate, w_up, w_down, tile=512):
    n_experts = w_gate.shape[0]
    h, comb, rank, counts = _router(x, gain, w_router)
    slots, gates, tile_expert, n_tiles = _route_plan(comb, rank, counts, n_experts, tile)
    xs = _dispatch(h, slots, n_tiles * tile)
    ys = _experts(xs, tile_expert, w_gate, w_up, w_down, tile)
    return _combine(x, ys, slots, gates)


def _lane_chunks(width):
    return [slice(c * LANES, (c + 1) * LANES) for c in range(width // LANES)]


def _store_dilated(dst_ref, stage_ref, value, dil):
    rows, width = value.shape
    if dil == 1:
        dst_ref[...] = value.astype(dst_ref.dtype)
        return
    for c, ls in enumerate(_lane_chunks(width)):
        stage_ref[c] = value[:, ls]
    for rho in range(dil):
        for c in range(width // LANES):
            dst_ref[:, pl.ds(rho * width + c * LANES, LANES)] = (
                stage_ref.at[c][pl.ds(rho, rows // dil, stride=dil), :].astype(dst_ref.dtype))


def _load_dilated(src_ref, stage_ref, base, rows, width, dil):
    if dil == 1:
        return src_ref[...].astype(F32)
    for rho in range(dil):
        for c in range(width // LANES):
            stage_ref.at[base + c][pl.ds(rho, rows // dil, stride=dil), :] = (
                src_ref[:, pl.ds(rho * width + c * LANES, LANES)].astype(F32))
    return jnp.concatenate([stage_ref[base + c] for c in range(width // LANES)], 1)


def _attn_in_kernel(x_ref, gain_ref, w_ref, hg_ref, e_ref, et_ref, *rest, dils):
    outs, (h_ref, stage_ref) = rest[:len(dils)], rest[len(dils):]
    g = pl.program_id(1)

    @pl.when(g == 0)
    def _():
        h_ref[...] = _rms(x_ref[...], gain_ref[...]).astype(BF16)

    y = jnp.dot(h_ref[...], w_ref[...], preferred_element_type=F32)
    n_qk = 2 * GROUP_WIDTH
    qk = y[:, :n_qk]
    ms = _head_sum(qk * qk, e_ref, et_ref) * (1.0 / HEAD)
    qk = qk * lax.rsqrt(ms + RMS_EPS) * hg_ref[...]
    y = jnp.concatenate([qk, y[:, n_qk:]], 1)
    for gi, dil in enumerate(dils):
        @pl.when(g == gi)
        def _(gi=gi, dil=dil):
            _store_dilated(outs[gi], stage_ref, y, dil)


def _attn_in(x, gain, w_in, q_gain, k_gain, tm=512):
    t, d = x.shape
    n_groups = q_gain.shape[0]
    gw = 3 * GROUP_WIDTH
    dils = tuple(dil for _, dil in ATTN_GROUPS)
    w = jnp.transpose(w_in.reshape(d, 3, n_groups, GROUP_WIDTH), (2, 0, 1, 3)).reshape(n_groups, d, gw).astype(BF16)
    head_gain = jnp.concatenate([jnp.tile(q_gain, (1, GROUP_HEADS)), jnp.tile(k_gain, (1, GROUP_HEADS))],
                                1).reshape(n_groups, 1, 2 * GROUP_WIDTH)
    e, et = _head_indicator(2 * GROUP_WIDTH)
    return pl.pallas_call(
        functools.partial(_attn_in_kernel, dils=dils),
        grid=(t // tm, n_groups),
        in_specs=[pl.BlockSpec((tm, d), lambda i, g: (i, 0)),
                  _const_spec((1, d)),
                  pl.BlockSpec((None, d, gw), lambda i, g: (g, 0, 0)),
                  pl.BlockSpec((None, 1, 2 * GROUP_WIDTH), lambda i, g: (g, 0, 0)),
                  _const_spec(e.shape), _const_spec(et.shape)],
        out_specs=[pl.BlockSpec((tm // dil, dil * gw), lambda i, g: (i, 0)) for dil in dils],
        out_shape=[jax.ShapeDtypeStruct((t // dil, dil * gw), BF16) for dil in dils],
        scratch_shapes=[pltpu.VMEM((tm, d), BF16), pltpu.VMEM((gw // LANES, tm, LANES), F32)],
        compiler_params=_cparams("arbitrary", "arbitrary"),
        name="attn_in",
    )(x, gain.reshape(1, d), w, head_gain, e, et)


def _t5_bucket_np(rel):
    nb = REL_BUCKETS // 2
    max_exact = nb // 2
    n = np.abs(rel)
    large = max_exact + (np.log(np.maximum(n, 1).astype(np.float32) / max_exact)
                         / math.log(REL_MAX_DIST / max_exact) * (nb - max_exact)).astype(np.int32)
    large = np.minimum(large, nb - 1)
    return np.where(rel > 0, nb, 0) + np.where(n < max_exact, n, large)


def _bias_table(rel_bias, group, window, dil):
    side = window // (2 * dil)
    assert side == KV_HALO
    buckets = _t5_bucket_np(dil * np.arange(-side, side + 1))
    table = rel_bias.reshape(REL_BUCKETS, -1, GROUP_HEADS)[buckets, group].T
    full = jnp.full((8, Q_BLOCK + 2 * KV_HALO), NEG_INF, F32)
    return full.at[:GROUP_HEADS, :2 * side + 1].set(table)


def _attn_kernel(q_ref, kp_ref, kc_ref, kn_ref, vp_ref, vc_ref, vn_ref, table_ref, o_ref, lse_ref, bias_ref,
                 *, regions, dil):
    tp, sp, ss = regions
    nk = Q_BLOCK + 2 * KV_HALO

    @pl.when((pl.program_id(0) == 0) & (pl.program_id(1) == 0))
    def _():
        for h in range(GROUP_HEADS):
            row = jnp.broadcast_to(table_ref[h:h + 1, :], (Q_BLOCK, nk))
            bias_ref[h] = pltpu.roll(row, 0, 1, stride=1, stride_axis=0)

    row0 = pl.program_id(1) * Q_BLOCK
    seq = _seq_len_at(row0, tp // dil, sp // dil, ss // dil)
    seq_start = row0 - lax.rem(row0, seq)
    key_row = row0 - KV_HALO + lax.broadcasted_iota(jnp.int32, (1, nk), 1)
    valid = (key_row >= seq_start) & (key_row < seq_start + seq)
    lane = lax.broadcasted_iota(jnp.int32, (Q_BLOCK, LANES), 1)
    scale = HEAD ** -0.5

    q = q_ref[...]
    k = jnp.concatenate([kp_ref[...], kc_ref[...], kn_ref[...]], 0)
    v = jnp.concatenate([vp_ref[...], vc_ref[...], vn_ref[...]], 0)
    for pair in range(GROUP_HEADS // 2):
        ls = slice(pair * LANES, (pair + 1) * LANES)
        qp, kpair, vpair = q[:, ls], k[:, ls], v[:, ls]
        o_pair = jnp.zeros((Q_BLOCK, LANES), F32)
        lse_pair = jnp.zeros((Q_BLOCK, LANES), F32)
        for sub in range(2):
            mine = (lane < HEAD) if sub == 0 else (lane >= HEAD)
            qm = jnp.where(mine, qp, jnp.zeros_like(qp))
            s = lax.dot_general(qm, kpair, (_NT, ((), ())), preferred_element_type=F32)
            s = s * scale + bias_ref[2 * pair + sub]
            s = jnp.where(valid, s, NEG_INF)
            m = jnp.max(s, axis=-1, keepdims=True)
            p = jnp.exp(s - m)
            den = jnp.sum(p, axis=-1, keepdims=True)
            pn = (p / den).astype(v.dtype)
            o = jnp.dot(pn, vpair, preferred_element_type=F32)
            o_pair = jnp.where(mine, o, o_pair)
            lse_pair = jnp.where(mine, m + jnp.log(den), lse_pair)
        o_ref[:, ls] = o_pair.astype(o_ref.dtype)
        lse_ref[:, ls] = lse_pair


def _attn_group(qkv_d, rel_bias, group, regions):
    window, dil = ATTN_GROUPS[group]
    rows = qkv_d.shape[0]
    parts = 3
    table = _bias_table(rel_bias, group, window, dil)
    halo_per_q = Q_BLOCK // KV_HALO
    last_halo = rows // KV_HALO - 1

    def cur(part):
        return pl.BlockSpec((Q_BLOCK, GROUP_WIDTH), lambda rho, m: (m, rho * parts + part))

    def prev(part):
        return pl.BlockSpec((KV_HALO, GROUP_WIDTH),
                            lambda rho, m: (jnp.maximum(m * halo_per_q - 1, 0), rho * parts + part))

    def nxt(part):
        return pl.BlockSpec((KV_HALO, GROUP_WIDTH),
                            lambda rho, m: (jnp.minimum((m + 1) * halo_per_q, last_halo), rho * parts + part))

    out_spec = pl.BlockSpec((Q_BLOCK, GROUP_WIDTH), lambda rho, m: (m, rho))
    return pl.pallas_call(
        functools.partial(_attn_kernel, regions=regions, dil=dil),
        grid=(dil, rows // Q_BLOCK),
        in_specs=[cur(0), prev(1), cur(1), nxt(1), prev(2), cur(2), nxt(2), _const_spec(table.shape)],
        out_specs=[out_spec, out_spec],
        out_shape=[jax.ShapeDtypeStruct((rows, dil * GROUP_WIDTH), BF16),
                   jax.ShapeDtypeStruct((rows, dil * GROUP_WIDTH), F32)],
        scratch_shapes=[pltpu.VMEM((GROUP_HEADS, Q_BLOCK, Q_BLOCK + 2 * KV_HALO), F32)],
        compiler_params=_cparams("arbitrary", "arbitrary"),
        name=f"attn_g{group}",
    )(qkv_d, qkv_d, qkv_d, qkv_d, qkv_d, qkv_d, qkv_d, table)


def _attn_out_kernel(o0, o1, o2, l0, l1, l2, x_ref, wo_ref, out_ref, stage_ref, *, dils):
    tm = x_ref.shape[0]
    chunks = GROUP_WIDTH // LANES
    os_ = [_load_dilated(r, stage_ref, 2 * g * chunks, tm, GROUP_WIDTH, dils[g]) for g, r in enumerate((o0, o1, o2))]
    lses = [_load_dilated(r, stage_ref, (2 * g + 1) * chunks, tm, GROUP_WIDTH, dils[g])
            for g, r in enumerate((l0, l1, l2))]
    mx = jnp.maximum(jnp.maximum(lses[0], lses[1]), lses[2])
    ws = [jnp.exp(l - mx) for l in lses]
    tot = ws[0] + ws[1] + ws[2]
    acc = x_ref[...]
    for g in range(len(dils)):
        scaled = (os_[g] * (ws[g] / tot)).astype(BF16)
        acc = acc + jnp.dot(scaled, wo_ref[g], preferred_element_type=F32)
    out_ref[...] = acc


def _attn_out(os_, lses, x, w_o, tm=512):
    t, d = x.shape
    n_groups = len(os_)
    dils = tuple(dil for _, dil in ATTN_GROUPS)
    grp = [pl.BlockSpec((tm // dil, dil * GROUP_WIDTH), lambda i: (i, 0)) for dil in dils]
    tok = pl.BlockSpec((tm, d), lambda i: (i, 0))
    wo = w_o.reshape(n_groups, GROUP_WIDTH, d).astype(BF16)
    return pl.pallas_call(
        functools.partial(_attn_out_kernel, dils=dils),
        grid=(t // tm,),
        in_specs=grp + grp + [tok, _const_spec(wo.shape)],
        out_specs=tok,
        out_shape=jax.ShapeDtypeStruct((t, d), F32),
        scratch_shapes=[pltpu.VMEM((2 * n_groups * (GROUP_WIDTH // LANES), tm, LANES), F32)],
        compiler_params=_cparams("arbitrary"),
        name="attn_out",
    )(*os_, *lses, x, wo)


def kernel(x_prompt, x_sample, norm_mix, norm_ffn, rwkv_mu, rwkv_w_rkv, rwkv_w0, rwkv_w1, rwkv_w2, rwkv_a0, rwkv_a1, rwkv_a2, rwkv_g1, rwkv_g2, rwkv_k_k, rwkv_k_a, rwkv_r_k, rwkv_ln_g, rwkv_ln_b, rwkv_w_o, attn_w_in, attn_q_gain, attn_k_gain, attn_w_o, rel_bias, ffn_w_gate, ffn_w_up, ffn_w_down, moe_router, moe_w_gate, moe_w_up, moe_w_down):
    d = x_prompt.shape[-1]
    tp = x_prompt.shape[0] * x_prompt.shape[1]
    regions = (tp, x_prompt.shape[1], x_sample.shape[1])
    x = jnp.concatenate([x_prompt.reshape(-1, d), x_sample.reshape(-1, d)], 0)
    depth = norm_mix.shape[0]
    for i in range(depth):
        j = i // 2
        if i % 2 == 0:
            r, v, kk, g, bonus, logw, kdir, bdir = _rwkv_pre(
                x, regions, norm_mix[i], rwkv_mu[j], rwkv_w_rkv[j], rwkv_w0[j], rwkv_w1[j], rwkv_w2[j],
                rwkv_a0[j], rwkv_a1[j], rwkv_a2[j], rwkv_g1[j], rwkv_g2[j], rwkv_k_k[j], rwkv_k_a[j], rwkv_r_k[j])
            yf = _wkv(r, v, kk, logw, kdir, bdir, regions, reverse=False)
            yb = _wkv(r, v, kk, logw, kdir, bdir, regions, reverse=True)
            x = _rwkv_post(yf, yb, bonus, g, x, rwkv_ln_g[j], rwkv_ln_b[j], rwkv_w_o[j])
            x = _ffn(x, norm_ffn[i], ffn_w_gate[j], ffn_w_up[j], ffn_w_down[j])
        else:
            qkv = _attn_in(x, norm_mix[i], attn_w_in[j], attn_q_gain[j], attn_k_gain[j])
            outs = [_attn_group(qkv[g], rel_bias, g, regions) for g in range(len(ATTN_GROUPS))]
            x = _attn_out([o for o, _ in outs], [l for _, l in outs], x, attn_w_o[j])
            x = _moe(x, norm_ffn[i], moe_router[j], moe_w_gate[j], moe_w_up[j], moe_w_down[j])
    y_prompt = x[:tp].reshape(x_prompt.shape)
    y_sample = x[tp:].reshape(x_sample.shape)
    return (y_prompt, y_sample)
```

```python
import functools
import math

import numpy as np
import jax
import jax.numpy as jnp
from jax import lax
from jax.experimental import pallas as pl
from jax.experimental.pallas import tpu as pltpu

F32 = jnp.float32
BF16 = jnp.bfloat16

LANES = 128
VMEM_LIMIT_BYTES = 56 * 1024 * 1024

HEAD = 64
RMS_EPS = 1e-6
GN_EPS = 64e-5
ATTN_GROUPS = ((128, 1), (512, 4), (2048, 16))
GROUP_HEADS = 6
GROUP_WIDTH = GROUP_HEADS * HEAD
REL_BUCKETS = 32
REL_MAX_DIST = 1024
Q_BLOCK = 128
KV_HALO = 64
SCAN_CHUNK = 64
NEG_INF = -1e30


def _cparams(*sem):
    return pltpu.CompilerParams(dimension_semantics=sem, vmem_limit_bytes=VMEM_LIMIT_BYTES)


def _const_spec(shape):
    nd = len(shape)
    return pl.BlockSpec(shape, lambda *_: (0,) * nd, pipeline_mode=pl.Buffered(1))


def _split_dot(x, w):
    hi = x.astype(BF16)
    lo = (x - hi.astype(F32)).astype(BF16)
    return (jnp.dot(hi, w, preferred_element_type=F32)
            + jnp.dot(lo, w, preferred_element_type=F32))


def _head_sum(x, e_ref, et_ref):
    return _split_dot(_split_dot(x, e_ref[...]), et_ref[...])


def _rms(x, gain):
    return x * lax.rsqrt(jnp.mean(x * x, axis=-1, keepdims=True) + RMS_EPS) * gain


def _seq_len_at(row0, tp, sp, ss):
    return jnp.where(row0 < tp, sp, ss)


def _head_indicator(width):
    e = np.zeros((width, LANES), np.float32)
    e[np.arange(width), np.arange(width) // HEAD] = 1.0
    return jnp.asarray(e, BF16), jnp.asarray(e.T, BF16)


def _rwkv_pre_kernel(x_ref, xp_ref, xn_ref, gain_ref, mu_ref, wrkv_ref, wl1_ref, w2_ref, a2_ref,
                     g2_ref, w0_ref, a0_ref, kk_gain_ref, ka_ref, rk_ref, e_ref, et_ref,
                     r_out, v_out, kk_out, g_out, bonus_out, logw_out, kdir_out, bdir_out,
                     *, tm, regions):
    tp, sp, ss = regions
    d = x_ref.shape[-1]
    row0 = pl.program_id(0) * tm
    seq = _seq_len_at(row0, tp, sp, ss)
    at_start = lax.rem(row0, seq) == 0
    at_end = lax.rem(row0 + tm, seq) == 0

    gain = gain_ref[...]
    h = _rms(x_ref[...], gain)
    hp = _rms(xp_ref[...], gain)[7:8, :] * jnp.where(at_start, 0.0, 1.0)
    hn = _rms(xn_ref[...], gain)[0:1, :] * jnp.where(at_end, 0.0, 1.0)
    rows = lax.broadcasted_iota(jnp.int32, (tm, d), 0)
    h_prev = jnp.where(rows == 0, hp, pltpu.roll(h, 1, 0))
    h_next = jnp.where(rows == tm - 1, hn, pltpu.roll(h, tm - 1, 0))
    xx = 0.5 * (h_prev + h_next) - h

    def mixed(n):
        return (h + xx * mu_ref[n:n + 1, :]).astype(BF16)

    r = jnp.dot(mixed(0), wrkv_ref[0], preferred_element_type=F32)
    k = jnp.dot(mixed(1), wrkv_ref[1], preferred_element_type=F32)
    v = jnp.dot(mixed(2), wrkv_ref[2], preferred_element_type=F32)

    t_w = jnp.tanh(jnp.dot(mixed(3), wl1_ref[0], preferred_element_type=F32))
    lw = w0_ref[...] + jnp.dot(t_w.astype(BF16), w2_ref[...], preferred_element_type=F32)
    logw = -math.exp(-0.5) * jax.nn.sigmoid(lw)
    t_a = jnp.dot(mixed(4), wl1_ref[1], preferred_element_type=F32)
    a = jax.nn.sigmoid(a0_ref[...] + jnp.dot(t_a.astype(BF16), a2_ref[...], preferred_element_type=F32))
    t_g = jax.nn.sigmoid(jnp.dot(mixed(5), wl1_ref[2], preferred_element_type=F32))
    g = jnp.dot(t_g.astype(BF16), g2_ref[...], preferred_element_type=F32)

    kk = k * kk_gain_ref[...]
    kk = kk * lax.rsqrt(jnp.maximum(_head_sum(kk * kk, e_ref, et_ref), 1e-12))
    bonus = _head_sum(r * k * rk_ref[...], e_ref, et_ref) * v

    r_out[...] = r
    v_out[...] = v
    kk_out[...] = kk
    g_out[...] = g
    bonus_out[...] = bonus
    ka = ka_ref[...]
    for z in range(2):
        a_z = a[:, z * d:(z + 1) * d]
        logw_out[z] = logw[:, z * d:(z + 1) * d]
        kdir_out[z] = k * (1.0 + (a_z - 1.0) * ka)
        bdir_out[z] = kk * a_z


def _block_diag2(m):
    z = jnp.zeros_like(m[0])
    return jnp.concatenate([jnp.concatenate([m[0], z], 1), jnp.concatenate([z, m[1]], 1)], 0)


def _rwkv_pre(x, regions, gain, mu, w_rkv, w0, w1, w2, a0, a1, a2, g1, g2, k_k, k_a, r_k, tm=256):
    t, d = x.shape
    e, et = _head_indicator(d)
    wl1 = jnp.stack([jnp.concatenate([w1[0], w1[1]], 1), jnp.concatenate([a1[0], a1[1]], 1), g1]).astype(BF16)
    w2c = _block_diag2(w2).astype(BF16)
    a2c = _block_diag2(a2).astype(BF16)
    row = lambda p: p.reshape(1, -1).astype(F32)
    tok = pl.BlockSpec((tm, d), lambda i: (i, 0))
    halo_blocks = t // 8
    prev = pl.BlockSpec((8, d), lambda i: (jnp.maximum(i * (tm // 8) - 1, 0), 0))
    nxt = pl.BlockSpec((8, d), lambda i: (jnp.minimum((i + 1) * (tm // 8), halo_blocks - 1), 0))
    dir_tok = pl.BlockSpec((2, tm, d), lambda i: (0, i, 0))
    consts = [row(gain), mu.astype(F32), w_rkv.astype(BF16), wl1, w2c, a2c, g2.astype(BF16),
              row(w0), row(a0), row(k_k), row(k_a), row(r_k), e, et]
    tok_shape = jax.ShapeDtypeStruct((t, d), F32)
    dir_shape = jax.ShapeDtypeStruct((2, t, d), F32)
    return pl.pallas_call(
        functools.partial(_rwkv_pre_kernel, tm=tm, regions=regions),
        grid=(t // tm,),
        in_specs=[tok, prev, nxt] + [_const_spec(c.shape) for c in consts],
        out_specs=[tok] * 5 + [dir_tok] * 3,
        out_shape=[tok_shape] * 5 + [dir_shape] * 3,
        compiler_params=_cparams("arbitrary"),
        name="rwkv_pre",
    )(x, x, x, *consts)


_NN = ((1,), (0,))
_NT = ((1,), (1,))
_TN = ((0,), (0,))


def _mm(a, b, dims=_NN):
    return lax.dot_general(a.astype(BF16), b.astype(BF16), (dims, ((), ())), preferred_element_type=F32)


def _cumsum_rows(x, reverse):
    n = x.shape[0]
    row = lax.broadcasted_iota(jnp.int32, x.shape, 0)
    s = 1
    while s < n:
        if reverse:
            x = x + jnp.where(row < n - s, pltpu.roll(x, n - s, 0), 0.0)
        else:
            x = x + jnp.where(row >= s, pltpu.roll(x, s, 0), 0.0)
        s *= 2
    return x


def _wkv_kernel(r_ref, v_ref, kk_ref, logw_ref, kdir_ref, bdir_ref, y_ref, state_ref,
                *, rb, regions, reverse):
    tp, sp, ss = regions
    c = SCAN_CHUNK
    npairs = r_ref.shape[-1] // LANES
    nsteps = pl.num_programs(1)
    step = pl.program_id(1)
    blk = (nsteps - 1 - step) if reverse else step
    row0 = blk * rb
    seq = _seq_len_at(row0, tp, sp, ss)
    first = (lax.rem(row0 + rb, seq) == 0) if reverse else (lax.rem(row0, seq) == 0)

    @pl.when(first)
    def _():
        state_ref[...] = jnp.zeros_like(state_ref)

    ti2 = lax.broadcasted_iota(jnp.int32, (c, 2 * c), 0)
    tj2 = lax.broadcasted_iota(jnp.int32, (c, 2 * c), 1) & (c - 1)
    ti4 = lax.broadcasted_iota(jnp.int32, (c, 4 * c), 0)
    tj4 = lax.broadcasted_iota(jnp.int32, (c, 4 * c), 1) & (c - 1)
    incl4 = (tj4 >= ti4) if reverse else (tj4 <= ti4)
    strict2 = (tj2 > ti2) if reverse else (tj2 < ti2)
    lane = lax.broadcasted_iota(jnp.int32, (c, LANES), 1)
    head0 = lane < HEAD
    rr = lax.broadcasted_iota(jnp.int32, (LANES, LANES), 0)
    cc = lax.broadcasted_iota(jnp.int32, (LANES, LANES), 1)
    same_head = (rr < HEAD) == (cc < HEAD)
    eye = (rr == cc).astype(F32)

    def stack_heads(m):
        return jnp.concatenate([jnp.where(head0, m, 0.0), jnp.where(head0, 0.0, m)], 0)

    order = list(range(rb // c - 1, -1, -1) if reverse else range(rb // c))
    units = [(pl.ds(ci * c, c), pl.ds(p * LANES, LANES)) for ci in order for p in range(npairs)]
    each = lambda fn, *lists: [fn(*xs) for xs in zip(*lists)]

    lw = [logw_ref[u] for u in units]
    cum = each(lambda x: _cumsum_rows(x, reverse), lw)
    total = each(lambda x: jnp.sum(x, axis=0, keepdims=True), lw)
    lhs = [jnp.concatenate([r_ref[u] * jnp.exp(cm), kk_ref[u] * jnp.exp(cm - x)], 0).astype(BF16)
           for u, cm, x in zip(units, cum, lw)]
    rhs = []
    kb_tail = []
    for u, cm, tt in zip(units, cum, total):
        kd, bd = kdir_ref[u], bdir_ref[u]
        inv = jnp.exp(-cm)
        tail = jnp.exp(tt - cm)
        rhs.append(jnp.concatenate([stack_heads(kd * inv), stack_heads(bd * inv)], 0).astype(BF16))
        kb_tail.append(jnp.concatenate([kd * tail, -(bd * tail)], 0).astype(BF16))
    q = each(lambda a, b: _mm(a, b, _NT), lhs, rhs)
    a_out = [jnp.where(incl4, x[0:c], 0.0).astype(BF16) for x in q]
    a_kk = [jnp.where(strict2, x[c:2 * c, 0:2 * c], 0.0).astype(BF16) for x in q]
    pw = [-stack_heads(jnp.where(strict2, x[c:2 * c, 2 * c:4 * c], 0.0)) for x in q]
    inv_l = [eye + x for x in pw]
    for _ in range(int(math.log2(c)) - 1):
        pw = each(lambda x: _mm(x, x), pw)
        inv_l = each(lambda t, x: t + _mm(t, x), inv_l, pw)
    inv_l = [x.astype(BF16) for x in inv_l]
    v_all = [v_ref[u] for u in units]
    v_bd = [stack_heads(x).astype(BF16) for x in v_all]
    akv = each(_mm, a_kk, v_bd)
    decay = [jnp.exp(x) for x in total]

    states = [state_ref[p] for p in range(npairs)]
    for k in range(len(order)):
        idx = list(range(k * npairs, (k + 1) * npairs))
        sk = [_mm(lhs[i], states[p], _NT) for p, i in enumerate(idx)]
        u_bd = [_mm(inv_l[i], stack_heads(s[c:2 * c] + akv[i])) for s, i in zip(sk, idx)]
        y = [s[0:c] + _mm(a_out[i], jnp.concatenate([v_bd[i], (-ub).astype(BF16)], 0))
             for s, ub, i in zip(sk, u_bd, idx)]
        for yy, i in zip(y, idx):
            y_ref[units[i]] = yy
        upd = [_mm(jnp.concatenate([v_all[i], ub[0:c] + ub[c:2 * c]], 0), kb_tail[i], _TN)
               for ub, i in zip(u_bd, idx)]
        states = [states[p] * decay[i] + jnp.where(same_head, upd[p], 0.0) for p, i in enumerate(idx)]
    for p in range(npairs):
        state_ref[p] = states[p]


def _wkv(r, v, kk, logw, kdir, bdir, regions, reverse, rb=256, width=512):
    t, d = r.shape
    z = 1 if reverse else 0
    nsteps = t // rb

    def row_block(s):
        return (nsteps - 1 - s) if reverse else s

    tok = pl.BlockSpec((rb, width), lambda j, s: (row_block(s), j))
    dir_tok = pl.BlockSpec((None, rb, width), lambda j, s: (z, row_block(s), j))
    return pl.pallas_call(
        functools.partial(_wkv_kernel, rb=rb, regions=regions, reverse=reverse),
        grid=(d // width, nsteps),
        in_specs=[tok, tok, tok, dir_tok, dir_tok, dir_tok],
        out_specs=tok,
        out_shape=jax.ShapeDtypeStruct((t, d), F32),
        scratch_shapes=[pltpu.VMEM((width // LANES, LANES, LANES), F32)],
        compiler_params=_cparams("arbitrary", "arbitrary"),
        name="wkv_bwd" if reverse else "wkv_fwd",
    )(r, v, kk, logw, kdir, bdir)


def _rwkv_post_kernel(yf_ref, yb_ref, bonus_ref, g_ref, x_ref, lng_ref, lnb_ref, wo_ref, e_ref, et_ref, out_ref):
    y = yf_ref[...] + yb_ref[...]
    mean = _head_sum(y, e_ref, et_ref) * (1.0 / HEAD)
    dlt = y - mean
    var = _head_sum(dlt * dlt, e_ref, et_ref) * (1.0 / HEAD)
    yn = dlt * lax.rsqrt(var + GN_EPS) * lng_ref[...] + lnb_ref[...]
    mixed = ((yn + bonus_ref[...]) * g_ref[...]).astype(BF16)
    out_ref[...] = x_ref[...] + jnp.dot(mixed, wo_ref[...], preferred_element_type=F32)


def _rwkv_post(yf, yb, bonus, g, x, ln_g, ln_b, w_o, tm=256):
    t, d = x.shape
    e, et = _head_indicator(d)
    tok = pl.BlockSpec((tm, d), lambda i: (i, 0))
    consts = [ln_g.reshape(1, d), ln_b.reshape(1, d), w_o.astype(BF16), e, et]
    return pl.pallas_call(
        _rwkv_post_kernel,
        grid=(t // tm,),
        in_specs=[tok] * 5 + [_const_spec(c.shape) for c in consts],
        out_specs=tok,
        out_shape=jax.ShapeDtypeStruct((t, d), F32),
        compiler_params=_cparams("arbitrary"),
        name="rwkv_post",
    )(yf, yb, bonus, g, x, *consts)


def _ffn_kernel(x_ref, gain_ref, wg_ref, wu_ref, wd_ref, out_ref, h_ref, acc_ref):
    f = pl.program_id(1)

    @pl.when(f == 0)
    def _():
        h_ref[...] = _rms(x_ref[...], gain_ref[...]).astype(BF16)
        acc_ref[...] = jnp.zeros_like(acc_ref)

    h = h_ref[...]
    gate = jnp.dot(h, wg_ref[...], preferred_element_type=F32)
    up = jnp.dot(h, wu_ref[...], preferred_element_type=F32)
    hidden = (jax.nn.silu(gate) * up).astype(BF16)
    acc_ref[...] += jnp.dot(hidden, wd_ref[...], preferred_element_type=F32)

    @pl.when(f == pl.num_programs(1) - 1)
    def _():
        out_ref[...] = x_ref[...] + acc_ref[...]


def _ffn(x, gain, w_gate, w_up, w_down, tm=512, tf=2816):
    t, d = x.shape
    dff = w_gate.shape[1]
    return pl.pallas_call(
        _ffn_kernel,
        grid=(t // tm, dff // tf),
        in_specs=[pl.BlockSpec((tm, d), lambda i, f: (i, 0)),
                  _const_spec((1, d)),
                  pl.BlockSpec((d, tf), lambda i, f: (0, f)),
                  pl.BlockSpec((d, tf), lambda i, f: (0, f)),
                  pl.BlockSpec((tf, d), lambda i, f: (f, 0))],
        out_specs=pl.BlockSpec((tm, d), lambda i, f: (i, 0)),
        out_shape=jax.ShapeDtypeStruct((t, d), F32),
        scratch_shapes=[pltpu.VMEM((tm, d), BF16), pltpu.VMEM((tm, d), F32)],
        compiler_params=_cparams("arbitrary", "arbitrary"),
        name="ffn",
    )(x, gain.reshape(1, d), w_gate.astype(BF16), w_up.astype(BF16), w_down.astype(BF16))


def _router_kernel(x_ref, gain_ref, wr_ref, tri_ref, h_out, comb_out, rank_out, count_out, carry_ref,
                   *, n_experts):
    @pl.when(pl.program_id(0) == 0)
    def _():
        carry_ref[...] = jnp.zeros_like(carry_ref)

    h = _rms(x_ref[...], gain_ref[...])
    h_out[...] = h
    logits = jnp.dot(h.astype(BF16), wr_ref[...], preferred_element_type=F32)
    lane = lax.broadcasted_iota(jnp.int32, logits.shape, 1)
    logits = jnp.where(lane < n_experts, logits, -jnp.inf)
    m1 = jnp.max(logits, axis=-1, keepdims=True)
    i1 = jnp.min(jnp.where(logits == m1, lane, LANES), axis=-1, keepdims=True)
    rest = jnp.where(lane == i1, -jnp.inf, logits)
    m2 = jnp.max(rest, axis=-1, keepdims=True)
    i2 = jnp.min(jnp.where(rest == m2, lane, LANES), axis=-1, keepdims=True)
    e2 = jnp.exp(m2 - m1)
    g1 = 1.0 / (1.0 + e2)
    g2 = e2 / (1.0 + e2)
    comb_out[...] = jnp.where(lane == i1, g1, 0.0) + jnp.where(lane == i2, g2, 0.0)
    sel = jnp.where((lane == i1) | (lane == i2), 1.0, 0.0)
    earlier = carry_ref[0:1, :] + jnp.dot(tri_ref[...], sel.astype(BF16), preferred_element_type=F32)
    rank_out[...] = sel * (earlier + 1.0)
    carry_ref[...] = carry_ref[...] + jnp.sum(sel, axis=0, keepdims=True)
    count_out[...] = carry_ref[...]


def _router(x, gain, w_router, tm=512):
    t, d = x.shape
    n_experts = w_router.shape[1]
    wr = jnp.zeros((d, LANES), F32).at[:, :n_experts].set(w_router).astype(BF16)
    tri = jnp.asarray(np.tril(np.ones((tm, tm), np.float32), -1), BF16)
    tok = pl.BlockSpec((tm, d), lambda i: (i, 0))
    lanes = pl.BlockSpec((tm, LANES), lambda i: (i, 0))
    return pl.pallas_call(
        functools.partial(_router_kernel, n_experts=n_experts),
        grid=(t // tm,),
        in_specs=[tok, _const_spec((1, d)), _const_spec((d, LANES)), _const_spec((tm, tm))],
        out_specs=[tok, lanes, lanes, pl.BlockSpec((8, LANES), lambda i: (0, 0))],
        out_shape=[jax.ShapeDtypeStruct((t, d), F32), jax.ShapeDtypeStruct((t, LANES), F32),
                   jax.ShapeDtypeStruct((t, LANES), F32), jax.ShapeDtypeStruct((8, LANES), F32)],
        scratch_shapes=[pltpu.VMEM((8, LANES), F32)],
        compiler_params=_cparams("arbitrary"),
        name="moe_router",
    )(x, gain.reshape(1, d), wr, tri)


def _route_plan(comb, rank, counts, n_experts, tile):
    t = comb.shape[0]
    n_tiles = (2 * t) // tile + n_experts
    cnt = counts[0, :n_experts].astype(jnp.int32)
    padded = ((cnt + tile - 1) // tile) * tile
    ends = jnp.cumsum(padded)
    starts = jnp.zeros((LANES,), jnp.int32).at[:n_experts].set(ends - padded)
    slot = jnp.where(rank > 0, starts[None, :] + rank.astype(jnp.int32) - 1, -1)
    slot_hi = jnp.max(slot, axis=1)
    slot_lo = jnp.min(jnp.where(slot >= 0, slot, jnp.iinfo(jnp.int32).max), axis=1)
    gate_lo = jnp.sum(jnp.where(slot == slot_lo[:, None], comb, 0.0), axis=1)
    gate_hi = jnp.sum(jnp.where(slot == slot_hi[:, None], comb, 0.0), axis=1)
    slots = jnp.stack([slot_lo, slot_hi], 1)
    gates = jnp.stack([gate_lo, gate_hi], 1)
    tile_expert = jnp.sum(jnp.arange(n_tiles, dtype=jnp.int32)[:, None] * tile >= ends[None, :], axis=1)
    return slots, gates, jnp.minimum(tile_expert, n_experts - 1).astype(jnp.int32), n_tiles


def _row_copy(src, src_row, dst, dst_row, sem):
    return pltpu.make_async_copy(src.at[pl.ds(src_row, 1)], dst.at[pl.ds(dst_row, 1)], sem)


def _dispatch_kernel(slots_ref, h_ref, init_hbm, xs_hbm, sem, *, tm):
    del init_hbm

    def issue(i, carry):
        for k in range(2):
            _row_copy(h_ref, i, xs_hbm, slots_ref[0, 2 * i + k], sem).start()
        return carry

    lax.fori_loop(0, tm, issue, 0)
    for _ in range(2):
        pltpu.make_async_copy(h_ref, xs_hbm.at[pl.ds(0, tm)], sem).wait()


def _dispatch(h, slots, n_rows, tm=512):
    t, d = h.shape
    return pl.pallas_call(
        functools.partial(_dispatch_kernel, tm=tm),
        grid=(t // tm,),
        in_specs=[pl.BlockSpec((None, 1, 2 * tm), lambda i: (i, 0, 0), memory_space=pltpu.SMEM),
                  pl.BlockSpec((tm, d), lambda i: (i, 0)), pl.BlockSpec(memory_space=pl.ANY)],
        out_specs=pl.BlockSpec(memory_space=pl.ANY),
        out_shape=jax.ShapeDtypeStruct((n_rows, d), h.dtype),
        scratch_shapes=[pltpu.SemaphoreType.DMA(())],
        input_output_aliases={2: 0},
        compiler_params=pltpu.CompilerParams(dimension_semantics=("arbitrary",), has_side_effects=True),
        name="moe_dispatch",
    )(slots.reshape(t // tm, 1, 2 * tm), h, jnp.zeros((n_rows, d), h.dtype))


def _experts_kernel(te_ref, xs_ref, wg_ref, wu_ref, wd_ref, ys_ref, xb_ref, acc_ref):
    del te_ref
    f = pl.program_id(1)

    @pl.when(f == 0)
    def _():
        xb_ref[...] = xs_ref[...].astype(BF16)
        acc_ref[...] = jnp.zeros_like(acc_ref)

    xb = xb_ref[...]
    gate = jnp.dot(xb, wg_ref[...], preferred_element_type=F32)
    up = jnp.dot(xb, wu_ref[...], preferred_element_type=F32)
    hidden = (jax.nn.silu(gate) * up).astype(BF16)
    acc_ref[...] += jnp.dot(hidden, wd_ref[...], preferred_element_type=F32)

    @pl.when(f == pl.num_programs(1) - 1)
    def _():
        ys_ref[...] = acc_ref[...]


def _experts(xs, tile_expert, w_gate, w_up, w_down, tile, tf=1792):
    n_rows, d = xs.shape
    dff = w_gate.shape[2]
    grid_spec = pltpu.PrefetchScalarGridSpec(
        num_scalar_prefetch=1,
        grid=(n_rows // tile, dff // tf),
        in_specs=[pl.BlockSpec((tile, d), lambda i, f, te: (i, 0)),
                  pl.BlockSpec((None, d, tf), lambda i, f, te: (te[i], 0, f)),
                  pl.BlockSpec((None, d, tf), lambda i, f, te: (te[i], 0, f)),
                  pl.BlockSpec((None, tf, d), lambda i, f, te: (te[i], f, 0))],
        out_specs=pl.BlockSpec((tile, d), lambda i, f, te: (i, 0)),
        scratch_shapes=[pltpu.VMEM((tile, d), BF16), pltpu.VMEM((tile, d), F32)])
    return pl.pallas_call(
        _experts_kernel,
        grid_spec=grid_spec,
        out_shape=jax.ShapeDtypeStruct((n_rows, d), F32),
        compiler_params=_cparams("arbitrary", "arbitrary"),
        name="moe_experts",
    )(tile_expert, xs, w_gate.astype(BF16), w_up.astype(BF16), w_down.astype(BF16))


def _combine_kernel(slots_ref, x_ref, gates_ref, ys_hbm, out_ref, buf_ref, sem, *, tm):
    def issue(i, carry):
        for k in range(2):
            _row_copy(ys_hbm, slots_ref[0, 2 * i + k], buf_ref.at[k], i, sem).start()
        return carry

    lax.fori_loop(0, tm, issue, 0)
    for k in range(2):
        pltpu.make_async_copy(ys_hbm.at[pl.ds(0, tm)], buf_ref.at[k], sem).wait()
    gates = gates_ref[...]
    out_ref[...] = x_ref[...] + gates[:, 0:1] * buf_ref[0] + gates[:, 1:2] * buf_ref[1]


def _combine(x, ys, slots, gates, tm=256):
    t, d = x.shape
    tok = pl.BlockSpec((tm, d), lambda i: (i, 0))
    return pl.pallas_call(
        functools.partial(_combine_kernel, tm=tm),
        grid=(t // tm,),
        in_specs=[pl.BlockSpec((None, 1, 2 * tm), lambda i: (i, 0, 0), memory_space=pltpu.SMEM),
                  tok, pl.BlockSpec((tm, 2), lambda i: (i, 0)), pl.BlockSpec(memory_space=pl.ANY)],
        out_specs=tok,
        out_shape=jax.ShapeDtypeStruct((t, d), F32),
        scratch_shapes=[pltpu.VMEM((2, tm, d), F32), pltpu.SemaphoreType.DMA(())],
        compiler_params=_cparams("arbitrary"),
        name="moe_combine",
    )(slots.reshape(t // tm, 1, 2 * tm), x, gates, ys)


def _moe(x, gain, w_router, w_gate, w_up, w_down, tile=512):
    n_experts = w_gate.shape[0]
    h, comb, rank, counts = _router(x, gain, w_router)
    slots, gates, tile_expert, n_tiles = _route_plan(comb, rank, counts, n_experts, tile)
    xs = _dispatch(h, slots, n_tiles * tile)
    ys = _experts(xs, tile_expert, w_gate, w_up, w_down, tile)
    return _combine(x, ys, slots, gates)


def _lane_chunks(width):
    return [slice(c * LANES, (c + 1) * LANES) for c in range(width // LANES)]


def _store_dilated(dst_ref, stage_ref, value, dil):
    rows, width = value.shape
    if dil == 1:
        dst_ref[...] = value.astype(dst_ref.dtype)
        return
    for c, ls in enumerate(_lane_chunks(width)):
        stage_ref[c] = value[:, ls]
    for rho in range(dil):
        for c in range(width // LANES):
            dst_ref[:, pl.ds(rho * width + c * LANES, LANES)] = (
                stage_ref.at[c][pl.ds(rho, rows // dil, stride=dil), :].astype(dst_ref.dtype))


def _load_dilated(src_ref, stage_ref, base, rows, width, dil):
    if dil == 1:
        return src_ref[...].astype(F32)
    for rho in range(dil):
        for c in range(width // LANES):
            stage_ref.at[base + c][pl.ds(rho, rows // dil, stride=dil), :] = (
                src_ref[:, pl.ds(rho * width + c * LANES, LANES)].astype(F32))
    return jnp.concatenate([stage_ref[base + c] for c in range(width // LANES)], 1)


def _attn_in_kernel(x_ref, gain_ref, w_ref, hg_ref, e_ref, et_ref, *rest, dils):
    outs, (h_ref, stage_ref) = rest[:len(dils)], rest[len(dils):]
    g = pl.program_id(1)

    @pl.when(g == 0)
    def _():
        h_ref[...] = _rms(x_ref[...], gain_ref[...]).astype(BF16)

    n_qk = 2 * GROUP_WIDTH
    half = h_ref.shape[0] // 2
    ys = [jnp.dot(h_ref[pl.ds(r * half, half), :], w_ref[...], preferred_element_type=F32) for r in range(2)]
    sq = [_split_dot(yy[:, :n_qk] * yy[:, :n_qk], e_ref[...]) for yy in ys]
    ms = [_split_dot(x, et_ref[...]) * (1.0 / HEAD) for x in sq]
    qk = [yy[:, :n_qk] * lax.rsqrt(m + RMS_EPS) * hg_ref[...] for yy, m in zip(ys, ms)]
    y = jnp.concatenate([jnp.concatenate([a, yy[:, n_qk:]], 1) for a, yy in zip(qk, ys)], 0)
    for gi, dil in enumerate(dils):
        @pl.when(g == gi)
        def _(gi=gi, dil=dil):
            _store_dilated(outs[gi], stage_ref, y, dil)


def _attn_in(x, gain, w_in, q_gain, k_gain, tm=512):
    t, d = x.shape
    n_groups = q_gain.shape[0]
    gw = 3 * GROUP_WIDTH
    dils = tuple(dil for _, dil in ATTN_GROUPS)
    w = jnp.transpose(w_in.reshape(d, 3, n_groups, GROUP_WIDTH), (2, 0, 1, 3)).reshape(n_groups, d, gw).astype(BF16)
    head_gain = jnp.concatenate([jnp.tile(q_gain, (1, GROUP_HEADS)), jnp.tile(k_gain, (1, GROUP_HEADS))],
                                1).reshape(n_groups, 1, 2 * GROUP_WIDTH)
    e, et = _head_indicator(2 * GROUP_WIDTH)
    return pl.pallas_call(
        functools.partial(_attn_in_kernel, dils=dils),
        grid=(t // tm, n_groups),
        in_specs=[pl.BlockSpec((tm, d), lambda i, g: (i, 0)),
                  _const_spec((1, d)),
                  pl.BlockSpec((None, d, gw), lambda i, g: (g, 0, 0)),
                  pl.BlockSpec((None, 1, 2 * GROUP_WIDTH), lambda i, g: (g, 0, 0)),
                  _const_spec(e.shape), _const_spec(et.shape)],
        out_specs=[pl.BlockSpec((tm // dil, dil * gw), lambda i, g: (i, 0)) for dil in dils],
        out_shape=[jax.ShapeDtypeStruct((t // dil, dil * gw), BF16) for dil in dils],
        scratch_shapes=[pltpu.VMEM((tm, d), BF16), pltpu.VMEM((gw // LANES, tm, LANES), F32)],
        compiler_params=_cparams("arbitrary", "arbitrary"),
        name="attn_in",
    )(x, gain.reshape(1, d), w, head_gain, e, et)


def _t5_bucket_np(rel):
    nb = REL_BUCKETS // 2
    max_exact = nb // 2
    n = np.abs(rel)
    large = max_exact + (np.log(np.maximum(n, 1).astype(np.float32) / max_exact)
                         / math.log(REL_MAX_DIST / max_exact) * (nb - max_exact)).astype(np.int32)
    large = np.minimum(large, nb - 1)
    return np.where(rel > 0, nb, 0) + np.where(n < max_exact, n, large)


def _bias_table(rel_bias, group, window, dil):
    side = window // (2 * dil)
    assert side == KV_HALO
    buckets = _t5_bucket_np(dil * np.arange(-side, side + 1))
    table = rel_bias.reshape(REL_BUCKETS, -1, GROUP_HEADS)[buckets, group].T
    full = jnp.full((8, Q_BLOCK + 2 * KV_HALO), NEG_INF, F32)
    return full.at[:GROUP_HEADS, :2 * side + 1].set(table)


def _attn_kernel(q_ref, kp_ref, kc_ref, kn_ref, vp_ref, vc_ref, vn_ref, table_ref, o_ref, lse_ref, bias_ref,
                 *, regions, dil):
    tp, sp, ss = regions
    nk = Q_BLOCK + 2 * KV_HALO

    @pl.when((pl.program_id(0) == 0) & (pl.program_id(1) == 0))
    def _():
        for h in range(GROUP_HEADS):
            row = jnp.broadcast_to(table_ref[h:h + 1, :], (Q_BLOCK, nk))
            bias_ref[h] = pltpu.roll(row, 0, 1, stride=1, stride_axis=0)

    row0 = pl.program_id(1) * Q_BLOCK
    seq = _seq_len_at(row0, tp // dil, sp // dil, ss // dil)
    seq_start = row0 - lax.rem(row0, seq)
    key_row = row0 - KV_HALO + lax.broadcasted_iota(jnp.int32, (1, nk), 1)
    valid = (key_row >= seq_start) & (key_row < seq_start + seq)
    lane = lax.broadcasted_iota(jnp.int32, (Q_BLOCK, LANES), 1)
    scale = HEAD ** -0.5

    q = q_ref[...]
    k = jnp.concatenate([kp_ref[...], kc_ref[...], kn_ref[...]], 0)
    v = jnp.concatenate([vp_ref[...], vc_ref[...], vn_ref[...]], 0)
    heads = [(pair, sub) for pair in range(GROUP_HEADS // 2) for sub in range(2)]
    lanes_of = lambda pair: slice(pair * LANES, (pair + 1) * LANES)
    mine = [(lane < HEAD), (lane >= HEAD)]
    s_all = []
    for pair, sub in heads:
        qp = q[:, lanes_of(pair)]
        qm = jnp.where(mine[sub], qp, jnp.zeros_like(qp))
        s = lax.dot_general(qm, k[:, lanes_of(pair)], (_NT, ((), ())), preferred_element_type=F32)
        s_all.append(jnp.where(valid, s * scale + bias_ref[2 * pair + sub], NEG_INF))
    m_all = [jnp.max(s, axis=-1, keepdims=True) for s in s_all]
    p_all = [jnp.exp(s - m) for s, m in zip(s_all, m_all)]
    den_all = [jnp.sum(p, axis=-1, keepdims=True) for p in p_all]
    o_all = [jnp.dot((p / den).astype(v.dtype), v[:, lanes_of(pair)], preferred_element_type=F32)
             for p, den, (pair, _) in zip(p_all, den_all, heads)]
    lse_all = [m + jnp.log(den) for m, den in zip(m_all, den_all)]
    for pair in range(GROUP_HEADS // 2):
        a, b = 2 * pair, 2 * pair + 1
        o_ref[:, lanes_of(pair)] = jnp.where(mine[0], o_all[a], o_all[b]).astype(o_ref.dtype)
        lse_ref[:, lanes_of(pair)] = jnp.where(mine[0], lse_all[a], lse_all[b])


def _attn_group(qkv_d, rel_bias, group, regions):
    window, dil = ATTN_GROUPS[group]
    rows = qkv_d.shape[0]
    parts = 3
    table = _bias_table(rel_bias, group, window, dil)
    halo_per_q = Q_BLOCK // KV_HALO
    last_halo = rows // KV_HALO - 1

    def cur(part):
        return pl.BlockSpec((Q_BLOCK, GROUP_WIDTH), lambda rho, m: (m, rho * parts + part))

    def prev(part):
        return pl.BlockSpec((KV_HALO, GROUP_WIDTH),
                            lambda rho, m: (jnp.maximum(m * halo_per_q - 1, 0), rho * parts + part))

    def nxt(part):
        return pl.BlockSpec((KV_HALO, GROUP_WIDTH),
                            lambda rho, m: (jnp.minimum((m + 1) * halo_per_q, last_halo), rho * parts + part))

    out_spec = pl.BlockSpec((Q_BLOCK, GROUP_WIDTH), lambda rho, m: (m, rho))
    return pl.pallas_call(
        functools.partial(_attn_kernel, regions=regions, dil=dil),
        grid=(dil, rows // Q_BLOCK),
        in_specs=[cur(0), prev(1), cur(1), nxt(1), prev(2), cur(2), nxt(2), _const_spec(table.shape)],
        out_specs=[out_spec, out_spec],
        out_shape=[jax.ShapeDtypeStruct((rows, dil * GROUP_WIDTH), BF16),
                   jax.ShapeDtypeStruct((rows, dil * GROUP_WIDTH), F32)],
        scratch_shapes=[pltpu.VMEM((GROUP_HEADS, Q_BLOCK, Q_BLOCK + 2 * KV_HALO), F32)],
        compiler_params=_cparams("arbitrary", "arbitrary"),
        name=f"attn_g{group}",
    )(qkv_d, qkv_d, qkv_d, qkv_d, qkv_d, qkv_d, qkv_d, table)


def _attn_out_kernel(o0, o1, o2, l0, l1, l2, x_ref, wo_ref, out_ref, stage_ref, *, dils):
    tm = x_ref.shape[0]
    chunks = GROUP_WIDTH // LANES
    os_ = [_load_dilated(r, stage_ref, 2 * g * chunks, tm, GROUP_WIDTH, dils[g]) for g, r in enumerate((o0, o1, o2))]
    lses = [_load_dilated(r, stage_ref, (2 * g + 1) * chunks, tm, GROUP_WIDTH, dils[g])
            for g, r in enumerate((l0, l1, l2))]
    mx = jnp.maximum(jnp.maximum(lses[0], lses[1]), lses[2])
    ws = [jnp.exp(l - mx) for l in lses]
    tot = ws[0] + ws[1] + ws[2]
    acc = x_ref[...]
    for g in range(len(dils)):
        scaled = (os_[g] * (ws[g] / tot)).astype(BF16)
        acc = acc + jnp.dot(scaled, wo_ref[g], preferred_element_type=F32)
    out_ref[...] = acc


def _attn_out(os_, lses, x, w_o, tm=512):
    t, d = x.shape
    n_groups = len(os_)
    dils = tuple(dil for _, dil in ATTN_GROUPS)
    grp = [pl.BlockSpec((tm // dil, dil * GROUP_WIDTH), lambda i: (i, 0)) for dil in dils]
    tok = pl.BlockSpec((tm, d), lambda i: (i, 0))
    wo = w_o.reshape(n_groups, GROUP_WIDTH, d).astype(BF16)
    return pl.pallas_call(
        functools.partial(_attn_out_kernel, dils=dils),
        grid=(t // tm,),
        in_specs=grp + grp + [tok, _const_spec(wo.shape)],
        out_specs=tok,
        out_shape=jax.ShapeDtypeStruct((t, d), F32),
        scratch_shapes=[pltpu.VMEM((2 * n_groups * (GROUP_WIDTH // LANES), tm, LANES), F32)],
        compiler_params=_cparams("arbitrary"),
        name="attn_out",
    )(*os_, *lses, x, wo)


def kernel(x_prompt, x_sample, norm_mix, norm_ffn, rwkv_mu, rwkv_w_rkv, rwkv_w0, rwkv_w1, rwkv_w2, rwkv_a0, rwkv_a1, rwkv_a2, rwkv_g1, rwkv_g2, rwkv_k_k, rwkv_k_a, rwkv_r_k, rwkv_ln_g, rwkv_ln_b, rwkv_w_o, attn_w_in, attn_q_gain, attn_k_gain, attn_w_o, rel_bias, ffn_w_gate, ffn_w_up, ffn_w_down, moe_router, moe_w_gate, moe_w_up, moe_w_down):
    d = x_prompt.shape[-1]
    tp = x_prompt.shape[0] * x_prompt.shape[1]
    regions = (tp, x_prompt.shape[1], x_sample.shape[1])
    x = jnp.concatenate([x_prompt.reshape(-1, d), x_sample.reshape(-1, d)], 0)
    depth = norm_mix.shape[0]
    for i in range(depth):
        j = i // 2
        if i % 2 == 0:
            r, v, kk, g, bonus, logw, kdir, bdir = _rwkv_pre(
                x, regions, norm_mix[i], rwkv_mu[j], rwkv_w_rkv[j], rwkv_w0[j], rwkv_w1[j], rwkv_w2[j],
                rwkv_a0[j], rwkv_a1[j], rwkv_a2[j], rwkv_g1[j], rwkv_g2[j], rwkv_k_k[j], rwkv_k_a[j], rwkv_r_k[j])
            yf = _wkv(r, v, kk, logw, kdir, bdir, regions, reverse=False)
            yb = _wkv(r, v, kk, logw, kdir, bdir, regions, reverse=True)
            x = _rwkv_post(yf, yb, bonus, g, x, rwkv_ln_g[j], rwkv_ln_b[j], rwkv_w_o[j])
            x = _ffn(x, norm_ffn[i], ffn_w_gate[j], ffn_w_up[j], ffn_w_down[j])
        else:
            qkv = _attn_in(x, norm_mix[i], attn_w_in[j], attn_q_gain[j], attn_k_gain[j])
            outs = [_attn_group(qkv[g], rel_bias, g, regions) for g in range(len(ATTN_GROUPS))]
            x = _attn_out([o for o, _ in outs], [l for _, l in outs], x, attn_w_o[j])
            x = _moe(x, norm_ffn[i], moe_router[j], moe_w_gate[j], moe_w_up[j], moe_w_down[j])
    y_prompt = x[:tp].reshape(x_prompt.shape)
    y_sample = x[tp:].reshape(x_sample.shape)
    return (y_prompt, y_sample)
```

```python
import functools
import math

import numpy as np
import jax
import jax.numpy as jnp
from jax import lax
from jax.experimental import pallas as pl
from jax.experimental.pallas import tpu as pltpu

F32 = jnp.float32
BF16 = jnp.bfloat16

LANES = 128
VMEM_LIMIT_BYTES = 56 * 1024 * 1024

HEAD = 64
RMS_EPS = 1e-6
GN_EPS = 64e-5
ATTN_GROUPS = ((128, 1), (512, 4), (2048, 16))
GROUP_HEADS = 6
GROUP_WIDTH = GROUP_HEADS * HEAD
REL_BUCKETS = 32
REL_MAX_DIST = 1024
Q_BLOCK = 128
KV_HALO = 64
SCAN_CHUNK = 64
NEG_INF = -1e30


def _cparams(*sem):
    return pltpu.CompilerParams(dimension_semantics=sem, vmem_limit_bytes=VMEM_LIMIT_BYTES)


def _const_spec(shape):
    nd = len(shape)
    return pl.BlockSpec(shape, lambda *_: (0,) * nd, pipeline_mode=pl.Buffered(1))


def _split_dot(x, w):
    hi = x.astype(BF16)
    lo = (x - hi.astype(F32)).astype(BF16)
    return (jnp.dot(hi, w, preferred_element_type=F32)
            + jnp.dot(lo, w, preferred_element_type=F32))


def _head_sum(x, e_ref, et_ref):
    return _split_dot(_split_dot(x, e_ref[...]), et_ref[...])


def _rms(x, gain):
    return x * lax.rsqrt(jnp.mean(x * x, axis=-1, keepdims=True) + RMS_EPS) * gain


def _seq_len_at(row0, tp, sp, ss):
    return jnp.where(row0 < tp, sp, ss)


def _head_indicator(width):
    e = np.zeros((width, LANES), np.float32)
    e[np.arange(width), np.arange(width) // HEAD] = 1.0
    return jnp.asarray(e, BF16), jnp.asarray(e.T, BF16)


def _rwkv_pre_kernel(x_ref, xp_ref, xn_ref, gain_ref, mu_ref, wrkv_ref, wl1_ref, w2_ref, a2_ref,
                     g2_ref, w0_ref, a0_ref, kk_gain_ref, ka_ref, rk_ref, e_ref, et_ref,
                     r_out, v_out, kk_out, g_out, bonus_out, logw_out, kdir_out, bdir_out,
                     *, tm, regions):
    tp, sp, ss = regions
    d = x_ref.shape[-1]
    row0 = pl.program_id(0) * tm
    seq = _seq_len_at(row0, tp, sp, ss)
    at_start = lax.rem(row0, seq) == 0
    at_end = lax.rem(row0 + tm, seq) == 0

    gain = gain_ref[...]
    h = _rms(x_ref[...], gain)
    hp = _rms(xp_ref[...], gain)[7:8, :] * jnp.where(at_start, 0.0, 1.0)
    hn = _rms(xn_ref[...], gain)[0:1, :] * jnp.where(at_end, 0.0, 1.0)
    rows = lax.broadcasted_iota(jnp.int32, (tm, d), 0)
    h_prev = jnp.where(rows == 0, hp, pltpu.roll(h, 1, 0))
    h_next = jnp.where(rows == tm - 1, hn, pltpu.roll(h, tm - 1, 0))
    xx = 0.5 * (h_prev + h_next) - h

    def mixed(n):
        return (h + xx * mu_ref[n:n + 1, :]).astype(BF16)

    r = jnp.dot(mixed(0), wrkv_ref[0], preferred_element_type=F32)
    k = jnp.dot(mixed(1), wrkv_ref[1], preferred_element_type=F32)
    v = jnp.dot(mixed(2), wrkv_ref[2], preferred_element_type=F32)

    t_w = jnp.tanh(jnp.dot(mixed(3), wl1_ref[0], preferred_element_type=F32))
    lw = w0_ref[...] + jnp.dot(t_w.astype(BF16), w2_ref[...], preferred_element_type=F32)
    logw = -math.exp(-0.5) * jax.nn.sigmoid(lw)
    t_a = jnp.dot(mixed(4), wl1_ref[1], preferred_element_type=F32)
    a = jax.nn.sigmoid(a0_ref[...] + jnp.dot(t_a.astype(BF16), a2_ref[...], preferred_element_type=F32))
    t_g = jax.nn.sigmoid(jnp.dot(mixed(5), wl1_ref[2], preferred_element_type=F32))
    g = jnp.dot(t_g.astype(BF16), g2_ref[...], preferred_element_type=F32)

    kk = k * kk_gain_ref[...]
    kk = kk * lax.rsqrt(jnp.maximum(_head_sum(kk * kk, e_ref, et_ref), 1e-12))
    bonus = _head_sum(r * k * rk_ref[...], e_ref, et_ref) * v

    r_out[...] = r
    v_out[...] = v
    kk_out[...] = kk
    g_out[...] = g
    bonus_out[...] = bonus
    ka = ka_ref[...]
    for z in range(2):
        a_z = a[:, z * d:(z + 1) * d]
        logw_out[z] = logw[:, z * d:(z + 1) * d]
        kdir_out[z] = k * (1.0 + (a_z - 1.0) * ka)
        bdir_out[z] = kk * a_z


def _block_diag2(m):
    z = jnp.zeros_like(m[0])
    return jnp.concatenate([jnp.concatenate([m[0], z], 1), jnp.concatenate([z, m[1]], 1)], 0)


def _rwkv_pre(x, regions, gain, mu, w_rkv, w0, w1, w2, a0, a1, a2, g1, g2, k_k, k_a, r_k, tm=256):
    t, d = x.shape
    e, et = _head_indicator(d)
    wl1 = jnp.stack([jnp.concatenate([w1[0], w1[1]], 1), jnp.concatenate([a1[0], a1[1]], 1), g1]).astype(BF16)
    w2c = _block_diag2(w2).astype(BF16)
    a2c = _block_diag2(a2).astype(BF16)
    row = lambda p: p.reshape(1, -1).astype(F32)
    tok = pl.BlockSpec((tm, d), lambda i: (i, 0))
    halo_blocks = t // 8
    prev = pl.BlockSpec((8, d), lambda i: (jnp.maximum(i * (tm // 8) - 1, 0), 0))
    nxt = pl.BlockSpec((8, d), lambda i: (jnp.minimum((i + 1) * (tm // 8), halo_blocks - 1), 0))
    dir_tok = pl.BlockSpec((2, tm, d), lambda i: (0, i, 0))
    consts = [row(gain), mu.astype(F32), w_rkv.astype(BF16), wl1, w2c, a2c, g2.astype(BF16),
              row(w0), row(a0), row(k_k), row(k_a), row(r_k), e, et]
    tok_shape = jax.ShapeDtypeStruct((t, d), F32)
    dir_shape = jax.ShapeDtypeStruct((2, t, d), F32)
    return pl.pallas_call(
        functools.partial(_rwkv_pre_kernel, tm=tm, regions=regions),
        grid=(t // tm,),
        in_specs=[tok, prev, nxt] + [_const_spec(c.shape) for c in consts],
        out_specs=[tok] * 5 + [dir_tok] * 3,
        out_shape=[tok_shape] * 5 + [dir_shape] * 3,
        compiler_params=_cparams("arbitrary"),
        name="rwkv_pre",
    )(x, x, x, *consts)


_NN = ((1,), (0,))
_NT = ((1,), (1,))
_TN = ((0,), (0,))


def _mm(a, b, dims=_NN):
    return lax.dot_general(a.astype(BF16), b.astype(BF16), (dims, ((), ())), preferred_element_type=F32)


def _cumsum_rows(x, reverse):
    n = x.shape[0]
    row = lax.broadcasted_iota(jnp.int32, x.shape, 0)
    s = 1
    while s < n:
        if reverse:
            x = x + jnp.where(row < n - s, pltpu.roll(x, n - s, 0), 0.0)
        else:
            x = x + jnp.where(row >= s, pltpu.roll(x, s, 0), 0.0)
        s *= 2
    return x


def _wkv_kernel(r_ref, v_ref, kk_ref, logw_ref, kdir_ref, bdir_ref, y_ref, state_ref,
                *, rb, regions, reverse):
    tp, sp, ss = regions
    c = SCAN_CHUNK
    npairs = r_ref.shape[-1] // LANES
    nsteps = pl.num_programs(1)
    step = pl.program_id(1)
    blk = (nsteps - 1 - step) if reverse else step
    row0 = blk * rb
    seq = _seq_len_at(row0, tp, sp, ss)
    first = (lax.rem(row0 + rb, seq) == 0) if reverse else (lax.rem(row0, seq) == 0)

    @pl.when(first)
    def _():
        state_ref[...] = jnp.zeros_like(state_ref)

    ti2 = lax.broadcasted_iota(jnp.int32, (c, 2 * c), 0)
    tj2 = lax.broadcasted_iota(jnp.int32, (c, 2 * c), 1) & (c - 1)
    ti4 = lax.broadcasted_iota(jnp.int32, (c, 4 * c), 0)
    tj4 = lax.broadcasted_iota(jnp.int32, (c, 4 * c), 1) & (c - 1)
    incl4 = (tj4 >= ti4) if reverse else (tj4 <= ti4)
    strict2 = (tj2 > ti2) if reverse else (tj2 < ti2)
    lane = lax.broadcasted_iota(jnp.int32, (c, LANES), 1)
    head0 = lane < HEAD
    rr = lax.broadcasted_iota(jnp.int32, (LANES, LANES), 0)
    cc = lax.broadcasted_iota(jnp.int32, (LANES, LANES), 1)
    same_head = (rr < HEAD) == (cc < HEAD)

    col4 = lax.broadcasted_iota(jnp.int32, (c, 4 * c), 1)
    eye4 = ((col4 & (c - 1)) == ti4).astype(F32)

    def block_diag4(m):
        return jnp.concatenate([jnp.where((col4 >= h * c) & (col4 < (h + 1) * c), m, 0.0) for h in range(4)], 0)

    def stack_heads(m):
        return jnp.concatenate([jnp.where(head0, m, 0.0), jnp.where(head0, 0.0, m)], 0)

    order = list(range(rb // c - 1, -1, -1) if reverse else range(rb // c))
    units = [(pl.ds(ci * c, c), pl.ds(p * LANES, LANES)) for ci in order for p in range(npairs)]
    each = lambda fn, *lists: [fn(*xs) for xs in zip(*lists)]

    lw = [logw_ref[u] for u in units]
    cum = each(lambda x: _cumsum_rows(x, reverse), lw)
    total = each(lambda x: jnp.sum(x, axis=0, keepdims=True), lw)
    lhs = [jnp.concatenate([r_ref[u] * jnp.exp(cm), kk_ref[u] * jnp.exp(cm - x)], 0).astype(BF16)
           for u, cm, x in zip(units, cum, lw)]
    rhs = []
    kb_tail = []
    for u, cm, tt in zip(units, cum, total):
        kd, bd = kdir_ref[u], bdir_ref[u]
        inv = jnp.exp(-cm)
        tail = jnp.exp(tt - cm)
        rhs.append(jnp.concatenate([stack_heads(kd * inv), stack_heads(bd * inv)], 0).astype(BF16))
        kb_tail.append(jnp.concatenate([kd * tail, -(bd * tail)], 0).astype(BF16))
    q = each(lambda a, b: _mm(a, b, _NT), lhs, rhs)
    a_out = [jnp.where(incl4, x[0:c], 0.0).astype(BF16) for x in q]
    a_kk = [jnp.where(strict2, x[c:2 * c, 0:2 * c], 0.0).astype(BF16) for x in q]
    low = [jnp.where(strict2, x[c:2 * c, 2 * c:4 * c], 0.0) for x in q]
    pw = [-jnp.concatenate([low[i], low[i + 1]], 1) for i in range(0, len(units), 2)]
    inv_cat = [eye4 + x for x in pw]
    pw = each(lambda x: _mm(x, block_diag4(x)), pw)
    for _ in range(int(math.log2(c)) - 2):
        both = each(lambda t, x: _mm(jnp.concatenate([t, x], 0), block_diag4(x)), inv_cat, pw)
        inv_cat = each(lambda t, b: t + b[0:c], inv_cat, both)
        pw = [b[c:2 * c] for b in both]
    inv_cat = each(lambda t, x: t + _mm(t, block_diag4(x)), inv_cat, pw)
    inv_l = [stack_heads(t[:, h * 2 * c:(h + 1) * 2 * c]).astype(BF16) for t in inv_cat for h in range(2)]
    v_all = [v_ref[u] for u in units]
    v_bd = [stack_heads(x).astype(BF16) for x in v_all]
    akv = each(_mm, a_kk, v_bd)
    decay = [jnp.exp(x) for x in total]

    states = [state_ref[p] for p in range(npairs)]
    for k in range(len(order)):
        idx = list(range(k * npairs, (k + 1) * npairs))
        sk = [_mm(lhs[i], states[p], _NT) for p, i in enumerate(idx)]
        u_bd = [_mm(inv_l[i], stack_heads(s[c:2 * c] + akv[i])) for s, i in zip(sk, idx)]
        y = [s[0:c] + _mm(a_out[i], jnp.concatenate([v_bd[i], (-ub).astype(BF16)], 0))
             for s, ub, i in zip(sk, u_bd, idx)]
        for yy, i in zip(y, idx):
            y_ref[units[i]] = yy
        upd = [_mm(jnp.concatenate([v_all[i], ub[0:c] + ub[c:2 * c]], 0), kb_tail[i], _TN)
               for ub, i in zip(u_bd, idx)]
        states = [states[p] * decay[i] + jnp.where(same_head, upd[p], 0.0) for p, i in enumerate(idx)]
    for p in range(npairs):
        state_ref[p] = states[p]


def _wkv(r, v, kk, logw, kdir, bdir, regions, reverse, rb=128, width=1024):
    t, d = r.shape
    z = 1 if reverse else 0
    nsteps = t // rb

    def row_block(s):
        return (nsteps - 1 - s) if reverse else s

    tok = pl.BlockSpec((rb, width), lambda j, s: (row_block(s), j))
    dir_tok = pl.BlockSpec((None, rb, width), lambda j, s: (z, row_block(s), j))
    return pl.pallas_call(
        functools.partial(_wkv_kernel, rb=rb, regions=regions, reverse=reverse),
        grid=(d // width, nsteps),
        in_specs=[tok, tok, tok, dir_tok, dir_tok, dir_tok],
        out_specs=tok,
        out_shape=jax.ShapeDtypeStruct((t, d), F32),
        scratch_shapes=[pltpu.VMEM((width // LANES, LANES, LANES), F32)],
        compiler_params=_cparams("arbitrary", "arbitrary"),
        name="wkv_bwd" if reverse else "wkv_fwd",
    )(r, v, kk, logw, kdir, bdir)


def _rwkv_post_kernel(yf_ref, yb_ref, bonus_ref, g_ref, x_ref, lng_ref, lnb_ref, wo_ref, e_ref, et_ref, out_ref):
    y = yf_ref[...] + yb_ref[...]
    mean = _head_sum(y, e_ref, et_ref) * (1.0 / HEAD)
    dlt = y - mean
    var = _head_sum(dlt * dlt, e_ref, et_ref) * (1.0 / HEAD)
    yn = dlt * lax.rsqrt(var + GN_EPS) * lng_ref[...] + lnb_ref[...]
    mixed = ((yn + bonus_ref[...]) * g_ref[...]).astype(BF16)
    out_ref[...] = x_ref[...] + jnp.dot(mixed, wo_ref[...], preferred_element_type=F32)


def _rwkv_post(yf, yb, bonus, g, x, ln_g, ln_b, w_o, tm=256):
    t, d = x.shape
    e, et = _head_indicator(d)
    tok = pl.BlockSpec((tm, d), lambda i: (i, 0))
    consts = [ln_g.reshape(1, d), ln_b.reshape(1, d), w_o.astype(BF16), e, et]
    return pl.pallas_call(
        _rwkv_post_kernel,
        grid=(t // tm,),
        in_specs=[tok] * 5 + [_const_spec(c.shape) for c in consts],
        out_specs=tok,
        out_shape=jax.ShapeDtypeStruct((t, d), F32),
        compiler_params=_cparams("arbitrary"),
        name="rwkv_post",
    )(yf, yb, bonus, g, x, *consts)


def _ffn_kernel(x_ref, gain_ref, wg_ref, wu_ref, wd_ref, out_ref, h_ref, acc_ref):
    f = pl.program_id(1)

    @pl.when(f == 0)
    def _():
        h_ref[...] = _rms(x_ref[...], gain_ref[...]).astype(BF16)
        acc_ref[...] = jnp.zeros_like(acc_ref)

    h = h_ref[...]
    gate = jnp.dot(h, wg_ref[...], preferred_element_type=F32)
    up = jnp.dot(h, wu_ref[...], preferred_element_type=F32)
    hidden = (jax.nn.silu(gate) * up).astype(BF16)
    acc_ref[...] += jnp.dot(hidden, wd_ref[...], preferred_element_type=F32)

    @pl.when(f == pl.num_programs(1) - 1)
    def _():
        out_ref[...] = x_ref[...] + acc_ref[...]


def _ffn(x, gain, w_gate, w_up, w_down, tm=512, tf=2816):
    t, d = x.shape
    dff = w_gate.shape[1]
    return pl.pallas_call(
        _ffn_kernel,
        grid=(t // tm, dff // tf),
        in_specs=[pl.BlockSpec((tm, d), lambda i, f: (i, 0)),
                  _const_spec((1, d)),
                  pl.BlockSpec((d, tf), lambda i, f: (0, f)),
                  pl.BlockSpec((d, tf), lambda i, f: (0, f)),
                  pl.BlockSpec((tf, d), lambda i, f: (f, 0))],
        out_specs=pl.BlockSpec((tm, d), lambda i, f: (i, 0)),
        out_shape=jax.ShapeDtypeStruct((t, d), F32),
        scratch_shapes=[pltpu.VMEM((tm, d), BF16), pltpu.VMEM((tm, d), F32)],
        compiler_params=_cparams("arbitrary", "arbitrary"),
        name="ffn",
    )(x, gain.reshape(1, d), w_gate.astype(BF16), w_up.astype(BF16), w_down.astype(BF16))


def _router_kernel(x_ref, gain_ref, wr_ref, tri_ref, h_out, comb_out, rank_out, count_out, carry_ref,
                   *, n_experts):
    @pl.when(pl.program_id(0) == 0)
    def _():
        carry_ref[...] = jnp.zeros_like(carry_ref)

    h = _rms(x_ref[...], gain_ref[...])
    h_out[...] = h
    logits = jnp.dot(h.astype(BF16), wr_ref[...], preferred_element_type=F32)
    lane = lax.broadcasted_iota(jnp.int32, logits.shape, 1)
    logits = jnp.where(lane < n_experts, logits, -jnp.inf)
    m1 = jnp.max(logits, axis=-1, keepdims=True)
    i1 = jnp.min(jnp.where(logits == m1, lane, LANES), axis=-1, keepdims=True)
    rest = jnp.where(lane == i1, -jnp.inf, logits)
    m2 = jnp.max(rest, axis=-1, keepdims=True)
    i2 = jnp.min(jnp.where(rest == m2, lane, LANES), axis=-1, keepdims=True)
    e2 = jnp.exp(m2 - m1)
    g1 = 1.0 / (1.0 + e2)
    g2 = e2 / (1.0 + e2)
    comb_out[...] = jnp.where(lane == i1, g1, 0.0) + jnp.where(lane == i2, g2, 0.0)
    sel = jnp.where((lane == i1) | (lane == i2), 1.0, 0.0)
    earlier = carry_ref[0:1, :] + jnp.dot(tri_ref[...], sel.astype(BF16), preferred_element_type=F32)
    rank_out[...] = sel * (earlier + 1.0)
    carry_ref[...] = carry_ref[...] + jnp.sum(sel, axis=0, keepdims=True)
    count_out[...] = carry_ref[...]


def _router(x, gain, w_router, tm=512):
    t, d = x.shape
    n_experts = w_router.shape[1]
    wr = jnp.zeros((d, LANES), F32).at[:, :n_experts].set(w_router).astype(BF16)
    tri = jnp.asarray(np.tril(np.ones((tm, tm), np.float32), -1), BF16)
    tok = pl.BlockSpec((tm, d), lambda i: (i, 0))
    lanes = pl.BlockSpec((tm, LANES), lambda i: (i, 0))
    return pl.pallas_call(
        functools.partial(_router_kernel, n_experts=n_experts),
        grid=(t // tm,),
        in_specs=[tok, _const_spec((1, d)), _const_spec((d, LANES)), _const_spec((tm, tm))],
        out_specs=[tok, lanes, lanes, pl.BlockSpec((8, LANES), lambda i: (0, 0))],
        out_shape=[jax.ShapeDtypeStruct((t, d), F32), jax.ShapeDtypeStruct((t, LANES), F32),
                   jax.ShapeDtypeStruct((t, LANES), F32), jax.ShapeDtypeStruct((8, LANES), F32)],
        scratch_shapes=[pltpu.VMEM((8, LANES), F32)],
        compiler_params=_cparams("arbitrary"),
        name="moe_router",
    )(x, gain.reshape(1, d), wr, tri)


def _route_plan(comb, rank, counts, n_experts, tile):
    t = comb.shape[0]
    n_tiles = (2 * t) // tile + n_experts
    cnt = counts[0, :n_experts].astype(jnp.int32)
    padded = ((cnt + tile - 1) // tile) * tile
    ends = jnp.cumsum(padded)
    starts = jnp.zeros((LANES,), jnp.int32).at[:n_experts].set(ends - padded)
    slot = jnp.where(rank > 0, starts[None, :] + rank.astype(jnp.int32) - 1, -1)
    slot_hi = jnp.max(slot, axis=1)
    slot_lo = jnp.min(jnp.where(slot >= 0, slot, jnp.iinfo(jnp.int32).max), axis=1)
    gate_lo = jnp.sum(jnp.where(slot == slot_lo[:, None], comb, 0.0), axis=1)
    gate_hi = jnp.sum(jnp.where(slot == slot_hi[:, None], comb, 0.0), axis=1)
    slots = jnp.stack([slot_lo, slot_hi], 1)
    gates = jnp.stack([gate_lo, gate_hi], 1)
    tile_expert = jnp.sum(jnp.arange(n_tiles, dtype=jnp.int32)[:, None] * tile >= ends[None, :], axis=1)
    return slots, gates, jnp.minimum(tile_expert, n_experts - 1).astype(jnp.int32), n_tiles


def _row_copy(src, src_row, dst, dst_row, sem):
    return pltpu.make_async_copy(src.at[pl.ds(src_row, 1)], dst.at[pl.ds(dst_row, 1)], sem)


def _dispatch_kernel(slots_ref, h_ref, init_hbm, xs_hbm, sem, *, tm):
    del init_hbm

    def issue(i, carry):
        for k in range(2):
            _row_copy(h_ref, i, xs_hbm, slots_ref[0, 2 * i + k], sem).start()
        return carry

    lax.fori_loop(0, tm, issue, 0, unroll=4)
    for _ in range(2):
        pltpu.make_async_copy(h_ref, xs_hbm.at[pl.ds(0, tm)], sem).wait()


def _dispatch(h, slots, n_rows, tm=512):
    t, d = h.shape
    return pl.pallas_call(
        functools.partial(_dispatch_kernel, tm=tm),
        grid=(t // tm,),
        in_specs=[pl.BlockSpec((None, 1, 2 * tm), lambda i: (i, 0, 0), memory_space=pltpu.SMEM),
                  pl.BlockSpec((tm, d), lambda i: (i, 0)), pl.BlockSpec(memory_space=pl.ANY)],
        out_specs=pl.BlockSpec(memory_space=pl.ANY),
        out_shape=jax.ShapeDtypeStruct((n_rows, d), h.dtype),
        scratch_shapes=[pltpu.SemaphoreType.DMA(())],
        input_output_aliases={2: 0},
        compiler_params=pltpu.CompilerParams(dimension_semantics=("arbitrary",), has_side_effects=True),
        name="moe_dispatch",
    )(slots.reshape(t // tm, 1, 2 * tm), h, jnp.zeros((n_rows, d), h.dtype))


def _experts_kernel(te_ref, xs_ref, wg_ref, wu_ref, wd_ref, ys_ref, xb_ref, acc_ref):
    del te_ref
    f = pl.program_id(1)

    @pl.when(f == 0)
    def _():
        xb_ref[...] = xs_ref[...].astype(BF16)
        acc_ref[...] = jnp.zeros_like(acc_ref)

    xb = xb_ref[...]
    gate = jnp.dot(xb, wg_ref[...], preferred_element_type=F32)
    up = jnp.dot(xb, wu_ref[...], preferred_element_type=F32)
    hidden = (jax.nn.silu(gate) * up).astype(BF16)
    acc_ref[...] += jnp.dot(hidden, wd_ref[...], preferred_element_type=F32)

    @pl.when(f == pl.num_programs(1) - 1)
    def _():
        ys_ref[...] = acc_ref[...]


def _experts(xs, tile_expert, w_gate, w_up, w_down, tile, tf=1792):
    n_rows, d = xs.shape
    dff = w_gate.shape[2]
    grid_spec = pltpu.PrefetchScalarGridSpec(
        num_scalar_prefetch=1,
        grid=(n_rows // tile, dff // tf),
        in_specs=[pl.BlockSpec((tile, d), lambda i, f, te: (i, 0)),
                  pl.BlockSpec((None, d, tf), lambda i, f, te: (te[i], 0, f)),
                  pl.BlockSpec((None, d, tf), lambda i, f, te: (te[i], 0, f)),
                  pl.BlockSpec((None, tf, d), lambda i, f, te: (te[i], f, 0))],
        out_specs=pl.BlockSpec((tile, d), lambda i, f, te: (i, 0)),
        scratch_shapes=[pltpu.VMEM((tile, d), BF16), pltpu.VMEM((tile, d), F32)])
    return pl.pallas_call(
        _experts_kernel,
        grid_spec=grid_spec,
        out_shape=jax.ShapeDtypeStruct((n_rows, d), F32),
        compiler_params=_cparams("arbitrary", "arbitrary"),
        name="moe_experts",
    )(tile_expert, xs, w_gate.astype(BF16), w_up.astype(BF16), w_down.astype(BF16))


def _combine_kernel(slots_ref, x_ref, gates_ref, ys_hbm, out_ref, buf_ref, sem, *, tm):
    def issue(i, carry):
        for k in range(2):
            _row_copy(ys_hbm, slots_ref[0, 2 * i + k], buf_ref.at[k], i, sem).start()
        return carry

    lax.fori_loop(0, tm, issue, 0, unroll=4)
    for k in range(2):
        pltpu.make_async_copy(ys_hbm.at[pl.ds(0, tm)], buf_ref.at[k], sem).wait()
    gates = gates_ref[...]
    out_ref[...] = x_ref[...] + gates[:, 0:1] * buf_ref[0] + gates[:, 1:2] * buf_ref[1]


def _combine(x, ys, slots, gates, tm=256):
    t, d = x.shape
    tok = pl.BlockSpec((tm, d), lambda i: (i, 0))
    return pl.pallas_call(
        functools.partial(_combine_kernel, tm=tm),
        grid=(t // tm,),
        in_specs=[pl.BlockSpec((None, 1, 2 * tm), lambda i: (i, 0, 0), memory_space=pltpu.SMEM),
                  tok, pl.BlockSpec((tm, 2), lambda i: (i, 0)), pl.BlockSpec(memory_space=pl.ANY)],
        out_specs=tok,
        out_shape=jax.ShapeDtypeStruct((t, d), F32),
        scratch_shapes=[pltpu.VMEM((2, tm, d), F32), pltpu.SemaphoreType.DMA(())],
        compiler_params=_cparams("arbitrary"),
        name="moe_combine",
    )(slots.reshape(t // tm, 1, 2 * tm), x, gates, ys)


def _moe(x, gain, w_router, w_gate, w_up, w_down, tile=512):
    n_experts = w_gate.shape[0]
    h, comb, rank, counts = _router(x, gain, w_router)
    slots, gates, tile_expert, n_tiles = _route_plan(comb, rank, counts, n_experts, tile)
    xs = _dispatch(h, slots, n_tiles * tile)
    ys = _experts(xs, tile_expert, w_gate, w_up, w_down, tile)
    return _combine(x, ys, slots, gates)


def _lane_chunks(width):
    return [slice(c * LANES, (c + 1) * LANES) for c in range(width // LANES)]


def _store_dilated(dst_ref, stage_ref, value, dil):
    rows, width = value.shape
    if dil == 1:
        dst_ref[...] = value.astype(dst_ref.dtype)
        return
    for c, ls in enumerate(_lane_chunks(width)):
        stage_ref[c] = value[:, ls]
    for rho in range(dil):
        for c in range(width // LANES):
            dst_ref[:, pl.ds(rho * width + c * LANES, LANES)] = (
                stage_ref.at[c][pl.ds(rho, rows // dil, stride=dil), :].astype(dst_ref.dtype))


def _load_dilated(src_ref, stage_ref, base, rows, width, dil):
    if dil == 1:
        return src_ref[...].astype(F32)
    for rho in range(dil):
        for c in range(width // LANES):
            stage_ref.at[base + c][pl.ds(rho, rows // dil, stride=dil), :] = (
                src_ref[:, pl.ds(rho * width + c * LANES, LANES)].astype(F32))
    return jnp.concatenate([stage_ref[base + c] for c in range(width // LANES)], 1)


def _attn_in_kernel(x_ref, gain_ref, w_ref, hg_ref, e_ref, et_ref, *rest, dils):
    outs, (h_ref, stage_ref) = rest[:len(dils)], rest[len(dils):]
    g = pl.program_id(1)

    @pl.when(g == 0)
    def _():
        h_ref[...] = _rms(x_ref[...], gain_ref[...]).astype(BF16)

    n_qk = 2 * GROUP_WIDTH
    half = h_ref.shape[0] // 2
    ys = [jnp.dot(h_ref[pl.ds(r * half, half), :], w_ref[...], preferred_element_type=F32) for r in range(2)]
    sq = [_split_dot(yy[:, :n_qk] * yy[:, :n_qk], e_ref[...]) for yy in ys]
    ms = [_split_dot(x, et_ref[...]) * (1.0 / HEAD) for x in sq]
    qk = [yy[:, :n_qk] * lax.rsqrt(m + RMS_EPS) * hg_ref[...] for yy, m in zip(ys, ms)]
    y = jnp.concatenate([jnp.concatenate([a, yy[:, n_qk:]], 1) for a, yy in zip(qk, ys)], 0)
    for gi, dil in enumerate(dils):
        @pl.when(g == gi)
        def _(gi=gi, dil=dil):
            _store_dilated(outs[gi], stage_ref, y, dil)


def _attn_in(x, gain, w_in, q_gain, k_gain, tm=512):
    t, d = x.shape
    n_groups = q_gain.shape[0]
    gw = 3 * GROUP_WIDTH
    dils = tuple(dil for _, dil in ATTN_GROUPS)
    w = jnp.transpose(w_in.reshape(d, 3, n_groups, GROUP_WIDTH), (2, 0, 1, 3)).reshape(n_groups, d, gw).astype(BF16)
    head_gain = jnp.concatenate([jnp.tile(q_gain, (1, GROUP_HEADS)), jnp.tile(k_gain, (1, GROUP_HEADS))],
                                1).reshape(n_groups, 1, 2 * GROUP_WIDTH)
    e, et = _head_indicator(2 * GROUP_WIDTH)
    return pl.pallas_call(
        functools.partial(_attn_in_kernel, dils=dils),
        grid=(t // tm, n_groups),
        in_specs=[pl.BlockSpec((tm, d), lambda i, g: (i, 0)),
                  _const_spec((1, d)),
                  pl.BlockSpec((None, d, gw), lambda i, g: (g, 0, 0)),
                  pl.BlockSpec((None, 1, 2 * GROUP_WIDTH), lambda i, g: (g, 0, 0)),
                  _const_spec(e.shape), _const_spec(et.shape)],
        out_specs=[pl.BlockSpec((tm // dil, dil * gw), lambda i, g: (i, 0)) for dil in dils],
        out_shape=[jax.ShapeDtypeStruct((t // dil, dil * gw), BF16) for dil in dils],
        scratch_shapes=[pltpu.VMEM((tm, d), BF16), pltpu.VMEM((gw // LANES, tm, LANES), F32)],
        compiler_params=_cparams("arbitrary", "arbitrary"),
        name="attn_in",
    )(x, gain.reshape(1, d), w, head_gain, e, et)


def _t5_bucket_np(rel):
    nb = REL_BUCKETS // 2
    max_exact = nb // 2
    n = np.abs(rel)
    large = max_exact + (np.log(np.maximum(n, 1).astype(np.float32) / max_exact)
                         / math.log(REL_MAX_DIST / max_exact) * (nb - max_exact)).astype(np.int32)
    large = np.minimum(large, nb - 1)
    return np.where(rel > 0, nb, 0) + np.where(n < max_exact, n, large)


def _bias_table(rel_bias, group, window, dil):
    side = window // (2 * dil)
    assert side == KV_HALO
    buckets = _t5_bucket_np(dil * np.arange(-side, side + 1))
    table = rel_bias.reshape(REL_BUCKETS, -1, GROUP_HEADS)[buckets, group].T
    full = jnp.full((8, Q_BLOCK + 2 * KV_HALO), NEG_INF, F32)
    return full.at[:GROUP_HEADS, :2 * side + 1].set(table)


def _attn_kernel(q_ref, kp_ref, kc_ref, kn_ref, vp_ref, vc_ref, vn_ref, table_ref, o_ref, lse_ref, bias_ref,
                 *, regions, dil):
    tp, sp, ss = regions
    nk = Q_BLOCK + 2 * KV_HALO

    @pl.when((pl.program_id(0) == 0) & (pl.program_id(1) == 0))
    def _():
        for h in range(GROUP_HEADS):
            row = jnp.broadcast_to(table_ref[h:h + 1, :], (Q_BLOCK, nk))
            bias_ref[h] = pltpu.roll(row, 0, 1, stride=1, stride_axis=0)

    row0 = pl.program_id(1) * Q_BLOCK
    seq = _seq_len_at(row0, tp // dil, sp // dil, ss // dil)
    seq_start = row0 - lax.rem(row0, seq)
    key_row = row0 - KV_HALO + lax.broadcasted_iota(jnp.int32, (1, nk), 1)
    valid = (key_row >= seq_start) & (key_row < seq_start + seq)
    lane = lax.broadcasted_iota(jnp.int32, (Q_BLOCK, LANES), 1)
    scale = HEAD ** -0.5

    q = q_ref[...]
    k = jnp.concatenate([kp_ref[...], kc_ref[...], kn_ref[...]], 0)
    v = jnp.concatenate([vp_ref[...], vc_ref[...], vn_ref[...]], 0)
    heads = [(pair, sub) for pair in range(GROUP_HEADS // 2) for sub in range(2)]
    lanes_of = lambda pair: slice(pair * LANES, (pair + 1) * LANES)
    mine = [(lane < HEAD), (lane >= HEAD)]
    s_all = []
    for pair, sub in heads:
        qp = q[:, lanes_of(pair)]
        qm = jnp.where(mine[sub], qp, jnp.zeros_like(qp))
        s = lax.dot_general(qm, k[:, lanes_of(pair)], (_NT, ((), ())), preferred_element_type=F32)
        s_all.append(jnp.where(valid, s * scale + bias_ref[2 * pair + sub], NEG_INF))
    m_all = [jnp.max(s, axis=-1, keepdims=True) for s in s_all]
    p_all = [jnp.exp(s - m) for s, m in zip(s_all, m_all)]
    den_all = [jnp.sum(p, axis=-1, keepdims=True) for p in p_all]
    o_all = [jnp.dot((p / den).astype(v.dtype), v[:, lanes_of(pair)], preferred_element_type=F32)
             for p, den, (pair, _) in zip(p_all, den_all, heads)]
    lse_all = [m + jnp.log(den) for m, den in zip(m_all, den_all)]
    for pair in range(GROUP_HEADS // 2):
        a, b = 2 * pair, 2 * pair + 1
        o_ref[:, lanes_of(pair)] = jnp.where(mine[0], o_all[a], o_all[b]).astype(o_ref.dtype)
        lse_ref[:, lanes_of(pair)] = jnp.where(mine[0], lse_all[a], lse_all[b])


def _attn_group(qkv_d, rel_bias, group, regions):
    window, dil = ATTN_GROUPS[group]
    rows = qkv_d.shape[0]
    parts = 3
    table = _bias_table(rel_bias, group, window, dil)
    halo_per_q = Q_BLOCK // KV_HALO
    last_halo = rows // KV_HALO - 1

    def cur(part):
        return pl.BlockSpec((Q_BLOCK, GROUP_WIDTH), lambda rho, m: (m, rho * parts + part))

    def prev(part):
        return pl.BlockSpec((KV_HALO, GROUP_WIDTH),
                            lambda rho, m: (jnp.maximum(m * halo_per_q - 1, 0), rho * parts + part))

    def nxt(part):
        return pl.BlockSpec((KV_HALO, GROUP_WIDTH),
                            lambda rho, m: (jnp.minimum((m + 1) * halo_per_q, last_halo), rho * parts + part))

    out_spec = pl.BlockSpec((Q_BLOCK, GROUP_WIDTH), lambda rho, m: (m, rho))
    return pl.pallas_call(
        functools.partial(_attn_kernel, regions=regions, dil=dil),
        grid=(dil, rows // Q_BLOCK),
        in_specs=[cur(0), prev(1), cur(1), nxt(1), prev(2), cur(2), nxt(2), _const_spec(table.shape)],
        out_specs=[out_spec, out_spec],
        out_shape=[jax.ShapeDtypeStruct((rows, dil * GROUP_WIDTH), BF16),
                   jax.ShapeDtypeStruct((rows, dil * GROUP_WIDTH), F32)],
        scratch_shapes=[pltpu.VMEM((GROUP_HEADS, Q_BLOCK, Q_BLOCK + 2 * KV_HALO), F32)],
        compiler_params=_cparams("arbitrary", "arbitrary"),
        name=f"attn_g{group}",
    )(qkv_d, qkv_d, qkv_d, qkv_d, qkv_d, qkv_d, qkv_d, table)


def _attn_out_kernel(o0, o1, o2, l0, l1, l2, x_ref, wo_ref, out_ref, stage_ref, *, dils):
    tm = x_ref.shape[0]
    chunks = GROUP_WIDTH // LANES
    os_ = [_load_dilated(r, stage_ref, 2 * g * chunks, tm, GROUP_WIDTH, dils[g]) for g, r in enumerate((o0, o1, o2))]
    lses = [_load_dilated(r, stage_ref, (2 * g + 1) * chunks, tm, GROUP_WIDTH, dils[g])
            for g, r in enumerate((l0, l1, l2))]
    mx = jnp.maximum(jnp.maximum(lses[0], lses[1]), lses[2])
    ws = [jnp.exp(l - mx) for l in lses]
    tot = ws[0] + ws[1] + ws[2]
    acc = x_ref[...]
    for g in range(len(dils)):
        scaled = (os_[g] * (ws[g] / tot)).astype(BF16)
        acc = acc + jnp.dot(scaled, wo_ref[g], preferred_element_type=F32)
    out_ref[...] = acc


def _attn_out(os_, lses, x, w_o, tm=512):
    t, d = x.shape
    n_groups = len(os_)
    dils = tuple(dil for _, dil in ATTN_GROUPS)
    grp = [pl.BlockSpec((tm // dil, dil * GROUP_WIDTH), lambda i: (i, 0)) for dil in dils]
    tok = pl.BlockSpec((tm, d), lambda i: (i, 0))
    wo = w_o.reshape(n_groups, GROUP_WIDTH, d).astype(BF16)
    return pl.pallas_call(
        functools.partial(_attn_out_kernel, dils=dils),
        grid=(t // tm,),
        in_specs=grp + grp + [tok, _const_spec(wo.shape)],
        out_specs=tok,
        out_shape=jax.ShapeDtypeStruct((t, d), F32),
        scratch_shapes=[pltpu.VMEM((2 * n_groups * (GROUP_WIDTH // LANES), tm, LANES), F32)],
        compiler_params=_cparams("arbitrary"),
        name="attn_out",
    )(*os_, *lses, x, wo)


def kernel(x_prompt, x_sample, norm_mix, norm_ffn, rwkv_mu, rwkv_w_rkv, rwkv_w0, rwkv_w1, rwkv_w2, rwkv_a0, rwkv_a1, rwkv_a2, rwkv_g1, rwkv_g2, rwkv_k_k, rwkv_k_a, rwkv_r_k, rwkv_ln_g, rwkv_ln_b, rwkv_w_o, attn_w_in, attn_q_gain, attn_k_gain, attn_w_o, rel_bias, ffn_w_gate, ffn_w_up, ffn_w_down, moe_router, moe_w_gate, moe_w_up, moe_w_down):
    d = x_prompt.shape[-1]
    tp = x_prompt.shape[0] * x_prompt.shape[1]
    regions = (tp, x_prompt.shape[1], x_sample.shape[1])
    x = jnp.concatenate([x_prompt.reshape(-1, d), x_sample.reshape(-1, d)], 0)
    depth = norm_mix.shape[0]
    for i in range(depth):
        j = i // 2
        if i % 2 == 0:
            r, v, kk, g, bonus, logw, kdir, bdir = _rwkv_pre(
                x, regions, norm_mix[i], rwkv_mu[j], rwkv_w_rkv[j], rwkv_w0[j], rwkv_w1[j], rwkv_w2[j],
                rwkv_a0[j], rwkv_a1[j], rwkv_a2[j], rwkv_g1[j], rwkv_g2[j], rwkv_k_k[j], rwkv_k_a[j], rwkv_r_k[j])
            yf = _wkv(r, v, kk, logw, kdir, bdir, regions, reverse=False)
            yb = _wkv(r, v, kk, logw, kdir, bdir, regions, reverse=True)
            x = _rwkv_post(yf, yb, bonus, g, x, rwkv_ln_g[j], rwkv_ln_b[j], rwkv_w_o[j])
            x = _ffn(x, norm_ffn[i], ffn_w_gate[j], ffn_w_up[j], ffn_w_down[j])
        else:
            qkv = _attn_in(x, norm_mix[i], attn_w_in[j], attn_q_gain[j], attn_k_gain[j])
            outs = [_attn_group(qkv[g], rel_bias, g, regions) for g in range(len(ATTN_GROUPS))]
            x = _attn_out([o for o, _ in outs], [l for _, l in outs], x, attn_w_o[j])
            x = _moe(x, norm_ffn[i], moe_router[j], moe_w_gate[j], moe_w_up[j], moe_w_down[j])
    y_prompt = x[:tp].reshape(x_prompt.shape)
    y_sample = x[tp:].reshape(x_sample.shape)
    return (y_prompt, y_sample)
```

```python
import functools
import math

import numpy as np
import jax
import jax.numpy as jnp
from jax import lax
from jax.experimental import pallas as pl
from jax.experimental.pallas import tpu as pltpu

F32 = jnp.float32
BF16 = jnp.bfloat16

LANES = 128
VMEM_LIMIT_BYTES = 56 * 1024 * 1024

HEAD = 64
RMS_EPS = 1e-6
GN_EPS = 64e-5
ATTN_GROUPS = ((128, 1), (512, 4), (2048, 16))
GROUP_HEADS = 6
GROUP_WIDTH = GROUP_HEADS * HEAD
REL_BUCKETS = 32
REL_MAX_DIST = 1024
Q_BLOCK = 128
Q_STEP = 256
KV_HALO = 64
SCAN_CHUNK = 64
NEG_INF = -1e30


def _cparams(*sem):
    return pltpu.CompilerParams(dimension_semantics=sem, vmem_limit_bytes=VMEM_LIMIT_BYTES)


def _const_spec(shape):
    nd = len(shape)
    return pl.BlockSpec(shape, lambda *_: (0,) * nd, pipeline_mode=pl.Buffered(1))


def _split_dot(x, w):
    hi = x.astype(BF16)
    lo = (x - hi.astype(F32)).astype(BF16)
    return (jnp.dot(hi, w, preferred_element_type=F32)
            + jnp.dot(lo, w, preferred_element_type=F32))


def _head_sum(x, e_ref, et_ref):
    return _split_dot(_split_dot(x, e_ref[...]), et_ref[...])


def _rms(x, gain):
    return x * lax.rsqrt(jnp.mean(x * x, axis=-1, keepdims=True) + RMS_EPS) * gain


def _seq_len_at(row0, tp, sp, ss):
    return jnp.where(row0 < tp, sp, ss)


def _head_indicator(width):
    e = np.zeros((width, LANES), np.float32)
    e[np.arange(width), np.arange(width) // HEAD] = 1.0
    return jnp.asarray(e, BF16), jnp.asarray(e.T, BF16)


def _rwkv_pre_kernel(x_ref, xp_ref, xn_ref, gain_ref, mu_ref, wrkv_ref, wl1_ref, w2_ref, a2_ref,
                     g2_ref, w0_ref, a0_ref, kk_gain_ref, ka_ref, rk_ref, e_ref, et_ref,
                     r_out, v_out, kk_out, g_out, bonus_out, logw_out, kdir_out, bdir_out,
                     *, tm, regions):
    tp, sp, ss = regions
    d = x_ref.shape[-1]
    row0 = pl.program_id(0) * tm
    seq = _seq_len_at(row0, tp, sp, ss)
    at_start = lax.rem(row0, seq) == 0
    at_end = lax.rem(row0 + tm, seq) == 0

    gain = gain_ref[...]
    h = _rms(x_ref[...], gain)
    hp = _rms(xp_ref[...], gain)[7:8, :] * jnp.where(at_start, 0.0, 1.0)
    hn = _rms(xn_ref[...], gain)[0:1, :] * jnp.where(at_end, 0.0, 1.0)
    rows = lax.broadcasted_iota(jnp.int32, (tm, d), 0)
    h_prev = jnp.where(rows == 0, hp, pltpu.roll(h, 1, 0))
    h_next = jnp.where(rows == tm - 1, hn, pltpu.roll(h, tm - 1, 0))
    xx = 0.5 * (h_prev + h_next) - h

    def mixed(n):
        return (h + xx * mu_ref[n:n + 1, :]).astype(BF16)

    r = jnp.dot(mixed(0), wrkv_ref[0], preferred_element_type=F32)
    k = jnp.dot(mixed(1), wrkv_ref[1], preferred_element_type=F32)
    v = jnp.dot(mixed(2), wrkv_ref[2], preferred_element_type=F32)

    t_w = jnp.tanh(jnp.dot(mixed(3), wl1_ref[0], preferred_element_type=F32))
    lw = w0_ref[...] + jnp.dot(t_w.astype(BF16), w2_ref[...], preferred_element_type=F32)
    logw = -math.exp(-0.5) * jax.nn.sigmoid(lw)
    t_a = jnp.dot(mixed(4), wl1_ref[1], preferred_element_type=F32)
    a = jax.nn.sigmoid(a0_ref[...] + jnp.dot(t_a.astype(BF16), a2_ref[...], preferred_element_type=F32))
    t_g = jax.nn.sigmoid(jnp.dot(mixed(5), wl1_ref[2], preferred_element_type=F32))
    g = jnp.dot(t_g.astype(BF16), g2_ref[...], preferred_element_type=F32)

    kk = k * kk_gain_ref[...]
    kk = kk * lax.rsqrt(jnp.maximum(_head_sum(kk * kk, e_ref, et_ref), 1e-12))
    bonus = _head_sum(r * k * rk_ref[...], e_ref, et_ref) * v

    r_out[...] = r
    v_out[...] = v
    kk_out[...] = kk
    g_out[...] = g
    bonus_out[...] = bonus
    ka = ka_ref[...]
    for z in range(2):
        a_z = a[:, z * d:(z + 1) * d]
        logw_out[z] = logw[:, z * d:(z + 1) * d]
        kdir_out[z] = k * (1.0 + (a_z - 1.0) * ka)
        bdir_out[z] = kk * a_z


def _block_diag2(m):
    z = jnp.zeros_like(m[0])
    return jnp.concatenate([jnp.concatenate([m[0], z], 1), jnp.concatenate([z, m[1]], 1)], 0)


def _rwkv_pre(x, regions, gain, mu, w_rkv, w0, w1, w2, a0, a1, a2, g1, g2, k_k, k_a, r_k, tm=256):
    t, d = x.shape
    e, et = _head_indicator(d)
    wl1 = jnp.stack([jnp.concatenate([w1[0], w1[1]], 1), jnp.concatenate([a1[0], a1[1]], 1), g1]).astype(BF16)
    w2c = _block_diag2(w2).astype(BF16)
    a2c = _block_diag2(a2).astype(BF16)
    row = lambda p: p.reshape(1, -1).astype(F32)
    tok = pl.BlockSpec((tm, d), lambda i: (i, 0))
    halo_blocks = t // 8
    prev = pl.BlockSpec((8, d), lambda i: (jnp.maximum(i * (tm // 8) - 1, 0), 0))
    nxt = pl.BlockSpec((8, d), lambda i: (jnp.minimum((i + 1) * (tm // 8), halo_blocks - 1), 0))
    dir_tok = pl.BlockSpec((2, tm, d), lambda i: (0, i, 0))
    consts = [row(gain), mu.astype(F32), w_rkv.astype(BF16), wl1, w2c, a2c, g2.astype(BF16),
              row(w0), row(a0), row(k_k), row(k_a), row(r_k), e, et]
    tok_shape = jax.ShapeDtypeStruct((t, d), F32)
    dir_shape = jax.ShapeDtypeStruct((2, t, d), F32)
    return pl.pallas_call(
        functools.partial(_rwkv_pre_kernel, tm=tm, regions=regions),
        grid=(t // tm,),
        in_specs=[tok, prev, nxt] + [_const_spec(c.shape) for c in consts],
        out_specs=[tok] * 5 + [dir_tok] * 3,
        out_shape=[tok_shape] * 5 + [dir_shape] * 3,
        compiler_params=_cparams("arbitrary"),
        name="rwkv_pre",
    )(x, x, x, *consts)


_NN = ((1,), (0,))
_NT = ((1,), (1,))
_TN = ((0,), (0,))


def _mm(a, b, dims=_NN):
    return lax.dot_general(a.astype(BF16), b.astype(BF16), (dims, ((), ())), preferred_element_type=F32)


def _cumsum_rows(x, reverse):
    n = x.shape[0]
    row = lax.broadcasted_iota(jnp.int32, x.shape, 0)
    s = 1
    while s < n:
        if reverse:
            x = x + jnp.where(row < n - s, pltpu.roll(x, n - s, 0), 0.0)
        else:
            x = x + jnp.where(row >= s, pltpu.roll(x, s, 0), 0.0)
        s *= 2
    return x


def _wkv_kernel(r_ref, v_ref, kk_ref, logw_ref, kdir_ref, bdir_ref, y_ref, state_ref,
                *, rb, regions, reverse, group_chunks=2):
    tp, sp, ss = regions
    c = SCAN_CHUNK
    npairs = r_ref.shape[-1] // LANES
    nsteps = pl.num_programs(1)
    step = pl.program_id(1)
    blk = (nsteps - 1 - step) if reverse else step
    row0 = blk * rb
    seq = _seq_len_at(row0, tp, sp, ss)
    first = (lax.rem(row0 + rb, seq) == 0) if reverse else (lax.rem(row0, seq) == 0)

    @pl.when(first)
    def _():
        state_ref[...] = jnp.zeros_like(state_ref)

    ti2 = lax.broadcasted_iota(jnp.int32, (c, 2 * c), 0)
    tj2 = lax.broadcasted_iota(jnp.int32, (c, 2 * c), 1) & (c - 1)
    ti4 = lax.broadcasted_iota(jnp.int32, (c, 4 * c), 0)
    tj4 = lax.broadcasted_iota(jnp.int32, (c, 4 * c), 1) & (c - 1)
    incl4 = (tj4 >= ti4) if reverse else (tj4 <= ti4)
    strict2 = (tj2 > ti2) if reverse else (tj2 < ti2)
    lane = lax.broadcasted_iota(jnp.int32, (c, LANES), 1)
    head0 = lane < HEAD
    rr = lax.broadcasted_iota(jnp.int32, (LANES, LANES), 0)
    cc = lax.broadcasted_iota(jnp.int32, (LANES, LANES), 1)
    same_head = (rr < HEAD) == (cc < HEAD)

    col4 = lax.broadcasted_iota(jnp.int32, (c, 4 * c), 1)
    eye4 = ((col4 & (c - 1)) == ti4).astype(F32)

    def block_diag4(m):
        return jnp.concatenate([jnp.where((col4 >= h * c) & (col4 < (h + 1) * c), m, 0.0) for h in range(4)], 0)

    def stack_heads(m):
        return jnp.concatenate([jnp.where(head0, m, 0.0), jnp.where(head0, 0.0, m)], 0)

    order = list(range(rb // c - 1, -1, -1) if reverse else range(rb // c))
    each = lambda fn, *lists: [fn(*xs) for xs in zip(*lists)]

    def prepare(chunks, out):
        units = [(pl.ds(ci * c, c), pl.ds(p * LANES, LANES)) for ci in chunks for p in range(npairs)]
        lw = [logw_ref[u] for u in units]
        cum = each(lambda x: _cumsum_rows(x, reverse), lw)
        total = each(lambda x: jnp.sum(x, axis=0, keepdims=True), lw)
        lhs = [jnp.concatenate([r_ref[u] * jnp.exp(cm), kk_ref[u] * jnp.exp(cm - x)], 0).astype(BF16)
               for u, cm, x in zip(units, cum, lw)]
        rhs = []
        kb_tail = []
        for u, cm, tt in zip(units, cum, total):
            kd, bd = kdir_ref[u], bdir_ref[u]
            inv = jnp.exp(-cm)
            tail = jnp.exp(tt - cm)
            rhs.append(jnp.concatenate([stack_heads(kd * inv), stack_heads(bd * inv)], 0).astype(BF16))
            kb_tail.append(jnp.concatenate([kd * tail, -(bd * tail)], 0).astype(BF16))
        yield
        q = each(lambda a, b: _mm(a, b, _NT), lhs, rhs)
        a_out = [jnp.where(incl4, x[0:c], 0.0).astype(BF16) for x in q]
        a_kk = [jnp.where(strict2, x[c:2 * c, 0:2 * c], 0.0).astype(BF16) for x in q]
        low = [jnp.where(strict2, x[c:2 * c, 2 * c:4 * c], 0.0) for x in q]
        pw = [-jnp.concatenate([low[i], low[i + 1]], 1) for i in range(0, len(units), 2)]
        inv_cat = [eye4 + x for x in pw]
        yield
        pw = each(lambda x: _mm(x, block_diag4(x)), pw)
        v_all = [v_ref[u] for u in units]
        v_bd = [stack_heads(x).astype(BF16) for x in v_all]
        akv = each(_mm, a_kk, v_bd)
        for _ in range(int(math.log2(c)) - 2):
            yield
            both = each(lambda t, x: _mm(jnp.concatenate([t, x], 0), block_diag4(x)), inv_cat, pw)
            inv_cat = each(lambda t, b: t + b[0:c], inv_cat, both)
            pw = [b[c:2 * c] for b in both]
        yield
        inv_cat = each(lambda t, x: t + _mm(t, block_diag4(x)), inv_cat, pw)
        inv_l = [stack_heads(t[:, h * 2 * c:(h + 1) * 2 * c]).astype(BF16) for t in inv_cat for h in range(2)]
        out.update(units=units, lhs=lhs, kb_tail=kb_tail, a_out=a_out, akv=akv, inv_l=inv_l, v_all=v_all,
                   v_bd=v_bd, decay=[jnp.exp(x) for x in total])

    def advance(group, k, states):
        pre = {name: vals[k * npairs:(k + 1) * npairs] for name, vals in group.items()}
        sk = each(lambda a, st: _mm(a, st, _NT), pre["lhs"], states)
        yield
        u_bd = each(lambda t, s_, kv: _mm(t, stack_heads(s_[c:2 * c] + kv)), pre["inv_l"], sk, pre["akv"])
        yield
        y = each(lambda s_, a, vb, ub: s_[0:c] + _mm(a, jnp.concatenate([vb, (-ub).astype(BF16)], 0)),
                 sk, pre["a_out"], pre["v_bd"], u_bd)
        for u, yy in zip(pre["units"], y):
            y_ref[u] = yy
        upd = each(lambda vv, ub, kb: _mm(jnp.concatenate([vv, ub[0:c] + ub[c:2 * c]], 0), kb, _TN),
                   pre["v_all"], u_bd, pre["kb_tail"])
        states[:] = each(lambda st, dc, up: st * dc + jnp.where(same_head, up, 0.0), states, pre["decay"], upd)

    def run(*gens):
        live = [g for g in gens if g is not None]
        while live:
            live = [g for g in live if next(g, StopIteration) is not StopIteration]

    states = [state_ref[p] for p in range(npairs)]
    groups = [order[i:i + group_chunks] for i in range(0, len(order), group_chunks)]
    prepared = {}
    run(prepare(groups[0], prepared))
    for gi, chunks in enumerate(groups):
        nxt = {}
        following = prepare(groups[gi + 1], nxt) if gi + 1 < len(groups) else None

        def advance_all(prepared=prepared, chunks=chunks):
            for k in range(len(chunks)):
                yield from advance(prepared, k, states)
                yield

        run(following, advance_all())
        prepared = nxt
    for p in range(npairs):
        state_ref[p] = states[p]


def _wkv(r, v, kk, logw, kdir, bdir, regions, reverse, rb=512, width=1024):
    t, d = r.shape
    z = 1 if reverse else 0
    nsteps = t // rb

    def row_block(s):
        return (nsteps - 1 - s) if reverse else s

    tok = pl.BlockSpec((rb, width), lambda j, s: (row_block(s), j))
    dir_tok = pl.BlockSpec((None, rb, width), lambda j, s: (z, row_block(s), j))
    return pl.pallas_call(
        functools.partial(_wkv_kernel, rb=rb, regions=regions, reverse=reverse),
        grid=(d // width, nsteps),
        in_specs=[tok, tok, tok, dir_tok, dir_tok, dir_tok],
        out_specs=tok,
        out_shape=jax.ShapeDtypeStruct((t, d), F32),
        scratch_shapes=[pltpu.VMEM((width // LANES, LANES, LANES), F32)],
        compiler_params=_cparams("arbitrary", "arbitrary"),
        name="wkv_bwd" if reverse else "wkv_fwd",
    )(r, v, kk, logw, kdir, bdir)


def _rwkv_post_kernel(yf_ref, yb_ref, bonus_ref, g_ref, x_ref, lng_ref, lnb_ref, wo_ref, e_ref, et_ref, out_ref):
    y = yf_ref[...] + yb_ref[...]
    mean = _head_sum(y, e_ref, et_ref) * (1.0 / HEAD)
    dlt = y - mean
    var = _head_sum(dlt * dlt, e_ref, et_ref) * (1.0 / HEAD)
    yn = dlt * lax.rsqrt(var + GN_EPS) * lng_ref[...] + lnb_ref[...]
    mixed = ((yn + bonus_ref[...]) * g_ref[...]).astype(BF16)
    out_ref[...] = x_ref[...] + jnp.dot(mixed, wo_ref[...], preferred_element_type=F32)


def _rwkv_post(yf, yb, bonus, g, x, ln_g, ln_b, w_o, tm=256):
    t, d = x.shape
    e, et = _head_indicator(d)
    tok = pl.BlockSpec((tm, d), lambda i: (i, 0))
    consts = [ln_g.reshape(1, d), ln_b.reshape(1, d), w_o.astype(BF16), e, et]
    return pl.pallas_call(
        _rwkv_post_kernel,
        grid=(t // tm,),
        in_specs=[tok] * 5 + [_const_spec(c.shape) for c in consts],
        out_specs=tok,
        out_shape=jax.ShapeDtypeStruct((t, d), F32),
        compiler_params=_cparams("arbitrary"),
        name="rwkv_post",
    )(yf, yb, bonus, g, x, *consts)


def _ffn_kernel(x_ref, gain_ref, wg_ref, wu_ref, wd_ref, out_ref, h_ref, acc_ref):
    f = pl.program_id(1)

    @pl.when(f == 0)
    def _():
        h_ref[...] = _rms(x_ref[...], gain_ref[...]).astype(BF16)
        acc_ref[...] = jnp.zeros_like(acc_ref)

    h = h_ref[...]
    gate = jnp.dot(h, wg_ref[...], preferred_element_type=F32)
    up = jnp.dot(h, wu_ref[...], preferred_element_type=F32)
    hidden = (jax.nn.silu(gate) * up).astype(BF16)
    acc_ref[...] += jnp.dot(hidden, wd_ref[...], preferred_element_type=F32)

    @pl.when(f == pl.num_programs(1) - 1)
    def _():
        out_ref[...] = x_ref[...] + acc_ref[...]


def _ffn(x, gain, w_gate, w_up, w_down, tm=512, tf=2816):
    t, d = x.shape
    dff = w_gate.shape[1]
    return pl.pallas_call(
        _ffn_kernel,
        grid=(t // tm, dff // tf),
        in_specs=[pl.BlockSpec((tm, d), lambda i, f: (i, 0)),
                  _const_spec((1, d)),
                  pl.BlockSpec((d, tf), lambda i, f: (0, f)),
                  pl.BlockSpec((d, tf), lambda i, f: (0, f)),
                  pl.BlockSpec((tf, d), lambda i, f: (f, 0))],
        out_specs=pl.BlockSpec((tm, d), lambda i, f: (i, 0)),
        out_shape=jax.ShapeDtypeStruct((t, d), F32),
        scratch_shapes=[pltpu.VMEM((tm, d), BF16), pltpu.VMEM((tm, d), F32)],
        compiler_params=_cparams("arbitrary", "arbitrary"),
        name="ffn",
    )(x, gain.reshape(1, d), w_gate.astype(BF16), w_up.astype(BF16), w_down.astype(BF16))


def _router_kernel(x_ref, gain_ref, wr_ref, tri_ref, h_out, comb_out, rank_out, count_out, carry_ref,
                   *, n_experts):
    @pl.when(pl.program_id(0) == 0)
    def _():
        carry_ref[...] = jnp.zeros_like(carry_ref)

    h = _rms(x_ref[...], gain_ref[...])
    h_out[...] = h
    logits = jnp.dot(h.astype(BF16), wr_ref[...], preferred_element_type=F32)
    lane = lax.broadcasted_iota(jnp.int32, logits.shape, 1)
    logits = jnp.where(lane < n_experts, logits, -jnp.inf)
    m1 = jnp.max(logits, axis=-1, keepdims=True)
    i1 = jnp.min(jnp.where(logits == m1, lane, LANES), axis=-1, keepdims=True)
    rest = jnp.where(lane == i1, -jnp.inf, logits)
    m2 = jnp.max(rest, axis=-1, keepdims=True)
    i2 = jnp.min(jnp.where(rest == m2, lane, LANES), axis=-1, keepdims=True)
    e2 = jnp.exp(m2 - m1)
    g1 = 1.0 / (1.0 + e2)
    g2 = e2 / (1.0 + e2)
    comb_out[...] = jnp.where(lane == i1, g1, 0.0) + jnp.where(lane == i2, g2, 0.0)
    sel = jnp.where((lane == i1) | (lane == i2), 1.0, 0.0)
    earlier = carry_ref[0:1, :] + jnp.dot(tri_ref[...], sel.astype(BF16), preferred_element_type=F32)
    rank_out[...] = sel * (earlier + 1.0)
    carry_ref[...] = carry_ref[...] + jnp.sum(sel, axis=0, keepdims=True)
    count_out[...] = carry_ref[...]


def _router(x, gain, w_router, tm=512):
    t, d = x.shape
    n_experts = w_router.shape[1]
    wr = jnp.zeros((d, LANES), F32).at[:, :n_experts].set(w_router).astype(BF16)
    tri = jnp.asarray(np.tril(np.ones((tm, tm), np.float32), -1), BF16)
    tok = pl.BlockSpec((tm, d), lambda i: (i, 0))
    lanes = pl.BlockSpec((tm, LANES), lambda i: (i, 0))
    return pl.pallas_call(
        functools.partial(_router_kernel, n_experts=n_experts),
        grid=(t // tm,),
        in_specs=[tok, _const_spec((1, d)), _const_spec((d, LANES)), _const_spec((tm, tm))],
        out_specs=[tok, lanes, lanes, pl.BlockSpec((8, LANES), lambda i: (0, 0))],
        out_shape=[jax.ShapeDtypeStruct((t, d), F32), jax.ShapeDtypeStruct((t, LANES), F32),
                   jax.ShapeDtypeStruct((t, LANES), F32), jax.ShapeDtypeStruct((8, LANES), F32)],
        scratch_shapes=[pltpu.VMEM((8, LANES), F32)],
        compiler_params=_cparams("arbitrary"),
        name="moe_router",
    )(x, gain.reshape(1, d), wr, tri)


def _route_plan(comb, rank, counts, n_experts, tile):
    t = comb.shape[0]
    n_tiles = (2 * t) // tile + n_experts
    cnt = counts[0, :n_experts].astype(jnp.int32)
    padded = ((cnt + tile - 1) // tile) * tile
    ends = jnp.cumsum(padded)
    starts = jnp.zeros((LANES,), jnp.int32).at[:n_experts].set(ends - padded)
    slot = jnp.where(rank > 0, starts[None, :] + rank.astype(jnp.int32) - 1, -1)
    slot_hi = jnp.max(slot, axis=1)
    slot_lo = jnp.min(jnp.where(slot >= 0, slot, jnp.iinfo(jnp.int32).max), axis=1)
    gate_lo = jnp.sum(jnp.where(slot == slot_lo[:, None], comb, 0.0), axis=1)
    gate_hi = jnp.sum(jnp.where(slot == slot_hi[:, None], comb, 0.0), axis=1)
    slots = jnp.stack([slot_lo, slot_hi], 1)
    gates = jnp.stack([gate_lo, gate_hi], 1)
    tile_expert = jnp.sum(jnp.arange(n_tiles, dtype=jnp.int32)[:, None] * tile >= ends[None, :], axis=1)
    holes = jnp.concatenate([ends - padded + cnt, ends[-1:], ends, jnp.full((1,), n_tiles * tile, jnp.int32)])
    return slots, gates, jnp.minimum(tile_expert, n_experts - 1).astype(jnp.int32), holes.astype(jnp.int32), n_tiles


def _row_copy(src, src_row, dst, dst_row, sem):
    return pltpu.make_async_copy(src.at[pl.ds(src_row, 1)], dst.at[pl.ds(dst_row, 1)], sem)


def _dispatch_kernel(holes_ref, slots_ref, h_ref, xs_hbm, zero_ref, sem, zero_sem, *, tm, n_holes):
    n_ranges = holes_ref.shape[0] // 2

    @pl.when(pl.program_id(0) == 0)
    def _():
        zero_ref[...] = jnp.zeros_like(zero_ref)
        for e in range(n_ranges):
            def fill(row, carry):
                _row_copy(zero_ref, 0, xs_hbm, row, zero_sem).start()
                return carry
            lax.fori_loop(holes_ref[e], holes_ref[n_ranges + e], fill, 0)
        pltpu.make_async_copy(xs_hbm.at[pl.ds(0, n_holes)], xs_hbm.at[pl.ds(0, n_holes)], zero_sem).wait()

    def issue(i, carry):
        for k in range(2):
            _row_copy(h_ref, i, xs_hbm, slots_ref[0, 2 * i + k], sem).start()
        return carry

    lax.fori_loop(0, tm, issue, 0, unroll=4)
    for _ in range(2):
        pltpu.make_async_copy(h_ref, xs_hbm.at[pl.ds(0, tm)], sem).wait()


def _dispatch(h, slots, holes, n_rows, tm=512):
    t, d = h.shape
    grid_spec = pltpu.PrefetchScalarGridSpec(
        num_scalar_prefetch=1,
        grid=(t // tm,),
        in_specs=[pl.BlockSpec((None, 1, 2 * tm), lambda i, holes: (i, 0, 0), memory_space=pltpu.SMEM),
                  pl.BlockSpec((tm, d), lambda i, holes: (i, 0))],
        out_specs=pl.BlockSpec(memory_space=pl.ANY),
        scratch_shapes=[pltpu.VMEM((8, d), h.dtype), pltpu.SemaphoreType.DMA(()), pltpu.SemaphoreType.DMA(())])
    return pl.pallas_call(
        functools.partial(_dispatch_kernel, tm=tm, n_holes=n_rows - 2 * t),
        grid_spec=grid_spec,
        out_shape=jax.ShapeDtypeStruct((n_rows, d), h.dtype),
        compiler_params=pltpu.CompilerParams(dimension_semantics=("arbitrary",), has_side_effects=True),
        name="moe_dispatch",
    )(holes, slots.reshape(t // tm, 1, 2 * tm), h)


def _experts_kernel(te_ref, xs_ref, wg_ref, wu_ref, wd_ref, ys_ref, xb_ref, acc_ref):
    del te_ref
    f = pl.program_id(1)

    @pl.when(f == 0)
    def _():
        xb_ref[...] = xs_ref[...].astype(BF16)
        acc_ref[...] = jnp.zeros_like(acc_ref)

    xb = xb_ref[...]
    gate = jnp.dot(xb, wg_ref[...], preferred_element_type=F32)
    up = jnp.dot(xb, wu_ref[...], preferred_element_type=F32)
    hidden = (jax.nn.silu(gate) * up).astype(BF16)
    acc_ref[...] += jnp.dot(hidden, wd_ref[...], preferred_element_type=F32)

    @pl.when(f == pl.num_programs(1) - 1)
    def _():
        ys_ref[...] = acc_ref[...]


def _experts(xs, tile_expert, w_gate, w_up, w_down, tile, tf=1792):
    n_rows, d = xs.shape
    dff = w_gate.shape[2]
    grid_spec = pltpu.PrefetchScalarGridSpec(
        num_scalar_prefetch=1,
        grid=(n_rows // tile, dff // tf),
        in_specs=[pl.BlockSpec((tile, d), lambda i, f, te: (i, 0)),
                  pl.BlockSpec((None, d, tf), lambda i, f, te: (te[i], 0, f)),
                  pl.BlockSpec((None, d, tf), lambda i, f, te: (te[i], 0, f)),
                  pl.BlockSpec((None, tf, d), lambda i, f, te: (te[i], f, 0))],
        out_specs=pl.BlockSpec((tile, d), lambda i, f, te: (i, 0)),
        scratch_shapes=[pltpu.VMEM((tile, d), BF16), pltpu.VMEM((tile, d), F32)])
    return pl.pallas_call(
        _experts_kernel,
        grid_spec=grid_spec,
        out_shape=jax.ShapeDtypeStruct((n_rows, d), F32),
        compiler_params=_cparams("arbitrary", "arbitrary"),
        name="moe_experts",
    )(tile_expert, xs, w_gate.astype(BF16), w_up.astype(BF16), w_down.astype(BF16))


def _combine_kernel(slots_ref, x_ref, gates_ref, ys_hbm, out_ref, buf_ref, sem, *, tm):
    def issue(i, carry):
        for k in range(2):
            _row_copy(ys_hbm, slots_ref[0, 2 * i + k], buf_ref.at[k], i, sem).start()
        return carry

    lax.fori_loop(0, tm, issue, 0, unroll=4)
    for k in range(2):
        pltpu.make_async_copy(ys_hbm.at[pl.ds(0, tm)], buf_ref.at[k], sem).wait()
    gates = gates_ref[...]
    out_ref[...] = x_ref[...] + gates[:, 0:1] * buf_ref[0] + gates[:, 1:2] * buf_ref[1]


def _combine(x, ys, slots, gates, tm=256):
    t, d = x.shape
    tok = pl.BlockSpec((tm, d), lambda i: (i, 0))
    return pl.pallas_call(
        functools.partial(_combine_kernel, tm=tm),
        grid=(t // tm,),
        in_specs=[pl.BlockSpec((None, 1, 2 * tm), lambda i: (i, 0, 0), memory_space=pltpu.SMEM),
                  tok, pl.BlockSpec((tm, 2), lambda i: (i, 0)), pl.BlockSpec(memory_space=pl.ANY)],
        out_specs=tok,
        out_shape=jax.ShapeDtypeStruct((t, d), F32),
        scratch_shapes=[pltpu.VMEM((2, tm, d), F32), pltpu.SemaphoreType.DMA(())],
        compiler_params=_cparams("arbitrary"),
        name="moe_combine",
    )(slots.reshape(t // tm, 1, 2 * tm), x, gates, ys)


def _moe(x, gain, w_router, w_gate, w_up, w_down, tile=512):
    n_experts = w_gate.shape[0]
    h, comb, rank, counts = _router(x, gain, w_router)
    slots, gates, tile_expert, holes, n_tiles = _route_plan(comb, rank, counts, n_experts, tile)
    xs = _dispatch(h, slots, holes, n_tiles * tile)
    ys = _experts(xs, tile_expert, w_gate, w_up, w_down, tile)
    return _combine(x, ys, slots, gates)


def _lane_chunks(width):
    return [slice(c * LANES, (c + 1) * LANES) for c in range(width // LANES)]


def _store_dilated(dst_ref, stage_ref, value, dil):
    rows, width = value.shape
    if dil == 1:
        dst_ref[...] = value.astype(dst_ref.dtype)
        return
    for c, ls in enumerate(_lane_chunks(width)):
        stage_ref[c] = value[:, ls]
    for rho in range(dil):
        for c in range(width // LANES):
            dst_ref[:, pl.ds(rho * width + c * LANES, LANES)] = (
                stage_ref.at[c][pl.ds(rho, rows // dil, stride=dil), :].astype(dst_ref.dtype))


def _load_dilated(src_ref, stage_ref, base, rows, width, dil):
    if dil == 1:
        return src_ref[...].astype(F32)
    for rho in range(dil):
        for c in range(width // LANES):
            stage_ref.at[base + c][pl.ds(rho, rows // dil, stride=dil), :] = (
                src_ref[:, pl.ds(rho * width + c * LANES, LANES)].astype(F32))
    return jnp.concatenate([stage_ref[base + c] for c in range(width // LANES)], 1)


def _attn_in_kernel(x_ref, gain_ref, w_ref, hg_ref, e_ref, et_ref, *rest, dils):
    outs, (h_ref, stage_ref) = rest[:len(dils)], rest[len(dils):]
    g = pl.program_id(1)

    @pl.when(g == 0)
    def _():
        h_ref[...] = _rms(x_ref[...], gain_ref[...]).astype(BF16)

    n_qk = 2 * GROUP_WIDTH
    half = h_ref.shape[0] // 2
    ys = [jnp.dot(h_ref[pl.ds(r * half, half), :], w_ref[...], preferred_element_type=F32) for r in range(2)]
    sq = [_split_dot(yy[:, :n_qk] * yy[:, :n_qk], e_ref[...]) for yy in ys]
    ms = [_split_dot(x, et_ref[...]) * (1.0 / HEAD) for x in sq]
    qk = [yy[:, :n_qk] * lax.rsqrt(m + RMS_EPS) * hg_ref[...] for yy, m in zip(ys, ms)]
    y = jnp.concatenate([jnp.concatenate([a, yy[:, n_qk:]], 1) for a, yy in zip(qk, ys)], 0)
    for gi, dil in enumerate(dils):
        @pl.when(g == gi)
        def _(gi=gi, dil=dil):
            _store_dilated(outs[gi], stage_ref, y, dil)


def _attn_in(x, gain, w_in, q_gain, k_gain, tm=512):
    t, d = x.shape
    n_groups = q_gain.shape[0]
    gw = 3 * GROUP_WIDTH
    dils = tuple(dil for _, dil in ATTN_GROUPS)
    w = jnp.transpose(w_in.reshape(d, 3, n_groups, GROUP_WIDTH), (2, 0, 1, 3)).reshape(n_groups, d, gw).astype(BF16)
    head_gain = jnp.concatenate([jnp.tile(q_gain, (1, GROUP_HEADS)), jnp.tile(k_gain, (1, GROUP_HEADS))],
                                1).reshape(n_groups, 1, 2 * GROUP_WIDTH)
    e, et = _head_indicator(2 * GROUP_WIDTH)
    return pl.pallas_call(
        functools.partial(_attn_in_kernel, dils=dils),
        grid=(t // tm, n_groups),
        in_specs=[pl.BlockSpec((tm, d), lambda i, g: (i, 0)),
                  _const_spec((1, d)),
                  pl.BlockSpec((None, d, gw), lambda i, g: (g, 0, 0)),
                  pl.BlockSpec((None, 1, 2 * GROUP_WIDTH), lambda i, g: (g, 0, 0)),
                  _const_spec(e.shape), _const_spec(et.shape)],
        out_specs=[pl.BlockSpec((tm // dil, dil * gw), lambda i, g: (i, 0)) for dil in dils],
        out_shape=[jax.ShapeDtypeStruct((t // dil, dil * gw), BF16) for dil in dils],
        scratch_shapes=[pltpu.VMEM((tm, d), BF16), pltpu.VMEM((gw // LANES, tm, LANES), F32)],
        compiler_params=_cparams("arbitrary", "arbitrary"),
        name="attn_in",
    )(x, gain.reshape(1, d), w, head_gain, e, et)


def _t5_bucket_np(rel):
    nb = REL_BUCKETS // 2
    max_exact = nb // 2
    n = np.abs(rel)
    large = max_exact + (np.log(np.maximum(n, 1).astype(np.float32) / max_exact)
                         / math.log(REL_MAX_DIST / max_exact) * (nb - max_exact)).astype(np.int32)
    large = np.minimum(large, nb - 1)
    return np.where(rel > 0, nb, 0) + np.where(n < max_exact, n, large)


def _bias_table(rel_bias, group, window, dil):
    side = window // (2 * dil)
    assert side == KV_HALO
    buckets = _t5_bucket_np(dil * np.arange(-side, side + 1))
    table = rel_bias.reshape(REL_BUCKETS, -1, GROUP_HEADS)[buckets, group].T
    full = jnp.full((8, Q_BLOCK + 2 * KV_HALO), NEG_INF, F32)
    return full.at[:GROUP_HEADS, :2 * side + 1].set(table)


def _attn_kernel(q_ref, kp_ref, kc_ref, kn_ref, vp_ref, vc_ref, vn_ref, table_ref, o_ref, lse_ref, bias_ref,
                 *, regions, dil):
    tp, sp, ss = regions
    nk = Q_BLOCK + 2 * KV_HALO

    @pl.when((pl.program_id(0) == 0) & (pl.program_id(1) == 0))
    def _():
        for h in range(GROUP_HEADS):
            row = jnp.broadcast_to(table_ref[h:h + 1, :], (Q_BLOCK, nk))
            bias_ref[h] = pltpu.roll(row, 0, 1, stride=1, stride_axis=0)

    lane = lax.broadcasted_iota(jnp.int32, (Q_BLOCK, LANES), 1)
    mine = [(lane < HEAD), (lane >= HEAD)]
    scale = HEAD ** -0.5
    q_all = q_ref[...]
    k_all = jnp.concatenate([kp_ref[...], kc_ref[...], kn_ref[...]], 0)
    v_all = jnp.concatenate([vp_ref[...], vc_ref[...], vn_ref[...]], 0)
    lanes_of = lambda pair: slice(pair * LANES, (pair + 1) * LANES)

    units = [(blk, pair, sub) for blk in range(Q_STEP // Q_BLOCK) for pair in range(GROUP_HEADS // 2)
             for sub in range(2)]
    valid = []
    for blk in range(Q_STEP // Q_BLOCK):
        row0 = pl.program_id(1) * Q_STEP + blk * Q_BLOCK
        seq = _seq_len_at(row0, tp // dil, sp // dil, ss // dil)
        seq_start = row0 - lax.rem(row0, seq)
        key_row = row0 - KV_HALO + lax.broadcasted_iota(jnp.int32, (1, nk), 1)
        valid.append((key_row >= seq_start) & (key_row < seq_start + seq))
    s_all = []
    for blk, pair, sub in units:
        qp = q_all[blk * Q_BLOCK:(blk + 1) * Q_BLOCK, lanes_of(pair)]
        qm = jnp.where(mine[sub], qp, jnp.zeros_like(qp))
        keys = k_all[blk * Q_BLOCK:blk * Q_BLOCK + nk, lanes_of(pair)]
        s = lax.dot_general(qm, keys, (_NT, ((), ())), preferred_element_type=F32)
        s_all.append(jnp.where(valid[blk], s * scale + bias_ref[2 * pair + sub], NEG_INF))
    m_all = [jnp.max(s, axis=-1, keepdims=True) for s in s_all]
    p_all = [jnp.exp(s - m) for s, m in zip(s_all, m_all)]
    den_all = [jnp.sum(p, axis=-1, keepdims=True) for p in p_all]
    o_all = [jnp.dot((p / den).astype(v_all.dtype), v_all[blk * Q_BLOCK:blk * Q_BLOCK + nk, lanes_of(pair)],
                     preferred_element_type=F32)
             for p, den, (blk, pair, _) in zip(p_all, den_all, units)]
    lse_all = [m + jnp.log(den) for m, den in zip(m_all, den_all)]
    for i in range(0, len(units), 2):
        blk, pair, _ = units[i]
        rows = pl.ds(blk * Q_BLOCK, Q_BLOCK)
        o_ref[rows, lanes_of(pair)] = jnp.where(mine[0], o_all[i], o_all[i + 1]).astype(o_ref.dtype)
        lse_ref[rows, lanes_of(pair)] = jnp.where(mine[0], lse_all[i], lse_all[i + 1])


def _attn_group(qkv_d, rel_bias, group, regions):
    window, dil = ATTN_GROUPS[group]
    rows = qkv_d.shape[0]
    parts = 3
    table = _bias_table(rel_bias, group, window, dil)
    halo_per_step = Q_STEP // KV_HALO
    last_halo = rows // KV_HALO - 1

    def cur(part):
        return pl.BlockSpec((Q_STEP, GROUP_WIDTH), lambda rho, m: (m, rho * parts + part))

    def prev(part):
        return pl.BlockSpec((KV_HALO, GROUP_WIDTH),
                            lambda rho, m: (jnp.maximum(m * halo_per_step - 1, 0), rho * parts + part))

    def nxt(part):
        return pl.BlockSpec((KV_HALO, GROUP_WIDTH),
                            lambda rho, m: (jnp.minimum((m + 1) * halo_per_step, last_halo), rho * parts + part))

    out_spec = pl.BlockSpec((Q_STEP, GROUP_WIDTH), lambda rho, m: (m, rho))
    return pl.pallas_call(
        functools.partial(_attn_kernel, regions=regions, dil=dil),
        grid=(dil, rows // Q_STEP),
        in_specs=[cur(0), prev(1), cur(1), nxt(1), prev(2), cur(2), nxt(2), _const_spec(table.shape)],
        out_specs=[out_spec, out_spec],
        out_shape=[jax.ShapeDtypeStruct((rows, dil * GROUP_WIDTH), BF16),
                   jax.ShapeDtypeStruct((rows, dil * GROUP_WIDTH), F32)],
        scratch_shapes=[pltpu.VMEM((GROUP_HEADS, Q_BLOCK, Q_BLOCK + 2 * KV_HALO), F32)],
        compiler_params=_cparams("arbitrary", "arbitrary"),
        name=f"attn_g{group}",
    )(qkv_d, qkv_d, qkv_d, qkv_d, qkv_d, qkv_d, qkv_d, table)


def _attn_out_kernel(o0, o1, o2, l0, l1, l2, x_ref, wo_ref, out_ref, stage_ref, *, dils):
    tm = x_ref.shape[0]
    chunks = GROUP_WIDTH // LANES
    os_ = [_load_dilated(r, stage_ref, 2 * g * chunks, tm, GROUP_WIDTH, dils[g]) for g, r in enumerate((o0, o1, o2))]
    lses = [_load_dilated(r, stage_ref, (2 * g + 1) * chunks, tm, GROUP_WIDTH, dils[g])
            for g, r in enumerate((l0, l1, l2))]
    mx = jnp.maximum(jnp.maximum(lses[0], lses[1]), lses[2])
    ws = [jnp.exp(l - mx) for l in lses]
    tot = ws[0] + ws[1] + ws[2]
    acc = x_ref[...]
    for g in range(len(dils)):
        scaled = (os_[g] * (ws[g] / tot)).astype(BF16)
        acc = acc + jnp.dot(scaled, wo_ref[g], preferred_element_type=F32)
    out_ref[...] = acc


def _attn_out(os_, lses, x, w_o, tm=512):
    t, d = x.shape
    n_groups = len(os_)
    dils = tuple(dil for _, dil in ATTN_GROUPS)
    grp = [pl.BlockSpec((tm // dil, dil * GROUP_WIDTH), lambda i: (i, 0)) for dil in dils]
    tok = pl.BlockSpec((tm, d), lambda i: (i, 0))
    wo = w_o.reshape(n_groups, GROUP_WIDTH, d).astype(BF16)
    return pl.pallas_call(
        functools.partial(_attn_out_kernel, dils=dils),
        grid=(t // tm,),
        in_specs=grp + grp + [tok, _const_spec(wo.shape)],
        out_specs=tok,
        out_shape=jax.ShapeDtypeStruct((t, d), F32),
        scratch_shapes=[pltpu.VMEM((2 * n_groups * (GROUP_WIDTH // LANES), tm, LANES), F32)],
        compiler_params=_cparams("arbitrary"),
        name="attn_out",
    )(*os_, *lses, x, wo)


def kernel(x_prompt, x_sample, norm_mix, norm_ffn, rwkv_mu, rwkv_w_rkv, rwkv_w0, rwkv_w1, rwkv_w2, rwkv_a0, rwkv_a1, rwkv_a2, rwkv_g1, rwkv_g2, rwkv_k_k, rwkv_k_a, rwkv_r_k, rwkv_ln_g, rwkv_ln_b, rwkv_w_o, attn_w_in, attn_q_gain, attn_k_gain, attn_w_o, rel_bias, ffn_w_gate, ffn_w_up, ffn_w_down, moe_router, moe_w_gate, moe_w_up, moe_w_down):
    d = x_prompt.shape[-1]
    tp = x_prompt.shape[0] * x_prompt.shape[1]
    regions = (tp, x_prompt.shape[1], x_sample.shape[1])
    x = jnp.concatenate([x_prompt.reshape(-1, d), x_sample.reshape(-1, d)], 0)
    depth = norm_mix.shape[0]
    for i in range(depth):
        j = i // 2
        if i % 2 == 0:
            r, v, kk, g, bonus, logw, kdir, bdir = _rwkv_pre(
                x, regions, norm_mix[i], rwkv_mu[j], rwkv_w_rkv[j], rwkv_w0[j], rwkv_w1[j], rwkv_w2[j],
                rwkv_a0[j], rwkv_a1[j], rwkv_a2[j], rwkv_g1[j], rwkv_g2[j], rwkv_k_k[j], rwkv_k_a[j], rwkv_r_k[j])
            yf = _wkv(r, v, kk, logw, kdir, bdir, regions, reverse=False)
            yb = _wkv(r, v, kk, logw, kdir, bdir, regions, reverse=True)
            x = _rwkv_post(yf, yb, bonus, g, x, rwkv_ln_g[j], rwkv_ln_b[j], rwkv_w_o[j])
            x = _ffn(x, norm_ffn[i], ffn_w_gate[j], ffn_w_up[j], ffn_w_down[j])
        else:
            qkv = _attn_in(x, norm_mix[i], attn_w_in[j], attn_q_gain[j], attn_k_gain[j])
            outs = [_attn_group(qkv[g], rel_bias, g, regions) for g in range(len(ATTN_GROUPS))]
            x = _attn_out([o for o, _ in outs], [l for _, l in outs], x, attn_w_o[j])
            x = _moe(x, norm_ffn[i], moe_router[j], moe_w_gate[j], moe_w_up[j], moe_w_down[j])
    y_prompt = x[:tp].reshape(x_prompt.shape)
    y_sample = x[tp:].reshape(x_sample.shape)
    return (y_prompt, y_sample)
```

```python
import functools
import math

import numpy as np
import jax
import jax.numpy as jnp
from jax import lax
from jax.experimental import pallas as pl
from jax.experimental.pallas import tpu as pltpu

F32 = jnp.float32
BF16 = jnp.bfloat16

LANES = 128
VMEM_LIMIT_BYTES = 56 * 1024 * 1024

HEAD = 64
RMS_EPS = 1e-6
GN_EPS = 64e-5
ATTN_GROUPS = ((128, 1), (512, 4), (2048, 16))
GROUP_HEADS = 6
GROUP_WIDTH = GROUP_HEADS * HEAD
REL_BUCKETS = 32
REL_MAX_DIST = 1024
Q_BLOCK = 128
Q_STEP = 512
KV_HALO = 64
SCAN_CHUNK = 64
NEG_INF = -1e30


def _cparams(*sem):
    return pltpu.CompilerParams(dimension_semantics=sem, vmem_limit_bytes=VMEM_LIMIT_BYTES)


def _const_spec(shape):
    nd = len(shape)
    return pl.BlockSpec(shape, lambda *_: (0,) * nd, pipeline_mode=pl.Buffered(1))


def _split_dot(x, w):
    hi = x.astype(BF16)
    lo = (x - hi.astype(F32)).astype(BF16)
    return (jnp.dot(hi, w, preferred_element_type=F32)
            + jnp.dot(lo, w, preferred_element_type=F32))


def _head_sum(x, e_ref, et_ref):
    return _split_dot(_split_dot(x, e_ref[...]), et_ref[...])


def _rms(x, gain):
    return x * lax.rsqrt(jnp.mean(x * x, axis=-1, keepdims=True) + RMS_EPS) * gain


def _seq_len_at(row0, tp, sp, ss):
    return jnp.where(row0 < tp, sp, ss)


def _head_indicator(width):
    e = np.zeros((width, LANES), np.float32)
    e[np.arange(width), np.arange(width) // HEAD] = 1.0
    return jnp.asarray(e, BF16), jnp.asarray(e.T, BF16)


def _rwkv_pre_kernel(x_ref, xp_ref, xn_ref, gain_ref, mu_ref, wrkv_ref, wl1_ref, w2_ref, a2_ref,
                     g2_ref, w0_ref, a0_ref, kk_gain_ref, ka_ref, rk_ref, e_ref, et_ref,
                     r_out, v_out, kk_out, g_out, bonus_out, logw_out, kdir_out, bdir_out,
                     *, tm, regions):
    tp, sp, ss = regions
    d = x_ref.shape[-1]
    row0 = pl.program_id(0) * tm
    seq = _seq_len_at(row0, tp, sp, ss)
    at_start = lax.rem(row0, seq) == 0
    at_end = lax.rem(row0 + tm, seq) == 0

    gain = gain_ref[...]
    h = _rms(x_ref[...], gain)
    hp = _rms(xp_ref[...], gain)[7:8, :] * jnp.where(at_start, 0.0, 1.0)
    hn = _rms(xn_ref[...], gain)[0:1, :] * jnp.where(at_end, 0.0, 1.0)
    rows = lax.broadcasted_iota(jnp.int32, (tm, d), 0)
    h_prev = jnp.where(rows == 0, hp, pltpu.roll(h, 1, 0))
    h_next = jnp.where(rows == tm - 1, hn, pltpu.roll(h, tm - 1, 0))
    xx = 0.5 * (h_prev + h_next) - h

    def half_tile(rs):
        hh, xh = h[rs], xx[rs]
        mixed = lambda n: (hh + xh * mu_ref[n:n + 1, :]).astype(BF16)
        r = jnp.dot(mixed(0), wrkv_ref[0], preferred_element_type=F32)
        k = jnp.dot(mixed(1), wrkv_ref[1], preferred_element_type=F32)
        v = jnp.dot(mixed(2), wrkv_ref[2], preferred_element_type=F32)
        r_out[rs, :] = r
        v_out[rs, :] = v
        yield
        t_w = jnp.tanh(jnp.dot(mixed(3), wl1_ref[0], preferred_element_type=F32))
        t_a = jnp.dot(mixed(4), wl1_ref[1], preferred_element_type=F32)
        t_g = jax.nn.sigmoid(jnp.dot(mixed(5), wl1_ref[2], preferred_element_type=F32))
        yield
        lw = w0_ref[...] + jnp.dot(t_w.astype(BF16), w2_ref[...], preferred_element_type=F32)
        a = jax.nn.sigmoid(a0_ref[...] + jnp.dot(t_a.astype(BF16), a2_ref[...], preferred_element_type=F32))
        g_out[rs, :] = jnp.dot(t_g.astype(BF16), g2_ref[...], preferred_element_type=F32)
        logw = -math.exp(-0.5) * jax.nn.sigmoid(lw)
        for z in range(2):
            logw_out[z, rs, :] = logw[:, z * d:(z + 1) * d]
        yield
        kk = k * kk_gain_ref[...]
        kk_sq = _split_dot(kk * kk, e_ref[...])
        rk_sum = _split_dot(r * k * rk_ref[...], e_ref[...])
        yield
        kk = kk * lax.rsqrt(jnp.maximum(_split_dot(kk_sq, et_ref[...]), 1e-12))
        bonus_out[rs, :] = _split_dot(rk_sum, et_ref[...]) * v
        kk_out[rs, :] = kk
        ka = ka_ref[...]
        for z in range(2):
            a_z = a[:, z * d:(z + 1) * d]
            kdir_out[z, rs, :] = k * (1.0 + (a_z - 1.0) * ka)
            bdir_out[z, rs, :] = kk * a_z

    live = [half_tile(slice(i * (tm // 2), (i + 1) * (tm // 2))) for i in range(2)]
    while live:
        live = [gen for gen in live if next(gen, StopIteration) is not StopIteration]


def _block_diag2(m):
    z = jnp.zeros_like(m[0])
    return jnp.concatenate([jnp.concatenate([m[0], z], 1), jnp.concatenate([z, m[1]], 1)], 0)


def _rwkv_pre(x, regions, gain, mu, w_rkv, w0, w1, w2, a0, a1, a2, g1, g2, k_k, k_a, r_k, tm=256):
    t, d = x.shape
    e, et = _head_indicator(d)
    wl1 = jnp.stack([jnp.concatenate([w1[0], w1[1]], 1), jnp.concatenate([a1[0], a1[1]], 1), g1]).astype(BF16)
    w2c = _block_diag2(w2).astype(BF16)
    a2c = _block_diag2(a2).astype(BF16)
    row = lambda p: p.reshape(1, -1).astype(F32)
    tok = pl.BlockSpec((tm, d), lambda i: (i, 0))
    halo_blocks = t // 8
    prev = pl.BlockSpec((8, d), lambda i: (jnp.maximum(i * (tm // 8) - 1, 0), 0))
    nxt = pl.BlockSpec((8, d), lambda i: (jnp.minimum((i + 1) * (tm // 8), halo_blocks - 1), 0))
    dir_tok = pl.BlockSpec((2, tm, d), lambda i: (0, i, 0))
    consts = [row(gain), mu.astype(F32), w_rkv.astype(BF16), wl1, w2c, a2c, g2.astype(BF16),
              row(w0), row(a0), row(k_k), row(k_a), row(r_k), e, et]
    tok_shape = jax.ShapeDtypeStruct((t, d), F32)
    dir_shape = jax.ShapeDtypeStruct((2, t, d), F32)
    return pl.pallas_call(
        functools.partial(_rwkv_pre_kernel, tm=tm, regions=regions),
        grid=(t // tm,),
        in_specs=[tok, prev, nxt] + [_const_spec(c.shape) for c in consts],
        out_specs=[tok] * 5 + [dir_tok] * 3,
        out_shape=[tok_shape] * 5 + [dir_shape] * 3,
        compiler_params=_cparams("arbitrary"),
        name="rwkv_pre",
    )(x, x, x, *consts)


_NN = ((1,), (0,))
_NT = ((1,), (1,))
_TN = ((0,), (0,))


def _mm(a, b, dims=_NN):
    return lax.dot_general(a.astype(BF16), b.astype(BF16), (dims, ((), ())), preferred_element_type=F32)


def _cumsum_rows(x, reverse):
    n = x.shape[0]
    row = lax.broadcasted_iota(jnp.int32, x.shape, 0)
    s = 1
    while s < n:
        if reverse:
            x = x + jnp.where(row < n - s, pltpu.roll(x, n - s, 0), 0.0)
        else:
            x = x + jnp.where(row >= s, pltpu.roll(x, s, 0), 0.0)
        s *= 2
    return x


def _wkv_kernel(r_ref, v_ref, kk_ref, logw_ref, kdir_ref, bdir_ref, y_ref, state_ref,
                *, rb, regions, reverse, group_chunks=2):
    tp, sp, ss = regions
    c = SCAN_CHUNK
    npairs = r_ref.shape[-1] // LANES
    nsteps = pl.num_programs(1)
    step = pl.program_id(1)
    blk = (nsteps - 1 - step) if reverse else step
    row0 = blk * rb
    seq = _seq_len_at(row0, tp, sp, ss)
    first = (lax.rem(row0 + rb, seq) == 0) if reverse else (lax.rem(row0, seq) == 0)

    @pl.when(first)
    def _():
        state_ref[...] = jnp.zeros_like(state_ref)

    ti2 = lax.broadcasted_iota(jnp.int32, (c, 2 * c), 0)
    tj2 = lax.broadcasted_iota(jnp.int32, (c, 2 * c), 1) & (c - 1)
    ti4 = lax.broadcasted_iota(jnp.int32, (c, 4 * c), 0)
    tj4 = lax.broadcasted_iota(jnp.int32, (c, 4 * c), 1) & (c - 1)
    incl4 = (tj4 >= ti4) if reverse else (tj4 <= ti4)
    strict2 = (tj2 > ti2) if reverse else (tj2 < ti2)
    lane = lax.broadcasted_iota(jnp.int32, (c, LANES), 1)
    head0 = lane < HEAD
    rr = lax.broadcasted_iota(jnp.int32, (LANES, LANES), 0)
    cc = lax.broadcasted_iota(jnp.int32, (LANES, LANES), 1)
    same_head = (rr < HEAD) == (cc < HEAD)

    col4 = lax.broadcasted_iota(jnp.int32, (c, 4 * c), 1)
    eye4 = ((col4 & (c - 1)) == ti4).astype(F32)

    def block_diag4(m):
        return jnp.concatenate([jnp.where((col4 >= h * c) & (col4 < (h + 1) * c), m, 0.0) for h in range(4)], 0)

    def stack_heads(m):
        return jnp.concatenate([jnp.where(head0, m, 0.0), jnp.where(head0, 0.0, m)], 0)

    order = list(range(rb // c - 1, -1, -1) if reverse else range(rb // c))
    each = lambda fn, *lists: [fn(*xs) for xs in zip(*lists)]

    def prepare(chunks, out):
        units = [(pl.ds(ci * c, c), pl.ds(p * LANES, LANES)) for ci in chunks for p in range(npairs)]
        lw = [logw_ref[u] for u in units]
        cum = each(lambda x: _cumsum_rows(x, reverse), lw)
        total = each(lambda x: jnp.sum(x, axis=0, keepdims=True), lw)
        lhs = [jnp.concatenate([r_ref[u] * jnp.exp(cm), kk_ref[u] * jnp.exp(cm - x)], 0).astype(BF16)
               for u, cm, x in zip(units, cum, lw)]
        rhs = []
        kb_tail = []
        for u, cm, tt in zip(units, cum, total):
            kd, bd = kdir_ref[u], bdir_ref[u]
            inv = jnp.exp(-cm)
            tail = jnp.exp(tt - cm)
            rhs.append(jnp.concatenate([stack_heads(kd * inv), stack_heads(bd * inv)], 0).astype(BF16))
            kb_tail.append(jnp.concatenate([kd * tail, -(bd * tail)], 0).astype(BF16))
        yield
        q = each(lambda a, b: _mm(a, b, _NT), lhs, rhs)
        a_out = [jnp.where(incl4, x[0:c], 0.0).astype(BF16) for x in q]
        a_kk = [jnp.where(strict2, x[c:2 * c, 0:2 * c], 0.0).astype(BF16) for x in q]
        low = [jnp.where(strict2, x[c:2 * c, 2 * c:4 * c], 0.0) for x in q]
        pw = [-jnp.concatenate([low[i], low[i + 1]], 1) for i in range(0, len(units), 2)]
        inv_cat = [eye4 + x for x in pw]
        yield
        pw = each(lambda x: _mm(x, block_diag4(x)), pw)
        v_all = [v_ref[u] for u in units]
        v_bd = [stack_heads(x).astype(BF16) for x in v_all]
        akv = each(_mm, a_kk, v_bd)
        for _ in range(int(math.log2(c)) - 2):
            yield
            both = each(lambda t, x: _mm(jnp.concatenate([t, x], 0), block_diag4(x)), inv_cat, pw)
            inv_cat = each(lambda t, b: t + b[0:c], inv_cat, both)
            pw = [b[c:2 * c] for b in both]
        yield
        inv_cat = each(lambda t, x: t + _mm(t, block_diag4(x)), inv_cat, pw)
        inv_l = [stack_heads(t[:, h * 2 * c:(h + 1) * 2 * c]).astype(BF16) for t in inv_cat for h in range(2)]
        out.update(units=units, lhs=lhs, kb_tail=kb_tail, a_out=a_out, akv=akv, inv_l=inv_l, v_all=v_all,
                   v_bd=v_bd, decay=[jnp.exp(x) for x in total])

    def advance(group, k, states):
        pre = {name: vals[k * npairs:(k + 1) * npairs] for name, vals in group.items()}
        sk = each(lambda a, st: _mm(a, st, _NT), pre["lhs"], states)
        yield
        u_bd = each(lambda t, s_, kv: _mm(t, stack_heads(s_[c:2 * c] + kv)), pre["inv_l"], sk, pre["akv"])
        yield
        y = each(lambda s_, a, vb, ub: s_[0:c] + _mm(a, jnp.concatenate([vb, (-ub).astype(BF16)], 0)),
                 sk, pre["a_out"], pre["v_bd"], u_bd)
        for u, yy in zip(pre["units"], y):
            y_ref[u] = yy
        upd = each(lambda vv, ub, kb: _mm(jnp.concatenate([vv, ub[0:c] + ub[c:2 * c]], 0), kb, _TN),
                   pre["v_all"], u_bd, pre["kb_tail"])
        states[:] = each(lambda st, dc, up: st * dc + jnp.where(same_head, up, 0.0), states, pre["decay"], upd)

    def run(*gens):
        live = [g for g in gens if g is not None]
        while live:
            live = [g for g in live if next(g, StopIteration) is not StopIteration]

    states = [state_ref[p] for p in range(npairs)]
    groups = [order[i:i + group_chunks] for i in range(0, len(order), group_chunks)]
    prepared = {}
    run(prepare(groups[0], prepared))
    for gi, chunks in enumerate(groups):
        nxt = {}
        following = prepare(groups[gi + 1], nxt) if gi + 1 < len(groups) else None

        def advance_all(prepared=prepared, chunks=chunks):
            for k in range(len(chunks)):
                yield from advance(prepared, k, states)
                yield

        run(following, advance_all())
        prepared = nxt
    for p in range(npairs):
        state_ref[p] = states[p]


def _wkv(r, v, kk, logw, kdir, bdir, regions, reverse, rb=512, width=1024):
    t, d = r.shape
    z = 1 if reverse else 0
    nsteps = t // rb

    def row_block(s):
        return (nsteps - 1 - s) if reverse else s

    tok = pl.BlockSpec((rb, width), lambda j, s: (row_block(s), j))
    dir_tok = pl.BlockSpec((None, rb, width), lambda j, s: (z, row_block(s), j))
    return pl.pallas_call(
        functools.partial(_wkv_kernel, rb=rb, regions=regions, reverse=reverse),
        grid=(d // width, nsteps),
        in_specs=[tok, tok, tok, dir_tok, dir_tok, dir_tok],
        out_specs=tok,
        out_shape=jax.ShapeDtypeStruct((t, d), F32),
        scratch_shapes=[pltpu.VMEM((width // LANES, LANES, LANES), F32)],
        compiler_params=_cparams("arbitrary", "arbitrary"),
        name="wkv_bwd" if reverse else "wkv_fwd",
    )(r, v, kk, logw, kdir, bdir)


def _rwkv_post_kernel(yf_ref, yb_ref, bonus_ref, g_ref, x_ref, lng_ref, lnb_ref, wo_ref, e_ref, et_ref, out_ref):
    y = yf_ref[...] + yb_ref[...]
    mean = _head_sum(y, e_ref, et_ref) * (1.0 / HEAD)
    dlt = y - mean
    var = _head_sum(dlt * dlt, e_ref, et_ref) * (1.0 / HEAD)
    yn = dlt * lax.rsqrt(var + GN_EPS) * lng_ref[...] + lnb_ref[...]
    mixed = ((yn + bonus_ref[...]) * g_ref[...]).astype(BF16)
    out_ref[...] = x_ref[...] + jnp.dot(mixed, wo_ref[...], preferred_element_type=F32)


def _rwkv_post(yf, yb, bonus, g, x, ln_g, ln_b, w_o, tm=256):
    t, d = x.shape
    e, et = _head_indicator(d)
    tok = pl.BlockSpec((tm, d), lambda i: (i, 0))
    consts = [ln_g.reshape(1, d), ln_b.reshape(1, d), w_o.astype(BF16), e, et]
    return pl.pallas_call(
        _rwkv_post_kernel,
        grid=(t // tm,),
        in_specs=[tok] * 5 + [_const_spec(c.shape) for c in consts],
        out_specs=tok,
        out_shape=jax.ShapeDtypeStruct((t, d), F32),
        compiler_params=_cparams("arbitrary"),
        name="rwkv_post",
    )(yf, yb, bonus, g, x, *consts)


def _ffn_kernel(x_ref, gain_ref, wg_ref, wu_ref, wd_ref, out_ref, h_ref, acc_ref):
    f = pl.program_id(1)

    @pl.when(f == 0)
    def _():
        h_ref[...] = _rms(x_ref[...], gain_ref[...]).astype(BF16)
        acc_ref[...] = jnp.zeros_like(acc_ref)

    h = h_ref[...]
    gate = jnp.dot(h, wg_ref[...], preferred_element_type=F32)
    up = jnp.dot(h, wu_ref[...], preferred_element_type=F32)
    hidden = (jax.nn.silu(gate) * up).astype(BF16)
    acc_ref[...] += jnp.dot(hidden, wd_ref[...], preferred_element_type=F32)

    @pl.when(f == pl.num_programs(1) - 1)
    def _():
        out_ref[...] = x_ref[...] + acc_ref[...]


def _ffn(x, gain, w_gate, w_up, w_down, tm=512, tf=2816):
    t, d = x.shape
    dff = w_gate.shape[1]
    return pl.pallas_call(
        _ffn_kernel,
        grid=(t // tm, dff // tf),
        in_specs=[pl.BlockSpec((tm, d), lambda i, f: (i, 0)),
                  _const_spec((1, d)),
                  pl.BlockSpec((d, tf), lambda i, f: (0, f)),
                  pl.BlockSpec((d, tf), lambda i, f: (0, f)),
                  pl.BlockSpec((tf, d), lambda i, f: (f, 0))],
        out_specs=pl.BlockSpec((tm, d), lambda i, f: (i, 0)),
        out_shape=jax.ShapeDtypeStruct((t, d), F32),
        scratch_shapes=[pltpu.VMEM((tm, d), BF16), pltpu.VMEM((tm, d), F32)],
        compiler_params=_cparams("arbitrary", "arbitrary"),
        name="ffn",
    )(x, gain.reshape(1, d), w_gate.astype(BF16), w_up.astype(BF16), w_down.astype(BF16))


def _router_kernel(x_ref, gain_ref, wr_ref, tri_ref, h_out, comb_out, rank_out, count_out, carry_ref,
                   *, n_experts):
    @pl.when(pl.program_id(0) == 0)
    def _():
        carry_ref[...] = jnp.zeros_like(carry_ref)

    h = _rms(x_ref[...], gain_ref[...])
    h_out[...] = h
    logits = jnp.dot(h.astype(BF16), wr_ref[...], preferred_element_type=F32)
    lane = lax.broadcasted_iota(jnp.int32, logits.shape, 1)
    logits = jnp.where(lane < n_experts, logits, -jnp.inf)
    m1 = jnp.max(logits, axis=-1, keepdims=True)
    i1 = jnp.min(jnp.where(logits == m1, lane, LANES), axis=-1, keepdims=True)
    rest = jnp.where(lane == i1, -jnp.inf, logits)
    m2 = jnp.max(rest, axis=-1, keepdims=True)
    i2 = jnp.min(jnp.where(rest == m2, lane, LANES), axis=-1, keepdims=True)
    e2 = jnp.exp(m2 - m1)
    g1 = 1.0 / (1.0 + e2)
    g2 = e2 / (1.0 + e2)
    comb_out[...] = jnp.where(lane == i1, g1, 0.0) + jnp.where(lane == i2, g2, 0.0)
    sel = jnp.where((lane == i1) | (lane == i2), 1.0, 0.0)
    earlier = carry_ref[0:1, :] + jnp.dot(tri_ref[...], sel.astype(BF16), preferred_element_type=F32)
    rank_out[...] = sel * (earlier + 1.0)
    carry_ref[...] = carry_ref[...] + jnp.sum(sel, axis=0, keepdims=True)
    count_out[...] = carry_ref[...]


def _router(x, gain, w_router, tm=512):
    t, d = x.shape
    n_experts = w_router.shape[1]
    wr = jnp.zeros((d, LANES), F32).at[:, :n_experts].set(w_router).astype(BF16)
    tri = jnp.asarray(np.tril(np.ones((tm, tm), np.float32), -1), BF16)
    tok = pl.BlockSpec((tm, d), lambda i: (i, 0))
    lanes = pl.BlockSpec((tm, LANES), lambda i: (i, 0))
    return pl.pallas_call(
        functools.partial(_router_kernel, n_experts=n_experts),
        grid=(t // tm,),
        in_specs=[tok, _const_spec((1, d)), _const_spec((d, LANES)), _const_spec((tm, tm))],
        out_specs=[tok, lanes, lanes, pl.BlockSpec((8, LANES), lambda i: (0, 0))],
        out_shape=[jax.ShapeDtypeStruct((t, d), F32), jax.ShapeDtypeStruct((t, LANES), F32),
                   jax.ShapeDtypeStruct((t, LANES), F32), jax.ShapeDtypeStruct((8, LANES), F32)],
        scratch_shapes=[pltpu.VMEM((8, LANES), F32)],
        compiler_params=_cparams("arbitrary"),
        name="moe_router",
    )(x, gain.reshape(1, d), wr, tri)


def _route_plan(comb, rank, counts, n_experts, tile):
    t = comb.shape[0]
    n_tiles = (2 * t) // tile + n_experts
    cnt = counts[0, :n_experts].astype(jnp.int32)
    padded = ((cnt + tile - 1) // tile) * tile
    ends = jnp.cumsum(padded)
    starts = jnp.zeros((LANES,), jnp.int32).at[:n_experts].set(ends - padded)
    slot = jnp.where(rank > 0, starts[None, :] + rank.astype(jnp.int32) - 1, -1)
    slot_hi = jnp.max(slot, axis=1)
    slot_lo = jnp.min(jnp.where(slot >= 0, slot, jnp.iinfo(jnp.int32).max), axis=1)
    gate_lo = jnp.sum(jnp.where(slot == slot_lo[:, None], comb, 0.0), axis=1)
    gate_hi = jnp.sum(jnp.where(slot == slot_hi[:, None], comb, 0.0), axis=1)
    slots = jnp.stack([slot_lo, slot_hi], 1)
    gates = jnp.stack([gate_lo, gate_hi], 1)
    tile_expert = jnp.sum(jnp.arange(n_tiles, dtype=jnp.int32)[:, None] * tile >= ends[None, :], axis=1)
    holes = jnp.concatenate([ends - padded + cnt, ends[-1:], ends, jnp.full((1,), n_tiles * tile, jnp.int32)])
    return slots, gates, jnp.minimum(tile_expert, n_experts - 1).astype(jnp.int32), holes.astype(jnp.int32), n_tiles


def _row_copy(src, src_row, dst, dst_row, sem):
    return pltpu.make_async_copy(src.at[pl.ds(src_row, 1)], dst.at[pl.ds(dst_row, 1)], sem)


def _dispatch_kernel(holes_ref, slots_ref, h_ref, xs_hbm, zero_ref, sem, zero_sem, *, tm, n_holes):
    n_ranges = holes_ref.shape[0] // 2

    @pl.when(pl.program_id(0) == 0)
    def _():
        zero_ref[...] = jnp.zeros_like(zero_ref)
        for e in range(n_ranges):
            def fill(row, carry):
                _row_copy(zero_ref, 0, xs_hbm, row, zero_sem).start()
                return carry
            lax.fori_loop(holes_ref[e], holes_ref[n_ranges + e], fill, 0)
        pltpu.make_async_copy(xs_hbm.at[pl.ds(0, n_holes)], xs_hbm.at[pl.ds(0, n_holes)], zero_sem).wait()

    def issue(i, carry):
        for k in range(2):
            _row_copy(h_ref, i, xs_hbm, slots_ref[0, 2 * i + k], sem).start()
        return carry

    lax.fori_loop(0, tm, issue, 0, unroll=4)
    for _ in range(2):
        pltpu.make_async_copy(h_ref, xs_hbm.at[pl.ds(0, tm)], sem).wait()


def _dispatch(h, slots, holes, n_rows, tm=512):
    t, d = h.shape
    grid_spec = pltpu.PrefetchScalarGridSpec(
        num_scalar_prefetch=1,
        grid=(t // tm,),
        in_specs=[pl.BlockSpec((None, 1, 2 * tm), lambda i, holes: (i, 0, 0), memory_space=pltpu.SMEM),
                  pl.BlockSpec((tm, d), lambda i, holes: (i, 0))],
        out_specs=pl.BlockSpec(memory_space=pl.ANY),
        scratch_shapes=[pltpu.VMEM((8, d), h.dtype), pltpu.SemaphoreType.DMA(()), pltpu.SemaphoreType.DMA(())])
    return pl.pallas_call(
        functools.partial(_dispatch_kernel, tm=tm, n_holes=n_rows - 2 * t),
        grid_spec=grid_spec,
        out_shape=jax.ShapeDtypeStruct((n_rows, d), h.dtype),
        compiler_params=pltpu.CompilerParams(dimension_semantics=("arbitrary",), has_side_effects=True),
        name="moe_dispatch",
    )(holes, slots.reshape(t // tm, 1, 2 * tm), h)


def _experts_kernel(te_ref, xs_ref, wg_ref, wu_ref, wd_ref, ys_ref, xb_ref, acc_ref):
    del te_ref
    f = pl.program_id(1)

    @pl.when(f == 0)
    def _():
        xb_ref[...] = xs_ref[...].astype(BF16)
        acc_ref[...] = jnp.zeros_like(acc_ref)

    xb = xb_ref[...]
    gate = jnp.dot(xb, wg_ref[...], preferred_element_type=F32)
    up = jnp.dot(xb, wu_ref[...], preferred_element_type=F32)
    hidden = (jax.nn.silu(gate) * up).astype(BF16)
    acc_ref[...] += jnp.dot(hidden, wd_ref[...], preferred_element_type=F32)

    @pl.when(f == pl.num_programs(1) - 1)
    def _():
        ys_ref[...] = acc_ref[...]


def _experts(xs, tile_expert, w_gate, w_up, w_down, tile, tf=1792):
    n_rows, d = xs.shape
    dff = w_gate.shape[2]
    grid_spec = pltpu.PrefetchScalarGridSpec(
        num_scalar_prefetch=1,
        grid=(n_rows // tile, dff // tf),
        in_specs=[pl.BlockSpec((tile, d), lambda i, f, te: (i, 0)),
                  pl.BlockSpec((None, d, tf), lambda i, f, te: (te[i], 0, f)),
                  pl.BlockSpec((None, d, tf), lambda i, f, te: (te[i], 0, f)),
                  pl.BlockSpec((None, tf, d), lambda i, f, te: (te[i], f, 0))],
        out_specs=pl.BlockSpec((tile, d), lambda i, f, te: (i, 0)),
        scratch_shapes=[pltpu.VMEM((tile, d), BF16), pltpu.VMEM((tile, d), F32)])
    return pl.pallas_call(
        _experts_kernel,
        grid_spec=grid_spec,
        out_shape=jax.ShapeDtypeStruct((n_rows, d), F32),
        compiler_params=_cparams("arbitrary", "arbitrary"),
        name="moe_experts",
    )(tile_expert, xs, w_gate.astype(BF16), w_up.astype(BF16), w_down.astype(BF16))


def _combine_kernel(slots_ref, next_slots_ref, x_ref, gates_ref, ys_hbm, out_ref, buf_ref, sems, *, tm):
    i = pl.program_id(0)

    def gather(slots, half):
        def issue(t, carry):
            for k in range(2):
                _row_copy(ys_hbm, slots[0, 2 * t + k], buf_ref.at[half, k], t, sems.at[half]).start()
            return carry
        lax.fori_loop(0, tm, issue, 0, unroll=4)

    @pl.when(i == 0)
    def _():
        gather(slots_ref, 0)

    @pl.when(i + 1 < pl.num_programs(0))
    def _():
        gather(next_slots_ref, (i + 1) % 2)

    half = i % 2
    for k in range(2):
        pltpu.make_async_copy(ys_hbm.at[pl.ds(0, tm)], buf_ref.at[half, k], sems.at[half]).wait()
    gates = gates_ref[...]
    out_ref[...] = x_ref[...] + gates[:, 0:1] * buf_ref[half, 0] + gates[:, 1:2] * buf_ref[half, 1]


def _combine(x, ys, slots, gates, tm=256):
    t, d = x.shape
    tok = pl.BlockSpec((tm, d), lambda i: (i, 0))
    n_tiles = t // tm
    slots3 = slots.reshape(n_tiles, 1, 2 * tm)
    return pl.pallas_call(
        functools.partial(_combine_kernel, tm=tm),
        grid=(n_tiles,),
        in_specs=[pl.BlockSpec((None, 1, 2 * tm), lambda i: (i, 0, 0), memory_space=pltpu.SMEM),
                  pl.BlockSpec((None, 1, 2 * tm), lambda i: (jnp.minimum(i + 1, n_tiles - 1), 0, 0),
                               memory_space=pltpu.SMEM),
                  tok, pl.BlockSpec((tm, 2), lambda i: (i, 0)), pl.BlockSpec(memory_space=pl.ANY)],
        out_specs=tok,
        out_shape=jax.ShapeDtypeStruct((t, d), F32),
        scratch_shapes=[pltpu.VMEM((2, 2, tm, d), F32), pltpu.SemaphoreType.DMA((2,))],
        compiler_params=_cparams("arbitrary"),
        name="moe_combine",
    )(slots3, slots3, x, gates, ys)


def _moe(x, gain, w_router, w_gate, w_up, w_down, tile=512):
    n_experts = w_gate.shape[0]
    h, comb, rank, counts = _router(x, gain, w_router)
    slots, gates, tile_expert, holes, n_tiles = _route_plan(comb, rank, counts, n_experts, tile)
    xs = _dispatch(h, slots, holes, n_tiles * tile)
    ys = _experts(xs, tile_expert, w_gate, w_up, w_down, tile)
    return _combine(x, ys, slots, gates)


def _lane_chunks(width):
    return [slice(c * LANES, (c + 1) * LANES) for c in range(width // LANES)]


def _store_dilated(dst_ref, stage_ref, value, dil):
    rows, width = value.shape
    if dil == 1:
        dst_ref[...] = value.astype(dst_ref.dtype)
        return
    for c, ls in enumerate(_lane_chunks(width)):
        stage_ref[c] = value[:, ls]
    for rho in range(dil):
        for c in range(width // LANES):
            dst_ref[:, pl.ds(rho * width + c * LANES, LANES)] = (
                stage_ref.at[c][pl.ds(rho, rows // dil, stride=dil), :].astype(dst_ref.dtype))


def _load_dilated(src_ref, stage_ref, base, rows, width, dil):
    if dil == 1:
        return src_ref[...].astype(F32)
    for rho in range(dil):
        for c in range(width // LANES):
            stage_ref.at[base + c][pl.ds(rho, rows // dil, stride=dil), :] = (
                src_ref[:, pl.ds(rho * width + c * LANES, LANES)].astype(F32))
    return jnp.concatenate([stage_ref[base + c] for c in range(width // LANES)], 1)


def _attn_in_kernel(x_ref, gain_ref, w_ref, hg_ref, e_ref, et_ref, *rest, dils):
    outs, (h_ref, stage_ref) = rest[:len(dils)], rest[len(dils):]
    g = pl.program_id(1)

    @pl.when(g == 0)
    def _():
        h_ref[...] = _rms(x_ref[...], gain_ref[...]).astype(BF16)

    n_qk = 2 * GROUP_WIDTH
    half = h_ref.shape[0] // 2
    ys = [jnp.dot(h_ref[pl.ds(r * half, half), :], w_ref[...], preferred_element_type=F32) for r in range(2)]
    sq = [_split_dot(yy[:, :n_qk] * yy[:, :n_qk], e_ref[...]) for yy in ys]
    ms = [_split_dot(x, et_ref[...]) * (1.0 / HEAD) for x in sq]
    qk = [yy[:, :n_qk] * lax.rsqrt(m + RMS_EPS) * hg_ref[...] for yy, m in zip(ys, ms)]
    y = jnp.concatenate([jnp.concatenate([a, yy[:, n_qk:]], 1) for a, yy in zip(qk, ys)], 0)
    for gi, dil in enumerate(dils):
        @pl.when(g == gi)
        def _(gi=gi, dil=dil):
            _store_dilated(outs[gi], stage_ref, y, dil)


def _attn_in(x, gain, w_in, q_gain, k_gain, tm=512):
    t, d = x.shape
    n_groups = q_gain.shape[0]
    gw = 3 * GROUP_WIDTH
    dils = tuple(dil for _, dil in ATTN_GROUPS)
    w = jnp.transpose(w_in.reshape(d, 3, n_groups, GROUP_WIDTH), (2, 0, 1, 3)).reshape(n_groups, d, gw).astype(BF16)
    head_gain = jnp.concatenate([jnp.tile(q_gain, (1, GROUP_HEADS)), jnp.tile(k_gain, (1, GROUP_HEADS))],
                                1).reshape(n_groups, 1, 2 * GROUP_WIDTH)
    e, et = _head_indicator(2 * GROUP_WIDTH)
    return pl.pallas_call(
        functools.partial(_attn_in_kernel, dils=dils),
        grid=(t // tm, n_groups),
        in_specs=[pl.BlockSpec((tm, d), lambda i, g: (i, 0)),
                  _const_spec((1, d)),
                  pl.BlockSpec((None, d, gw), lambda i, g: (g, 0, 0)),
                  pl.BlockSpec((None, 1, 2 * GROUP_WIDTH), lambda i, g: (g, 0, 0)),
                  _const_spec(e.shape), _const_spec(et.shape)],
        out_specs=[pl.BlockSpec((tm // dil, dil * gw), lambda i, g: (i, 0)) for dil in dils],
        out_shape=[jax.ShapeDtypeStruct((t // dil, dil * gw), BF16) for dil in dils],
        scratch_shapes=[pltpu.VMEM((tm, d), BF16), pltpu.VMEM((gw // LANES, tm, LANES), F32)],
        compiler_params=_cparams("arbitrary", "arbitrary"),
        name="attn_in",
    )(x, gain.reshape(1, d), w, head_gain, e, et)


def _t5_bucket_np(rel):
    nb = REL_BUCKETS // 2
    max_exact = nb // 2
    n = np.abs(rel)
    large = max_exact + (np.log(np.maximum(n, 1).astype(np.float32) / max_exact)
                         / math.log(REL_MAX_DIST / max_exact) * (nb - max_exact)).astype(np.int32)
    large = np.minimum(large, nb - 1)
    return np.where(rel > 0, nb, 0) + np.where(n < max_exact, n, large)


def _bias_table(rel_bias, group, window, dil):
    side = window // (2 * dil)
    assert side == KV_HALO
    buckets = _t5_bucket_np(dil * np.arange(-side, side + 1))
    table = rel_bias.reshape(REL_BUCKETS, -1, GROUP_HEADS)[buckets, group].T
    full = jnp.full((8, Q_BLOCK + 2 * KV_HALO), NEG_INF, F32)
    return full.at[:GROUP_HEADS, :2 * side + 1].set(table)


def _attn_kernel(q_ref, kp_ref, kc_ref, kn_ref, vp_ref, vc_ref, vn_ref, table_ref, o_ref, lse_ref, bias_ref,
                 *, regions, dil):
    tp, sp, ss = regions
    nk = Q_BLOCK + 2 * KV_HALO

    @pl.when((pl.program_id(0) == 0) & (pl.program_id(1) == 0))
    def _():
        for h in range(GROUP_HEADS):
            row = jnp.broadcast_to(table_ref[h:h + 1, :], (Q_BLOCK, nk))
            bias_ref[h] = pltpu.roll(row, 0, 1, stride=1, stride_axis=0)

    lane = lax.broadcasted_iota(jnp.int32, (Q_BLOCK, LANES), 1)
    mine = [(lane < HEAD), (lane >= HEAD)]
    scale = HEAD ** -0.5
    q_all = q_ref[...]
    k_all = jnp.concatenate([kp_ref[...], kc_ref[...], kn_ref[...]], 0)
    v_all = jnp.concatenate([vp_ref[...], vc_ref[...], vn_ref[...]], 0)
    lanes_of = lambda pair: slice(pair * LANES, (pair + 1) * LANES)

    units = [(blk, pair, sub) for blk in range(Q_STEP // Q_BLOCK) for pair in range(GROUP_HEADS // 2)
             for sub in range(2)]
    valid = []
    for blk in range(Q_STEP // Q_BLOCK):
        row0 = pl.program_id(1) * Q_STEP + blk * Q_BLOCK
        seq = _seq_len_at(row0, tp // dil, sp // dil, ss // dil)
        seq_start = row0 - lax.rem(row0, seq)
        key_row = row0 - KV_HALO + lax.broadcasted_iota(jnp.int32, (1, nk), 1)
        valid.append((key_row >= seq_start) & (key_row < seq_start + seq))
    s_all = []
    for blk, pair, sub in units:
        qp = q_all[blk * Q_BLOCK:(blk + 1) * Q_BLOCK, lanes_of(pair)]
        qm = jnp.where(mine[sub], qp, jnp.zeros_like(qp))
        keys = k_all[blk * Q_BLOCK:blk * Q_BLOCK + nk, lanes_of(pair)]
        s = lax.dot_general(qm, keys, (_NT, ((), ())), preferred_element_type=F32)
        s_all.append(jnp.where(valid[blk], s * scale + bias_ref[2 * pair + sub], NEG_INF))
    m_all = [jnp.max(s, axis=-1, keepdims=True) for s in s_all]
    p_all = [jnp.exp(s - m) for s, m in zip(s_all, m_all)]
    den_all = [jnp.sum(p, axis=-1, keepdims=True) for p in p_all]
    o_all = [jnp.dot((p / den).astype(v_all.dtype), v_all[blk * Q_BLOCK:blk * Q_BLOCK + nk, lanes_of(pair)],
                     preferred_element_type=F32)
             for p, den, (blk, pair, _) in zip(p_all, den_all, units)]
    lse_all = [m + jnp.log(den) for m, den in zip(m_all, den_all)]
    for i in range(0, len(units), 2):
        blk, pair, _ = units[i]
        rows = pl.ds(blk * Q_BLOCK, Q_BLOCK)
        o_ref[rows, lanes_of(pair)] = jnp.where(mine[0], o_all[i], o_all[i + 1]).astype(o_ref.dtype)
        lse_ref[rows, lanes_of(pair)] = jnp.where(mine[0], lse_all[i], lse_all[i + 1])


def _attn_group(qkv_d, rel_bias, group, regions):
    window, dil = ATTN_GROUPS[group]
    rows = qkv_d.shape[0]
    parts = 3
    table = _bias_table(rel_bias, group, window, dil)
    halo_per_step = Q_STEP // KV_HALO
    last_halo = rows // KV_HALO - 1

    def cur(part):
        return pl.BlockSpec((Q_STEP, GROUP_WIDTH), lambda rho, m: (m, rho * parts + part))

    def prev(part):
        return pl.BlockSpec((KV_HALO, GROUP_WIDTH),
                            lambda rho, m: (jnp.maximum(m * halo_per_step - 1, 0), rho * parts + part))

    def nxt(part):
        return pl.BlockSpec((KV_HALO, GROUP_WIDTH),
                            lambda rho, m: (jnp.minimum((m + 1) * halo_per_step, last_halo), rho * parts + part))

    out_spec = pl.BlockSpec((Q_STEP, GROUP_WIDTH), lambda rho, m: (m, rho))
    return pl.pallas_call(
        functools.partial(_attn_kernel, regions=regions, dil=dil),
        grid=(dil, rows // Q_STEP),
        in_specs=[cur(0), prev(1), cur(1), nxt(1), prev(2), cur(2), nxt(2), _const_spec(table.shape)],
        out_specs=[out_spec, out_spec],
        out_shape=[jax.ShapeDtypeStruct((rows, dil * GROUP_WIDTH), BF16),
                   jax.ShapeDtypeStruct((rows, dil * GROUP_WIDTH), F32)],
        scratch_shapes=[pltpu.VMEM((GROUP_HEADS, Q_BLOCK, Q_BLOCK + 2 * KV_HALO), F32)],
        compiler_params=_cparams("arbitrary", "arbitrary"),
        name=f"attn_g{group}",
    )(qkv_d, qkv_d, qkv_d, qkv_d, qkv_d, qkv_d, qkv_d, table)


def _attn_out_kernel(o0, o1, o2, l0, l1, l2, x_ref, wo_ref, out_ref, stage_ref, *, dils):
    tm = x_ref.shape[0]
    chunks = GROUP_WIDTH // LANES
    os_ = [_load_dilated(r, stage_ref, 2 * g * chunks, tm, GROUP_WIDTH, dils[g]) for g, r in enumerate((o0, o1, o2))]
    lses = [_load_dilated(r, stage_ref, (2 * g + 1) * chunks, tm, GROUP_WIDTH, dils[g])
            for g, r in enumerate((l0, l1, l2))]
    mx = jnp.maximum(jnp.maximum(lses[0], lses[1]), lses[2])
    ws = [jnp.exp(l - mx) for l in lses]
    tot = ws[0] + ws[1] + ws[2]
    acc = x_ref[...]
    for g in range(len(dils)):
        scaled = (os_[g] * (ws[g] / tot)).astype(BF16)
        acc = acc + jnp.dot(scaled, wo_ref[g], preferred_element_type=F32)
    out_ref[...] = acc


def _attn_out(os_, lses, x, w_o, tm=512):
    t, d = x.shape
    n_groups = len(os_)
    dils = tuple(dil for _, dil in ATTN_GROUPS)
    grp = [pl.BlockSpec((tm // dil, dil * GROUP_WIDTH), lambda i: (i, 0)) for dil in dils]
    tok = pl.BlockSpec((tm, d), lambda i: (i, 0))
    wo = w_o.reshape(n_groups, GROUP_WIDTH, d).astype(BF16)
    return pl.pallas_call(
        functools.partial(_attn_out_kernel, dils=dils),
        grid=(t // tm,),
        in_specs=grp + grp + [tok, _const_spec(wo.shape)],
        out_specs=tok,
        out_shape=jax.ShapeDtypeStruct((t, d), F32),
        scratch_shapes=[pltpu.VMEM((2 * n_groups * (GROUP_WIDTH // LANES), tm, LANES), F32)],
        compiler_params=_cparams("arbitrary"),
        name="attn_out",
    )(*os_, *lses, x, wo)


def kernel(x_prompt, x_sample, norm_mix, norm_ffn, rwkv_mu, rwkv_w_rkv, rwkv_w0, rwkv_w1, rwkv_w2, rwkv_a0, rwkv_a1, rwkv_a2, rwkv_g1, rwkv_g2, rwkv_k_k, rwkv_k_a, rwkv_r_k, rwkv_ln_g, rwkv_ln_b, rwkv_w_o, attn_w_in, attn_q_gain, attn_k_gain, attn_w_o, rel_bias, ffn_w_gate, ffn_w_up, ffn_w_down, moe_router, moe_w_gate, moe_w_up, moe_w_down):
    d = x_prompt.shape[-1]
    tp = x_prompt.shape[0] * x_prompt.shape[1]
    regions = (tp, x_prompt.shape[1], x_sample.shape[1])
    x = jnp.concatenate([x_prompt.reshape(-1, d), x_sample.reshape(-1, d)], 0)
    depth = norm_mix.shape[0]
    for i in range(depth):
        j = i // 2
        if i % 2 == 0:
            r, v, kk, g, bonus, logw, kdir, bdir = _rwkv_pre(
                x, regions, norm_mix[i], rwkv_mu[j], rwkv_w_rkv[j], rwkv_w0[j], rwkv_w1[j], rwkv_w2[j],
                rwkv_a0[j], rwkv_a1[j], rwkv_a2[j], rwkv_g1[j], rwkv_g2[j], rwkv_k_k[j], rwkv_k_a[j], rwkv_r_k[j])
            yf = _wkv(r, v, kk, logw, kdir, bdir, regions, reverse=False)
            yb = _wkv(r, v, kk, logw, kdir, bdir, regions, reverse=True)
            x = _rwkv_post(yf, yb, bonus, g, x, rwkv_ln_g[j], rwkv_ln_b[j], rwkv_w_o[j])
            x = _ffn(x, norm_ffn[i], ffn_w_gate[j], ffn_w_up[j], ffn_w_down[j])
        else:
            qkv = _attn_in(x, norm_mix[i], attn_w_in[j], attn_q_gain[j], attn_k_gain[j])
            outs = [_attn_group(qkv[g], rel_bias, g, regions) for g in range(len(ATTN_GROUPS))]
            x = _attn_out([o for o, _ in outs], [l for _, l in outs], x, attn_w_o[j])
            x = _moe(x, norm_ffn[i], moe_router[j], moe_w_gate[j], moe_w_up[j], moe_w_down[j])
    y_prompt = x[:tp].reshape(x_prompt.shape)
    y_sample = x[tp:].reshape(x_sample.shape)
    return (y_prompt, y_sample)
```

```python
import functools
import math

import numpy as np
import jax
import jax.numpy as jnp
from jax import lax
from jax.experimental import pallas as pl
from jax.experimental.pallas import tpu as pltpu

F32 = jnp.float32
BF16 = jnp.bfloat16

LANES = 128
VMEM_LIMIT_BYTES = 56 * 1024 * 1024

HEAD = 64
RMS_EPS = 1e-6
GN_EPS = 64e-5
ATTN_GROUPS = ((128, 1), (512, 4), (2048, 16))
GROUP_HEADS = 6
GROUP_WIDTH = GROUP_HEADS * HEAD
REL_BUCKETS = 32
REL_MAX_DIST = 1024
Q_BLOCK = 128
Q_STEP = 512
KV_HALO = 64
SCAN_CHUNK = 64
NEG_INF = -1e30


def _cparams(*sem):
    return pltpu.CompilerParams(dimension_semantics=sem, vmem_limit_bytes=VMEM_LIMIT_BYTES)


def _const_spec(shape):
    nd = len(shape)
    return pl.BlockSpec(shape, lambda *_: (0,) * nd, pipeline_mode=pl.Buffered(1))


def _split_dot(x, w):
    hi = x.astype(BF16)
    lo = (x - hi.astype(F32)).astype(BF16)
    return (jnp.dot(hi, w, preferred_element_type=F32)
            + jnp.dot(lo, w, preferred_element_type=F32))


def _head_sum(x, e_ref, et_ref):
    return _split_dot(_split_dot(x, e_ref[...]), et_ref[...])


def _rms(x, gain):
    return x * lax.rsqrt(jnp.mean(x * x, axis=-1, keepdims=True) + RMS_EPS) * gain


def _seq_len_at(row0, tp, sp, ss):
    return jnp.where(row0 < tp, sp, ss)


def _head_indicator(width):
    e = np.zeros((width, LANES), np.float32)
    e[np.arange(width), np.arange(width) // HEAD] = 1.0
    return jnp.asarray(e, BF16), jnp.asarray(e.T, BF16)


def _rwkv_pre_kernel(x_ref, xp_ref, xn_ref, gain_ref, mu_ref, wrkv_ref, wl1_ref, w2_ref, a2_ref,
                     g2_ref, w0_ref, a0_ref, kk_gain_ref, ka_ref, rk_ref, e_ref, et_ref,
                     r_out, v_out, kk_out, g_out, bonus_out, logw_out, kdir_out, bdir_out,
                     *, tm, regions):
    tp, sp, ss = regions
    d = x_ref.shape[-1]
    row0 = pl.program_id(0) * tm
    seq = _seq_len_at(row0, tp, sp, ss)
    at_start = lax.rem(row0, seq) == 0
    at_end = lax.rem(row0 + tm, seq) == 0

    gain = gain_ref[...]
    h = _rms(x_ref[...], gain)
    hp = _rms(xp_ref[...], gain)[7:8, :] * jnp.where(at_start, 0.0, 1.0)
    hn = _rms(xn_ref[...], gain)[0:1, :] * jnp.where(at_end, 0.0, 1.0)
    rows = lax.broadcasted_iota(jnp.int32, (tm, d), 0)
    h_prev = jnp.where(rows == 0, hp, pltpu.roll(h, 1, 0))
    h_next = jnp.where(rows == tm - 1, hn, pltpu.roll(h, tm - 1, 0))
    xx = 0.5 * (h_prev + h_next) - h

    def half_tile(rs):
        hh, xh = h[rs], xx[rs]
        mixed = lambda n: (hh + xh * mu_ref[n:n + 1, :]).astype(BF16)
        r = jnp.dot(mixed(0), wrkv_ref[0], preferred_element_type=F32)
        k = jnp.dot(mixed(1), wrkv_ref[1], preferred_element_type=F32)
        v = jnp.dot(mixed(2), wrkv_ref[2], preferred_element_type=F32)
        r_out[rs, :] = r
        v_out[rs, :] = v
        yield
        t_w = jnp.tanh(jnp.dot(mixed(3), wl1_ref[0], preferred_element_type=F32))
        t_a = jnp.dot(mixed(4), wl1_ref[1], preferred_element_type=F32)
        t_g = jax.nn.sigmoid(jnp.dot(mixed(5), wl1_ref[2], preferred_element_type=F32))
        yield
        lw = w0_ref[...] + jnp.dot(t_w.astype(BF16), w2_ref[...], preferred_element_type=F32)
        a = jax.nn.sigmoid(a0_ref[...] + jnp.dot(t_a.astype(BF16), a2_ref[...], preferred_element_type=F32))
        g_out[rs, :] = jnp.dot(t_g.astype(BF16), g2_ref[...], preferred_element_type=F32)
        logw = -math.exp(-0.5) * jax.nn.sigmoid(lw)
        for z in range(2):
            logw_out[z, rs, :] = logw[:, z * d:(z + 1) * d]
        yield
        kk = k * kk_gain_ref[...]
        kk_sq = _split_dot(kk * kk, e_ref[...])
        rk_sum = _split_dot(r * k * rk_ref[...], e_ref[...])
        yield
        kk = kk * lax.rsqrt(jnp.maximum(_split_dot(kk_sq, et_ref[...]), 1e-12))
        bonus_out[rs, :] = _split_dot(rk_sum, et_ref[...]) * v
        kk_out[rs, :] = kk
        ka = ka_ref[...]
        for z in range(2):
            a_z = a[:, z * d:(z + 1) * d]
            kdir_out[z, rs, :] = k * (1.0 + (a_z - 1.0) * ka)
            bdir_out[z, rs, :] = kk * a_z

    live = [half_tile(slice(i * (tm // 2), (i + 1) * (tm // 2))) for i in range(2)]
    while live:
        live = [gen for gen in live if next(gen, StopIteration) is not StopIteration]


def _block_diag2(m):
    z = jnp.zeros_like(m[0])
    return jnp.concatenate([jnp.concatenate([m[0], z], 1), jnp.concatenate([z, m[1]], 1)], 0)


def _rwkv_pre(x, regions, gain, mu, w_rkv, w0, w1, w2, a0, a1, a2, g1, g2, k_k, k_a, r_k, tm=256):
    t, d = x.shape
    e, et = _head_indicator(d)
    wl1 = jnp.stack([jnp.concatenate([w1[0], w1[1]], 1), jnp.concatenate([a1[0], a1[1]], 1), g1]).astype(BF16)
    w2c = _block_diag2(w2).astype(BF16)
    a2c = _block_diag2(a2).astype(BF16)
    row = lambda p: p.reshape(1, -1).astype(F32)
    tok = pl.BlockSpec((tm, d), lambda i: (i, 0))
    halo_blocks = t // 8
    prev = pl.BlockSpec((8, d), lambda i: (jnp.maximum(i * (tm // 8) - 1, 0), 0))
    nxt = pl.BlockSpec((8, d), lambda i: (jnp.minimum((i + 1) * (tm // 8), halo_blocks - 1), 0))
    dir_tok = pl.BlockSpec((2, tm, d), lambda i: (0, i, 0))
    consts = [row(gain), mu.astype(F32), w_rkv.astype(BF16), wl1, w2c, a2c, g2.astype(BF16),
              row(w0), row(a0), row(k_k), row(k_a), row(r_k), e, et]
    tok_shape = jax.ShapeDtypeStruct((t, d), F32)
    dir_shape = jax.ShapeDtypeStruct((2, t, d), F32)
    return pl.pallas_call(
        functools.partial(_rwkv_pre_kernel, tm=tm, regions=regions),
        grid=(t // tm,),
        in_specs=[tok, prev, nxt] + [_const_spec(c.shape) for c in consts],
        out_specs=[tok] * 5 + [dir_tok] * 3,
        out_shape=[tok_shape] * 5 + [dir_shape] * 3,
        compiler_params=_cparams("arbitrary"),
        name="rwkv_pre",
    )(x, x, x, *consts)


_NN = ((1,), (0,))
_NT = ((1,), (1,))
_TN = ((0,), (0,))


def _mm(a, b, dims=_NN):
    return lax.dot_general(a.astype(BF16), b.astype(BF16), (dims, ((), ())), preferred_element_type=F32)


def _cumsum_rows(x, reverse):
    n = x.shape[0]
    row = lax.broadcasted_iota(jnp.int32, x.shape, 0)
    s = 1
    while s < n:
        if reverse:
            x = x + jnp.where(row < n - s, pltpu.roll(x, n - s, 0), 0.0)
        else:
            x = x + jnp.where(row >= s, pltpu.roll(x, s, 0), 0.0)
        s *= 2
    return x


def _wkv_kernel(r_ref, v_ref, kk_ref, logw_ref, kdir_ref, bdir_ref, y_ref, state_ref,
                *, rb, regions, reverse, group_chunks=2):
    tp, sp, ss = regions
    c = SCAN_CHUNK
    npairs = r_ref.shape[-1] // LANES
    nsteps = pl.num_programs(1)
    step = pl.program_id(1)
    blk = (nsteps - 1 - step) if reverse else step
    row0 = blk * rb
    seq = _seq_len_at(row0, tp, sp, ss)
    first = (lax.rem(row0 + rb, seq) == 0) if reverse else (lax.rem(row0, seq) == 0)

    @pl.when(first)
    def _():
        state_ref[...] = jnp.zeros_like(state_ref)

    ti2 = lax.broadcasted_iota(jnp.int32, (c, 2 * c), 0)
    tj2 = lax.broadcasted_iota(jnp.int32, (c, 2 * c), 1) & (c - 1)
    ti4 = lax.broadcasted_iota(jnp.int32, (c, 4 * c), 0)
    tj4 = lax.broadcasted_iota(jnp.int32, (c, 4 * c), 1) & (c - 1)
    incl4 = (tj4 >= ti4) if reverse else (tj4 <= ti4)
    strict2 = (tj2 > ti2) if reverse else (tj2 < ti2)
    lane = lax.broadcasted_iota(jnp.int32, (c, LANES), 1)
    head0 = lane < HEAD
    rr = lax.broadcasted_iota(jnp.int32, (LANES, LANES), 0)
    cc = lax.broadcasted_iota(jnp.int32, (LANES, LANES), 1)
    same_head = (rr < HEAD) == (cc < HEAD)

    col4 = lax.broadcasted_iota(jnp.int32, (c, 4 * c), 1)
    eye4 = ((col4 & (c - 1)) == ti4).astype(F32)

    def block_diag4(m):
        return jnp.concatenate([jnp.where((col4 >= h * c) & (col4 < (h + 1) * c), m, 0.0) for h in range(4)], 0)

    def stack_heads(m):
        return jnp.concatenate([jnp.where(head0, m, 0.0), jnp.where(head0, 0.0, m)], 0)

    order = list(range(rb // c - 1, -1, -1) if reverse else range(rb // c))
    each = lambda fn, *lists: [fn(*xs) for xs in zip(*lists)]

    def prepare(chunks, out):
        units = [(pl.ds(ci * c, c), pl.ds(p * LANES, LANES)) for ci in chunks for p in range(npairs)]
        lw = [logw_ref[u] for u in units]
        cum = each(lambda x: _cumsum_rows(x, reverse), lw)
        total = each(lambda x: jnp.sum(x, axis=0, keepdims=True), lw)
        lhs = [jnp.concatenate([r_ref[u] * jnp.exp(cm), kk_ref[u] * jnp.exp(cm - x)], 0).astype(BF16)
               for u, cm, x in zip(units, cum, lw)]
        rhs = []
        kb_tail = []
        for u, cm, tt in zip(units, cum, total):
            kd, bd = kdir_ref[u], bdir_ref[u]
            inv = jnp.exp(-cm)
            tail = jnp.exp(tt - cm)
            rhs.append(jnp.concatenate([stack_heads(kd * inv), stack_heads(bd * inv)], 0).astype(BF16))
            kb_tail.append(jnp.concatenate([kd * tail, -(bd * tail)], 0).astype(BF16))
        yield
        q = each(lambda a, b: _mm(a, b, _NT), lhs, rhs)
        a_out = [jnp.where(incl4, x[0:c], 0.0).astype(BF16) for x in q]
        a_kk = [jnp.where(strict2, x[c:2 * c, 0:2 * c], 0.0).astype(BF16) for x in q]
        low = [jnp.where(strict2, x[c:2 * c, 2 * c:4 * c], 0.0) for x in q]
        pw = [-jnp.concatenate([low[i], low[i + 1]], 1) for i in range(0, len(units), 2)]
        inv_cat = [eye4 + x for x in pw]
        yield
        pw = each(lambda x: _mm(x, block_diag4(x)), pw)
        v_all = [v_ref[u] for u in units]
        v_bd = [stack_heads(x).astype(BF16) for x in v_all]
        akv = each(_mm, a_kk, v_bd)
        for _ in range(int(math.log2(c)) - 2):
            yield
            both = each(lambda t, x: _mm(jnp.concatenate([t, x], 0), block_diag4(x)), inv_cat, pw)
            inv_cat = each(lambda t, b: t + b[0:c], inv_cat, both)
            pw = [b[c:2 * c] for b in both]
        yield
        inv_cat = each(lambda t, x: t + _mm(t, block_diag4(x)), inv_cat, pw)
        inv_l = [stack_heads(t[:, h * 2 * c:(h + 1) * 2 * c]).astype(BF16) for t in inv_cat for h in range(2)]
        out.update(units=units, lhs=lhs, kb_tail=kb_tail, a_out=a_out, akv=akv, inv_l=inv_l, v_all=v_all,
                   v_bd=v_bd, decay=[jnp.exp(x) for x in total])

    def advance(group, k, states):
        pre = {name: vals[k * npairs:(k + 1) * npairs] for name, vals in group.items()}
        sk = each(lambda a, st: _mm(a, st, _NT), pre["lhs"], states)
        yield
        u_bd = each(lambda t, s_, kv: _mm(t, stack_heads(s_[c:2 * c] + kv)), pre["inv_l"], sk, pre["akv"])
        yield
        y = each(lambda s_, a, vb, ub: s_[0:c] + _mm(a, jnp.concatenate([vb, (-ub).astype(BF16)], 0)),
                 sk, pre["a_out"], pre["v_bd"], u_bd)
        for u, yy in zip(pre["units"], y):
            y_ref[u] = yy
        upd = each(lambda vv, ub, kb: _mm(jnp.concatenate([vv, ub[0:c] + ub[c:2 * c]], 0), kb, _TN),
                   pre["v_all"], u_bd, pre["kb_tail"])
        states[:] = each(lambda st, dc, up: st * dc + jnp.where(same_head, up, 0.0), states, pre["decay"], upd)

    def run(*gens):
        live = [g for g in gens if g is not None]
        while live:
            live = [g for g in live if next(g, StopIteration) is not StopIteration]

    states = [state_ref[p] for p in range(npairs)]
    groups = [order[i:i + group_chunks] for i in range(0, len(order), group_chunks)]
    prepared = {}
    run(prepare(groups[0], prepared))
    for gi, chunks in enumerate(groups):
        nxt = {}
        following = prepare(groups[gi + 1], nxt) if gi + 1 < len(groups) else None

        def advance_all(prepared=prepared, chunks=chunks):
            for k in range(len(chunks)):
                yield from advance(prepared, k, states)
                yield

        run(following, advance_all())
        prepared = nxt
    for p in range(npairs):
        state_ref[p] = states[p]


def _wkv(r, v, kk, logw, kdir, bdir, regions, reverse, rb=512, width=1024):
    t, d = r.shape
    z = 1 if reverse else 0
    nsteps = t // rb

    def row_block(s):
        return (nsteps - 1 - s) if reverse else s

    tok = pl.BlockSpec((rb, width), lambda j, s: (row_block(s), j))
    dir_tok = pl.BlockSpec((None, rb, width), lambda j, s: (z, row_block(s), j))
    return pl.pallas_call(
        functools.partial(_wkv_kernel, rb=rb, regions=regions, reverse=reverse),
        grid=(d // width, nsteps),
        in_specs=[tok, tok, tok, dir_tok, dir_tok, dir_tok],
        out_specs=tok,
        out_shape=jax.ShapeDtypeStruct((t, d), F32),
        scratch_shapes=[pltpu.VMEM((width // LANES, LANES, LANES), F32)],
        compiler_params=_cparams("arbitrary", "arbitrary"),
        name="wkv_bwd" if reverse else "wkv_fwd",
    )(r, v, kk, logw, kdir, bdir)


def _rwkv_post_kernel(yf_ref, yb_ref, bonus_ref, g_ref, x_ref, lng_ref, lnb_ref, wo_ref, e_ref, et_ref, out_ref):
    y = yf_ref[...] + yb_ref[...]
    mean = _head_sum(y, e_ref, et_ref) * (1.0 / HEAD)
    dlt = y - mean
    var = _head_sum(dlt * dlt, e_ref, et_ref) * (1.0 / HEAD)
    yn = dlt * lax.rsqrt(var + GN_EPS) * lng_ref[...] + lnb_ref[...]
    mixed = ((yn + bonus_ref[...]) * g_ref[...]).astype(BF16)
    out_ref[...] = x_ref[...] + jnp.dot(mixed, wo_ref[...], preferred_element_type=F32)


def _rwkv_post(yf, yb, bonus, g, x, ln_g, ln_b, w_o, tm=256):
    t, d = x.shape
    e, et = _head_indicator(d)
    tok = pl.BlockSpec((tm, d), lambda i: (i, 0))
    consts = [ln_g.reshape(1, d), ln_b.reshape(1, d), w_o.astype(BF16), e, et]
    return pl.pallas_call(
        _rwkv_post_kernel,
        grid=(t // tm,),
        in_specs=[tok] * 5 + [_const_spec(c.shape) for c in consts],
        out_specs=tok,
        out_shape=jax.ShapeDtypeStruct((t, d), F32),
        compiler_params=_cparams("arbitrary"),
        name="rwkv_post",
    )(yf, yb, bonus, g, x, *consts)


def _ffn_kernel(x_ref, gain_ref, wg_ref, wu_ref, wd_ref, out_ref, h_ref, acc_ref):
    f = pl.program_id(1)

    @pl.when(f == 0)
    def _():
        h_ref[...] = _rms(x_ref[...], gain_ref[...]).astype(BF16)
        acc_ref[...] = jnp.zeros_like(acc_ref)

    h = h_ref[...]
    gate = jnp.dot(h, wg_ref[...], preferred_element_type=F32)
    up = jnp.dot(h, wu_ref[...], preferred_element_type=F32)
    hidden = (jax.nn.silu(gate) * up).astype(BF16)
    acc_ref[...] += jnp.dot(hidden, wd_ref[...], preferred_element_type=F32)

    @pl.when(f == pl.num_programs(1) - 1)
    def _():
        out_ref[...] = x_ref[...] + acc_ref[...]


def _ffn(x, gain, w_gate, w_up, w_down, tm=512, tf=2816):
    t, d = x.shape
    dff = w_gate.shape[1]
    return pl.pallas_call(
        _ffn_kernel,
        grid=(t // tm, dff // tf),
        in_specs=[pl.BlockSpec((tm, d), lambda i, f: (i, 0)),
                  _const_spec((1, d)),
                  pl.BlockSpec((d, tf), lambda i, f: (0, f)),
                  pl.BlockSpec((d, tf), lambda i, f: (0, f)),
                  pl.BlockSpec((tf, d), lambda i, f: (f, 0))],
        out_specs=pl.BlockSpec((tm, d), lambda i, f: (i, 0)),
        out_shape=jax.ShapeDtypeStruct((t, d), F32),
        scratch_shapes=[pltpu.VMEM((tm, d), BF16), pltpu.VMEM((tm, d), F32)],
        compiler_params=_cparams("arbitrary", "arbitrary"),
        name="ffn",
    )(x, gain.reshape(1, d), w_gate.astype(BF16), w_up.astype(BF16), w_down.astype(BF16))


def _router_kernel(x_ref, gain_ref, wr_ref, tri_ref, h_out, comb_out, rank_out, count_out, carry_ref,
                   *, n_experts):
    @pl.when(pl.program_id(0) == 0)
    def _():
        carry_ref[...] = jnp.zeros_like(carry_ref)

    h = _rms(x_ref[...], gain_ref[...])
    h_out[...] = h
    logits = jnp.dot(h.astype(BF16), wr_ref[...], preferred_element_type=F32)
    lane = lax.broadcasted_iota(jnp.int32, logits.shape, 1)
    logits = jnp.where(lane < n_experts, logits, -jnp.inf)
    m1 = jnp.max(logits, axis=-1, keepdims=True)
    i1 = jnp.min(jnp.where(logits == m1, lane, LANES), axis=-1, keepdims=True)
    rest = jnp.where(lane == i1, -jnp.inf, logits)
    m2 = jnp.max(rest, axis=-1, keepdims=True)
    i2 = jnp.min(jnp.where(rest == m2, lane, LANES), axis=-1, keepdims=True)
    e2 = jnp.exp(m2 - m1)
    g1 = 1.0 / (1.0 + e2)
    g2 = e2 / (1.0 + e2)
    comb_out[...] = jnp.where(lane == i1, g1, 0.0) + jnp.where(lane == i2, g2, 0.0)
    sel = jnp.where((lane == i1) | (lane == i2), 1.0, 0.0)
    earlier = carry_ref[0:1, :] + jnp.dot(tri_ref[...], sel.astype(BF16), preferred_element_type=F32)
    rank_out[...] = sel * (earlier + 1.0)
    carry_ref[...] = carry_ref[...] + jnp.sum(sel, axis=0, keepdims=True)
    count_out[...] = carry_ref[...]


def _router(x, gain, w_router, tm=512):
    t, d = x.shape
    n_experts = w_router.shape[1]
    wr = jnp.zeros((d, LANES), F32).at[:, :n_experts].set(w_router).astype(BF16)
    tri = jnp.asarray(np.tril(np.ones((tm, tm), np.float32), -1), BF16)
    tok = pl.BlockSpec((tm, d), lambda i: (i, 0))
    lanes = pl.BlockSpec((tm, LANES), lambda i: (i, 0))
    return pl.pallas_call(
        functools.partial(_router_kernel, n_experts=n_experts),
        grid=(t // tm,),
        in_specs=[tok, _const_spec((1, d)), _const_spec((d, LANES)), _const_spec((tm, tm))],
        out_specs=[tok, lanes, lanes, pl.BlockSpec((8, LANES), lambda i: (0, 0))],
        out_shape=[jax.ShapeDtypeStruct((t, d), F32), jax.ShapeDtypeStruct((t, LANES), F32),
                   jax.ShapeDtypeStruct((t, LANES), F32), jax.ShapeDtypeStruct((8, LANES), F32)],
        scratch_shapes=[pltpu.VMEM((8, LANES), F32)],
        compiler_params=_cparams("arbitrary"),
        name="moe_router",
    )(x, gain.reshape(1, d), wr, tri)


def _route_plan(comb, rank, counts, n_experts, tile):
    t = comb.shape[0]
    n_tiles = (2 * t) // tile + n_experts
    cnt = counts[0, :n_experts].astype(jnp.int32)
    padded = ((cnt + tile - 1) // tile) * tile
    ends = jnp.cumsum(padded)
    starts = jnp.zeros((LANES,), jnp.int32).at[:n_experts].set(ends - padded)
    slot = jnp.where(rank > 0, starts[None, :] + rank.astype(jnp.int32) - 1, -1)
    slot_hi = jnp.max(slot, axis=1)
    slot_lo = jnp.min(jnp.where(slot >= 0, slot, jnp.iinfo(jnp.int32).max), axis=1)
    gate_lo = jnp.sum(jnp.where(slot == slot_lo[:, None], comb, 0.0), axis=1)
    gate_hi = jnp.sum(jnp.where(slot == slot_hi[:, None], comb, 0.0), axis=1)
    slots = jnp.stack([slot_lo, slot_hi], 1)
    gates = jnp.stack([gate_lo, gate_hi], 1)
    tile_expert = jnp.sum(jnp.arange(n_tiles, dtype=jnp.int32)[:, None] * tile >= ends[None, :], axis=1)
    holes = jnp.concatenate([ends - padded + cnt, ends[-1:], ends, jnp.full((1,), n_tiles * tile, jnp.int32)])
    return slots, gates, jnp.minimum(tile_expert, n_experts - 1).astype(jnp.int32), holes.astype(jnp.int32), n_tiles


def _row_copy(src, src_row, dst, dst_row, sem):
    return pltpu.make_async_copy(src.at[pl.ds(src_row, 1)], dst.at[pl.ds(dst_row, 1)], sem)


def _dispatch_kernel(holes_ref, slots_ref, h_ref, xs_hbm, zero_ref, sem, zero_sem, *, tm, n_holes):
    n_ranges = holes_ref.shape[0] // 2

    @pl.when(pl.program_id(0) == 0)
    def _():
        zero_ref[...] = jnp.zeros_like(zero_ref)
        for e in range(n_ranges):
            def fill(row, carry):
                _row_copy(zero_ref, 0, xs_hbm, row, zero_sem).start()
                return carry
            lax.fori_loop(holes_ref[e], holes_ref[n_ranges + e], fill, 0)
        pltpu.make_async_copy(xs_hbm.at[pl.ds(0, n_holes)], xs_hbm.at[pl.ds(0, n_holes)], zero_sem).wait()

    def issue(i, carry):
        for k in range(2):
            _row_copy(h_ref, i, xs_hbm, slots_ref[0, 2 * i + k], sem).start()
        return carry

    lax.fori_loop(0, tm, issue, 0, unroll=4)
    for _ in range(2):
        pltpu.make_async_copy(h_ref, xs_hbm.at[pl.ds(0, tm)], sem).wait()


def _dispatch(h, slots, holes, n_rows, tm=512):
    t, d = h.shape
    grid_spec = pltpu.PrefetchScalarGridSpec(
        num_scalar_prefetch=1,
        grid=(t // tm,),
        in_specs=[pl.BlockSpec((None, 1, 2 * tm), lambda i, holes: (i, 0, 0), memory_space=pltpu.SMEM),
                  pl.BlockSpec((tm, d), lambda i, holes: (i, 0))],
        out_specs=pl.BlockSpec(memory_space=pl.ANY),
        scratch_shapes=[pltpu.VMEM((8, d), h.dtype), pltpu.SemaphoreType.DMA(()), pltpu.SemaphoreType.DMA(())])
    return pl.pallas_call(
        functools.partial(_dispatch_kernel, tm=tm, n_holes=n_rows - 2 * t),
        grid_spec=grid_spec,
        out_shape=jax.ShapeDtypeStruct((n_rows, d), h.dtype),
        compiler_params=pltpu.CompilerParams(dimension_semantics=("arbitrary",), has_side_effects=True),
        name="moe_dispatch",
    )(holes, slots.reshape(t // tm, 1, 2 * tm), h)


def _experts_kernel(te_ref, xs_ref, wg_ref, wu_ref, wd_ref, ys_ref, xb_ref, acc_ref):
    del te_ref
    f = pl.program_id(1)

    @pl.when(f == 0)
    def _():
        xb_ref[...] = xs_ref[...].astype(BF16)
        acc_ref[...] = jnp.zeros_like(acc_ref)

    xb = xb_ref[...]
    gate = jnp.dot(xb, wg_ref[...], preferred_element_type=F32)
    up = jnp.dot(xb, wu_ref[...], preferred_element_type=F32)
    hidden = (jax.nn.silu(gate) * up).astype(BF16)
    acc_ref[...] += jnp.dot(hidden, wd_ref[...], preferred_element_type=F32)

    @pl.when(f == pl.num_programs(1) - 1)
    def _():
        ys_ref[...] = acc_ref[...]


def _experts(xs, tile_expert, w_gate, w_up, w_down, tile, tf=1792):
    n_rows, d = xs.shape
    dff = w_gate.shape[2]
    grid_spec = pltpu.PrefetchScalarGridSpec(
        num_scalar_prefetch=1,
        grid=(n_rows // tile, dff // tf),
        in_specs=[pl.BlockSpec((tile, d), lambda i, f, te: (i, 0)),
                  pl.BlockSpec((None, d, tf), lambda i, f, te: (te[i], 0, f)),
                  pl.BlockSpec((None, d, tf), lambda i, f, te: (te[i], 0, f)),
                  pl.BlockSpec((None, tf, d), lambda i, f, te: (te[i], f, 0))],
        out_specs=pl.BlockSpec((tile, d), lambda i, f, te: (i, 0)),
        scratch_shapes=[pltpu.VMEM((tile, d), BF16), pltpu.VMEM((tile, d), F32)])
    return pl.pallas_call(
        _experts_kernel,
        grid_spec=grid_spec,
        out_shape=jax.ShapeDtypeStruct((n_rows, d), F32),
        compiler_params=_cparams("arbitrary", "arbitrary"),
        name="moe_experts",
    )(tile_expert, xs, w_gate.astype(BF16), w_up.astype(BF16), w_down.astype(BF16))


def _combine_kernel(slots_ref, next_slots_ref, x_ref, gates_ref, ys_hbm, out_a, out_b, buf_ref, sems, *, tm, tiles_a):
    i = pl.program_id(0)

    def gather(slots, half):
        def issue(t, carry):
            for k in range(2):
                _row_copy(ys_hbm, slots[0, 2 * t + k], buf_ref.at[half, k], t, sems.at[half]).start()
            return carry
        lax.fori_loop(0, tm, issue, 0, unroll=4)

    @pl.when(i == 0)
    def _():
        gather(slots_ref, 0)

    @pl.when(i + 1 < pl.num_programs(0))
    def _():
        gather(next_slots_ref, (i + 1) % 2)

    half = i % 2
    for k in range(2):
        pltpu.make_async_copy(ys_hbm.at[pl.ds(0, tm)], buf_ref.at[half, k], sems.at[half]).wait()
    gates = gates_ref[...]
    out = x_ref[...] + gates[:, 0:1] * buf_ref[half, 0] + gates[:, 1:2] * buf_ref[half, 1]

    @pl.when(i < tiles_a)
    def _():
        out_a[...] = out

    @pl.when(i >= tiles_a)
    def _():
        out_b[...] = out


def _combine(x, ys, slots, gates, rows_a, tm=256):
    t, d = x.shape
    tok = pl.BlockSpec((tm, d), lambda i: (i, 0))
    n_tiles = t // tm
    tiles_a = rows_a // tm
    slots3 = slots.reshape(n_tiles, 1, 2 * tm)
    out_a = pl.BlockSpec((tm, d), lambda i: (jnp.minimum(i, tiles_a - 1), 0))
    out_b = pl.BlockSpec((tm, d), lambda i: (jnp.maximum(i - tiles_a, 0), 0))
    return pl.pallas_call(
        functools.partial(_combine_kernel, tm=tm, tiles_a=tiles_a),
        grid=(n_tiles,),
        in_specs=[pl.BlockSpec((None, 1, 2 * tm), lambda i: (i, 0, 0), memory_space=pltpu.SMEM),
                  pl.BlockSpec((None, 1, 2 * tm), lambda i: (jnp.minimum(i + 1, n_tiles - 1), 0, 0),
                               memory_space=pltpu.SMEM),
                  tok, pl.BlockSpec((tm, 2), lambda i: (i, 0)), pl.BlockSpec(memory_space=pl.ANY)],
        out_specs=[out_a, out_b],
        out_shape=[jax.ShapeDtypeStruct((rows_a, d), F32), jax.ShapeDtypeStruct((t - rows_a, d), F32)],
        scratch_shapes=[pltpu.VMEM((2, 2, tm, d), F32), pltpu.SemaphoreType.DMA((2,))],
        compiler_params=_cparams("arbitrary"),
        name="moe_combine",
    )(slots3, slots3, x, gates, ys)


def _moe(x, gain, w_router, w_gate, w_up, w_down, rows_a, tile=512):
    n_experts = w_gate.shape[0]
    h, comb, rank, counts = _router(x, gain, w_router)
    slots, gates, tile_expert, holes, n_tiles = _route_plan(comb, rank, counts, n_experts, tile)
    xs = _dispatch(h, slots, holes, n_tiles * tile)
    ys = _experts(xs, tile_expert, w_gate, w_up, w_down, tile)
    return _combine(x, ys, slots, gates, rows_a)


def _lane_chunks(width):
    return [slice(c * LANES, (c + 1) * LANES) for c in range(width // LANES)]


def _store_dilated(dst_ref, stage_ref, value, dil):
    rows, width = value.shape
    if dil == 1:
        dst_ref[...] = value.astype(dst_ref.dtype)
        return
    for c, ls in enumerate(_lane_chunks(width)):
        stage_ref[c] = value[:, ls]
    for rho in range(dil):
        for c in range(width // LANES):
            dst_ref[:, pl.ds(rho * width + c * LANES, LANES)] = (
                stage_ref.at[c][pl.ds(rho, rows // dil, stride=dil), :].astype(dst_ref.dtype))


def _load_dilated(src_ref, stage_ref, base, rows, width, dil):
    if dil == 1:
        return src_ref[...].astype(F32)
    for rho in range(dil):
        for c in range(width // LANES):
            stage_ref.at[base + c][pl.ds(rho, rows // dil, stride=dil), :] = (
                src_ref[:, pl.ds(rho * width + c * LANES, LANES)].astype(F32))
    return jnp.concatenate([stage_ref[base + c] for c in range(width // LANES)], 1)


def _attn_in_kernel(x_ref, gain_ref, w_ref, hg_ref, e_ref, et_ref, *rest, dils):
    outs, (h_ref, stage_ref) = rest[:len(dils)], rest[len(dils):]
    g = pl.program_id(1)

    @pl.when(g == 0)
    def _():
        h_ref[...] = _rms(x_ref[...], gain_ref[...]).astype(BF16)

    n_qk = 2 * GROUP_WIDTH
    half = h_ref.shape[0] // 2
    ys = [jnp.dot(h_ref[pl.ds(r * half, half), :], w_ref[...], preferred_element_type=F32) for r in range(2)]
    sq = [_split_dot(yy[:, :n_qk] * yy[:, :n_qk], e_ref[...]) for yy in ys]
    ms = [_split_dot(x, et_ref[...]) * (1.0 / HEAD) for x in sq]
    qk = [yy[:, :n_qk] * lax.rsqrt(m + RMS_EPS) * hg_ref[...] for yy, m in zip(ys, ms)]
    y = jnp.concatenate([jnp.concatenate([a, yy[:, n_qk:]], 1) for a, yy in zip(qk, ys)], 0)
    for gi, dil in enumerate(dils):
        @pl.when(g == gi)
        def _(gi=gi, dil=dil):
            _store_dilated(outs[gi], stage_ref, y, dil)


def _attn_in(x, gain, w_in, q_gain, k_gain, tm=512):
    t, d = x.shape
    n_groups = q_gain.shape[0]
    gw = 3 * GROUP_WIDTH
    dils = tuple(dil for _, dil in ATTN_GROUPS)
    w = jnp.transpose(w_in.reshape(d, 3, n_groups, GROUP_WIDTH), (2, 0, 1, 3)).reshape(n_groups, d, gw).astype(BF16)
    head_gain = jnp.concatenate([jnp.tile(q_gain, (1, GROUP_HEADS)), jnp.tile(k_gain, (1, GROUP_HEADS))],
                                1).reshape(n_groups, 1, 2 * GROUP_WIDTH)
    e, et = _head_indicator(2 * GROUP_WIDTH)
    return pl.pallas_call(
        functools.partial(_attn_in_kernel, dils=dils),
        grid=(t // tm, n_groups),
        in_specs=[pl.BlockSpec((tm, d), lambda i, g: (i, 0)),
                  _const_spec((1, d)),
                  pl.BlockSpec((None, d, gw), lambda i, g: (g, 0, 0)),
                  pl.BlockSpec((None, 1, 2 * GROUP_WIDTH), lambda i, g: (g, 0, 0)),
                  _const_spec(e.shape), _const_spec(et.shape)],
        out_specs=[pl.BlockSpec((tm // dil, dil * gw), lambda i, g: (i, 0)) for dil in dils],
        out_shape=[jax.ShapeDtypeStruct((t // dil, dil * gw), BF16) for dil in dils],
        scratch_shapes=[pltpu.VMEM((tm, d), BF16), pltpu.VMEM((gw // LANES, tm, LANES), F32)],
        compiler_params=_cparams("arbitrary", "arbitrary"),
        name="attn_in",
    )(x, gain.reshape(1, d), w, head_gain, e, et)


def _t5_bucket_np(rel):
    nb = REL_BUCKETS // 2
    max_exact = nb // 2
    n = np.abs(rel)
    large = max_exact + (np.log(np.maximum(n, 1).astype(np.float32) / max_exact)
                         / math.log(REL_MAX_DIST / max_exact) * (nb - max_exact)).astype(np.int32)
    large = np.minimum(large, nb - 1)
    return np.where(rel > 0, nb, 0) + np.where(n < max_exact, n, large)


def _bias_table(rel_bias, group, window, dil):
    side = window // (2 * dil)
    assert side == KV_HALO
    buckets = _t5_bucket_np(dil * np.arange(-side, side + 1))
    table = rel_bias.reshape(REL_BUCKETS, -1, GROUP_HEADS)[buckets, group].T
    full = jnp.full((8, Q_BLOCK + 2 * KV_HALO), NEG_INF, F32)
    return full.at[:GROUP_HEADS, :2 * side + 1].set(table)


def _attn_kernel(q_ref, kp_ref, kc_ref, kn_ref, vp_ref, vc_ref, vn_ref, table_ref, o_ref, lse_ref, bias_ref,
                 *, regions, dil):
    tp, sp, ss = regions
    nk = Q_BLOCK + 2 * KV_HALO

    @pl.when((pl.program_id(0) == 0) & (pl.program_id(1) == 0))
    def _():
        for h in range(GROUP_HEADS):
            row = jnp.broadcast_to(table_ref[h:h + 1, :], (Q_BLOCK, nk))
            bias_ref[h] = pltpu.roll(row, 0, 1, stride=1, stride_axis=0)

    lane = lax.broadcasted_iota(jnp.int32, (Q_BLOCK, LANES), 1)
    mine = [(lane < HEAD), (lane >= HEAD)]
    scale = HEAD ** -0.5
    q_all = q_ref[...]
    k_all = jnp.concatenate([kp_ref[...], kc_ref[...], kn_ref[...]], 0)
    v_all = jnp.concatenate([vp_ref[...], vc_ref[...], vn_ref[...]], 0)
    lanes_of = lambda pair: slice(pair * LANES, (pair + 1) * LANES)

    units = [(blk, pair, sub) for blk in range(Q_STEP // Q_BLOCK) for pair in range(GROUP_HEADS // 2)
             for sub in range(2)]
    valid = []
    for blk in range(Q_STEP // Q_BLOCK):
        row0 = pl.program_id(1) * Q_STEP + blk * Q_BLOCK
        seq = _seq_len_at(row0, tp // dil, sp // dil, ss // dil)
        seq_start = row0 - lax.rem(row0, seq)
        key_row = row0 - KV_HALO + lax.broadcasted_iota(jnp.int32, (1, nk), 1)
        valid.append((key_row >= seq_start) & (key_row < seq_start + seq))
    s_all = []
    for blk, pair, sub in units:
        qp = q_all[blk * Q_BLOCK:(blk + 1) * Q_BLOCK, lanes_of(pair)]
        qm = jnp.where(mine[sub], qp, jnp.zeros_like(qp))
        keys = k_all[blk * Q_BLOCK:blk * Q_BLOCK + nk, lanes_of(pair)]
        s = lax.dot_general(qm, keys, (_NT, ((), ())), preferred_element_type=F32)
        s_all.append(jnp.where(valid[blk], s * scale + bias_ref[2 * pair + sub], NEG_INF))
    m_all = [jnp.max(s, axis=-1, keepdims=True) for s in s_all]
    p_all = [jnp.exp(s - m) for s, m in zip(s_all, m_all)]
    den_all = [jnp.sum(p, axis=-1, keepdims=True) for p in p_all]
    o_all = [jnp.dot((p / den).astype(v_all.dtype), v_all[blk * Q_BLOCK:blk * Q_BLOCK + nk, lanes_of(pair)],
                     preferred_element_type=F32)
             for p, den, (blk, pair, _) in zip(p_all, den_all, units)]
    lse_all = [m + jnp.log(den) for m, den in zip(m_all, den_all)]
    for i in range(0, len(units), 2):
        blk, pair, _ = units[i]
        rows = pl.ds(blk * Q_BLOCK, Q_BLOCK)
        o_ref[rows, lanes_of(pair)] = jnp.where(mine[0], o_all[i], o_all[i + 1]).astype(o_ref.dtype)
        lse_ref[rows, lanes_of(pair)] = jnp.where(mine[0], lse_all[i], lse_all[i + 1])


def _attn_group(qkv_d, rel_bias, group, regions):
    window, dil = ATTN_GROUPS[group]
    rows = qkv_d.shape[0]
    parts = 3
    table = _bias_table(rel_bias, group, window, dil)
    halo_per_step = Q_STEP // KV_HALO
    last_halo = rows // KV_HALO - 1

    def cur(part):
        return pl.BlockSpec((Q_STEP, GROUP_WIDTH), lambda rho, m: (m, rho * parts + part))

    def prev(part):
        return pl.BlockSpec((KV_HALO, GROUP_WIDTH),
                            lambda rho, m: (jnp.maximum(m * halo_per_step - 1, 0), rho * parts + part))

    def nxt(part):
        return pl.BlockSpec((KV_HALO, GROUP_WIDTH),
                            lambda rho, m: (jnp.minimum((m + 1) * halo_per_step, last_halo), rho * parts + part))

    out_spec = pl.BlockSpec((Q_STEP, GROUP_WIDTH), lambda rho, m: (m, rho))
    return pl.pallas_call(
        functools.partial(_attn_kernel, regions=regions, dil=dil),
        grid=(dil, rows // Q_STEP),
        in_specs=[cur(0), prev(1), cur(1), nxt(1), prev(2), cur(2), nxt(2), _const_spec(table.shape)],
        out_specs=[out_spec, out_spec],
        out_shape=[jax.ShapeDtypeStruct((rows, dil * GROUP_WIDTH), BF16),
                   jax.ShapeDtypeStruct((rows, dil * GROUP_WIDTH), F32)],
        scratch_shapes=[pltpu.VMEM((GROUP_HEADS, Q_BLOCK, Q_BLOCK + 2 * KV_HALO), F32)],
        compiler_params=_cparams("arbitrary", "arbitrary"),
        name=f"attn_g{group}",
    )(qkv_d, qkv_d, qkv_d, qkv_d, qkv_d, qkv_d, qkv_d, table)


def _attn_out_kernel(o0, o1, o2, l0, l1, l2, x_ref, wo_ref, out_ref, stage_ref, *, dils):
    tm = x_ref.shape[0]
    chunks = GROUP_WIDTH // LANES
    os_ = [_load_dilated(r, stage_ref, 2 * g * chunks, tm, GROUP_WIDTH, dils[g]) for g, r in enumerate((o0, o1, o2))]
    lses = [_load_dilated(r, stage_ref, (2 * g + 1) * chunks, tm, GROUP_WIDTH, dils[g])
            for g, r in enumerate((l0, l1, l2))]
    mx = jnp.maximum(jnp.maximum(lses[0], lses[1]), lses[2])
    ws = [jnp.exp(l - mx) for l in lses]
    tot = ws[0] + ws[1] + ws[2]
    acc = x_ref[...]
    for g in range(len(dils)):
        scaled = (os_[g] * (ws[g] / tot)).astype(BF16)
        acc = acc + jnp.dot(scaled, wo_ref[g], preferred_element_type=F32)
    out_ref[...] = acc


def _attn_out(os_, lses, x, w_o, tm=512):
    t, d = x.shape
    n_groups = len(os_)
    dils = tuple(dil for _, dil in ATTN_GROUPS)
    grp = [pl.BlockSpec((tm // dil, dil * GROUP_WIDTH), lambda i: (i, 0)) for dil in dils]
    tok = pl.BlockSpec((tm, d), lambda i: (i, 0))
    wo = w_o.reshape(n_groups, GROUP_WIDTH, d).astype(BF16)
    return pl.pallas_call(
        functools.partial(_attn_out_kernel, dils=dils),
        grid=(t // tm,),
        in_specs=grp + grp + [tok, _const_spec(wo.shape)],
        out_specs=tok,
        out_shape=jax.ShapeDtypeStruct((t, d), F32),
        scratch_shapes=[pltpu.VMEM((2 * n_groups * (GROUP_WIDTH // LANES), tm, LANES), F32)],
        compiler_params=_cparams("arbitrary"),
        name="attn_out",
    )(*os_, *lses, x, wo)


def kernel(x_prompt, x_sample, norm_mix, norm_ffn, rwkv_mu, rwkv_w_rkv, rwkv_w0, rwkv_w1, rwkv_w2, rwkv_a0, rwkv_a1, rwkv_a2, rwkv_g1, rwkv_g2, rwkv_k_k, rwkv_k_a, rwkv_r_k, rwkv_ln_g, rwkv_ln_b, rwkv_w_o, attn_w_in, attn_q_gain, attn_k_gain, attn_w_o, rel_bias, ffn_w_gate, ffn_w_up, ffn_w_down, moe_router, moe_w_gate, moe_w_up, moe_w_down):
    d = x_prompt.shape[-1]
    tp = x_prompt.shape[0] * x_prompt.shape[1]
    regions = (tp, x_prompt.shape[1], x_sample.shape[1])
    x = jnp.concatenate([x_prompt.reshape(-1, d), x_sample.reshape(-1, d)], 0)
    depth = norm_mix.shape[0]
    for i in range(depth):
        j = i // 2
        if i % 2 == 0:
            r, v, kk, g, bonus, logw, kdir, bdir = _rwkv_pre(
                x, regions, norm_mix[i], rwkv_mu[j], rwkv_w_rkv[j], rwkv_w0[j], rwkv_w1[j], rwkv_w2[j],
                rwkv_a0[j], rwkv_a1[j], rwkv_a2[j], rwkv_g1[j], rwkv_g2[j], rwkv_k_k[j], rwkv_k_a[j], rwkv_r_k[j])
            yf = _wkv(r, v, kk, logw, kdir, bdir, regions, reverse=False)
            yb = _wkv(r, v, kk, logw, kdir, bdir, regions, reverse=True)
            x = _rwkv_post(yf, yb, bonus, g, x, rwkv_ln_g[j], rwkv_ln_b[j], rwkv_w_o[j])
            x = _ffn(x, norm_ffn[i], ffn_w_gate[j], ffn_w_up[j], ffn_w_down[j])
        else:
            qkv = _attn_in(x, norm_mix[i], attn_w_in[j], attn_q_gain[j], attn_k_gain[j])
            outs = [_attn_group(qkv[g], rel_bias, g, regions) for g in range(len(ATTN_GROUPS))]
            x = _attn_out([o for o, _ in outs], [l for _, l in outs], x, attn_w_o[j])
            x = _moe(x, norm_ffn[i], moe_router[j], moe_w_gate[j], moe_w_up[j], moe_w_down[j], rows_a=tp)
            if i + 1 < depth:
                x = jnp.concatenate(x, 0)
    if not isinstance(x, (list, tuple)):
        x = (x[:tp], x[tp:])
    return (x[0].reshape(x_prompt.shape), x[1].reshape(x_sample.shape))
```

```python
import functools
import math

import numpy as np
import jax
import jax.numpy as jnp
from jax import lax
from jax.experimental import pallas as pl
from jax.experimental.pallas import tpu as pltpu

F32 = jnp.float32
BF16 = jnp.bfloat16

LANES = 128
VMEM_LIMIT_BYTES = 56 * 1024 * 1024

HEAD = 64
RMS_EPS = 1e-6
GN_EPS = 64e-5
ATTN_GROUPS = ((128, 1), (512, 4), (2048, 16))
GROUP_HEADS = 6
GROUP_WIDTH = GROUP_HEADS * HEAD
REL_BUCKETS = 32
REL_MAX_DIST = 1024
Q_BLOCK = 128
Q_STEP = 512
KV_HALO = 64
SCAN_CHUNK = 64
NEG_INF = -1e30


def _cparams(*sem):
    return pltpu.CompilerParams(dimension_semantics=sem, vmem_limit_bytes=VMEM_LIMIT_BYTES)


def _const_spec(shape):
    nd = len(shape)
    return pl.BlockSpec(shape, lambda *_: (0,) * nd, pipeline_mode=pl.Buffered(1))


def _split_dot(x, w):
    hi = x.astype(BF16)
    lo = (x - hi.astype(F32)).astype(BF16)
    return (jnp.dot(hi, w, preferred_element_type=F32)
            + jnp.dot(lo, w, preferred_element_type=F32))


def _head_sum(x, e_ref, et_ref):
    return _split_dot(_split_dot(x, e_ref[...]), et_ref[...])


def _rms(x, gain):
    return x * lax.rsqrt(jnp.mean(x * x, axis=-1, keepdims=True) + RMS_EPS) * gain


def _seq_len_at(row0, tp, sp, ss):
    return jnp.where(row0 < tp, sp, ss)


def _head_indicator(width):
    e = np.zeros((width, LANES), np.float32)
    e[np.arange(width), np.arange(width) // HEAD] = 1.0
    return jnp.asarray(e, BF16), jnp.asarray(e.T, BF16)


def _pick(i_block, blocks_a, a_ref, b_ref):
    return jnp.where(i_block < blocks_a, a_ref[...], b_ref[...])


def _two_part_specs(shape, index_fn, blocks_a, blocks_b):
    return [pl.BlockSpec(shape, lambda i: (jnp.clip(index_fn(i), 0, blocks_a - 1), 0)),
            pl.BlockSpec(shape, lambda i: (jnp.clip(index_fn(i) - blocks_a, 0, blocks_b - 1), 0))]


def _rwkv_pre_kernel(xa_ref, xb_ref, xpa_ref, xpb_ref, xna_ref, xnb_ref, gain_ref, mu_ref, wrkv_ref, wl1_ref, w2_ref, a2_ref,
                     g2_ref, w0_ref, a0_ref, kk_gain_ref, ka_ref, rk_ref, e_ref, et_ref,
                     r_out, v_out, kk_out, g_out, bonus_out, logw_out, kdir_out, bdir_out,
                     *, tm, regions):
    tp, sp, ss = regions
    d = xa_ref.shape[-1]
    i = pl.program_id(0)
    row0 = i * tm
    seq = _seq_len_at(row0, tp, sp, ss)
    at_start = lax.rem(row0, seq) == 0
    at_end = lax.rem(row0 + tm, seq) == 0

    gain = gain_ref[...]
    sub = tm // 8
    h = _rms(_pick(i, tp // tm, xa_ref, xb_ref), gain)
    hp = _rms(_pick(i * sub - 1, tp // 8, xpa_ref, xpb_ref), gain)[7:8, :] * jnp.where(at_start, 0.0, 1.0)
    hn = _rms(_pick((i + 1) * sub, tp // 8, xna_ref, xnb_ref), gain)[0:1, :] * jnp.where(at_end, 0.0, 1.0)
    rows = lax.broadcasted_iota(jnp.int32, (tm, d), 0)
    h_prev = jnp.where(rows == 0, hp, pltpu.roll(h, 1, 0))
    h_next = jnp.where(rows == tm - 1, hn, pltpu.roll(h, tm - 1, 0))
    xx = 0.5 * (h_prev + h_next) - h

    def half_tile(rs):
        hh, xh = h[rs], xx[rs]
        mixed = lambda n: (hh + xh * mu_ref[n:n + 1, :]).astype(BF16)
        r = jnp.dot(mixed(0), wrkv_ref[0], preferred_element_type=F32)
        k = jnp.dot(mixed(1), wrkv_ref[1], preferred_element_type=F32)
        v = jnp.dot(mixed(2), wrkv_ref[2], preferred_element_type=F32)
        r_out[rs, :] = r
        v_out[rs, :] = v
        yield
        t_w = jnp.tanh(jnp.dot(mixed(3), wl1_ref[0], preferred_element_type=F32))
        t_a = jnp.dot(mixed(4), wl1_ref[1], preferred_element_type=F32)
        t_g = jax.nn.sigmoid(jnp.dot(mixed(5), wl1_ref[2], preferred_element_type=F32))
        yield
        lw = w0_ref[...] + jnp.dot(t_w.astype(BF16), w2_ref[...], preferred_element_type=F32)
        a = jax.nn.sigmoid(a0_ref[...] + jnp.dot(t_a.astype(BF16), a2_ref[...], preferred_element_type=F32))
        g_out[rs, :] = jnp.dot(t_g.astype(BF16), g2_ref[...], preferred_element_type=F32)
        logw = -math.exp(-0.5) * jax.nn.sigmoid(lw)
        for z in range(2):
            logw_out[z, rs, :] = logw[:, z * d:(z + 1) * d]
        yield
        kk = k * kk_gain_ref[...]
        kk_sq = _split_dot(kk * kk, e_ref[...])
        rk_sum = _split_dot(r * k * rk_ref[...], e_ref[...])
        yield
        kk = kk * lax.rsqrt(jnp.maximum(_split_dot(kk_sq, et_ref[...]), 1e-12))
        bonus_out[rs, :] = _split_dot(rk_sum, et_ref[...]) * v
        kk_out[rs, :] = kk
        ka = ka_ref[...]
        for z in range(2):
            a_z = a[:, z * d:(z + 1) * d]
            kdir_out[z, rs, :] = k * (1.0 + (a_z - 1.0) * ka)
            bdir_out[z, rs, :] = kk * a_z

    live = [half_tile(slice(i * (tm // 2), (i + 1) * (tm // 2))) for i in range(2)]
    while live:
        live = [gen for gen in live if next(gen, StopIteration) is not StopIteration]


def _block_diag2(m):
    z = jnp.zeros_like(m[0])
    return jnp.concatenate([jnp.concatenate([m[0], z], 1), jnp.concatenate([z, m[1]], 1)], 0)


def _rwkv_pre(xa, xb, regions, gain, mu, w_rkv, w0, w1, w2, a0, a1, a2, g1, g2, k_k, k_a, r_k, tm=256):
    (ta, d), tb = xa.shape, xb.shape[0]
    t = ta + tb
    e, et = _head_indicator(d)
    wl1 = jnp.stack([jnp.concatenate([w1[0], w1[1]], 1), jnp.concatenate([a1[0], a1[1]], 1), g1]).astype(BF16)
    w2c = _block_diag2(w2).astype(BF16)
    a2c = _block_diag2(a2).astype(BF16)
    row = lambda p: p.reshape(1, -1).astype(F32)
    tok = pl.BlockSpec((tm, d), lambda i: (i, 0))
    x_specs = (_two_part_specs((tm, d), lambda i: i, ta // tm, tb // tm)
               + _two_part_specs((8, d), lambda i: i * (tm // 8) - 1, ta // 8, tb // 8)
               + _two_part_specs((8, d), lambda i: (i + 1) * (tm // 8), ta // 8, tb // 8))
    dir_tok = pl.BlockSpec((2, tm, d), lambda i: (0, i, 0))
    consts = [row(gain), mu.astype(F32), w_rkv.astype(BF16), wl1, w2c, a2c, g2.astype(BF16),
              row(w0), row(a0), row(k_k), row(k_a), row(r_k), e, et]
    tok_shape = jax.ShapeDtypeStruct((t, d), F32)
    dir_shape = jax.ShapeDtypeStruct((2, t, d), F32)
    return pl.pallas_call(
        functools.partial(_rwkv_pre_kernel, tm=tm, regions=regions),
        grid=(t // tm,),
        in_specs=x_specs + [_const_spec(c.shape) for c in consts],
        out_specs=[tok] * 5 + [dir_tok] * 3,
        out_shape=[tok_shape] * 5 + [dir_shape] * 3,
        compiler_params=_cparams("arbitrary"),
        name="rwkv_pre",
    )(xa, xb, xa, xb, xa, xb, *consts)


_NN = ((1,), (0,))
_NT = ((1,), (1,))
_TN = ((0,), (0,))


def _mm(a, b, dims=_NN):
    return lax.dot_general(a.astype(BF16), b.astype(BF16), (dims, ((), ())), preferred_element_type=F32)


def _cumsum_rows(x, reverse):
    n = x.shape[0]
    row = lax.broadcasted_iota(jnp.int32, x.shape, 0)
    s = 1
    while s < n:
        if reverse:
            x = x + jnp.where(row < n - s, pltpu.roll(x, n - s, 0), 0.0)
        else:
            x = x + jnp.where(row >= s, pltpu.roll(x, s, 0), 0.0)
        s *= 2
    return x


def _wkv_kernel(r_ref, v_ref, kk_ref, logw_ref, kdir_ref, bdir_ref, y_ref, state_ref,
                *, rb, regions, reverse, group_chunks=2):
    tp, sp, ss = regions
    c = SCAN_CHUNK
    npairs = r_ref.shape[-1] // LANES
    nsteps = pl.num_programs(1)
    step = pl.program_id(1)
    blk = (nsteps - 1 - step) if reverse else step
    row0 = blk * rb
    seq = _seq_len_at(row0, tp, sp, ss)
    first = (lax.rem(row0 + rb, seq) == 0) if reverse else (lax.rem(row0, seq) == 0)

    @pl.when(first)
    def _():
        state_ref[...] = jnp.zeros_like(state_ref)

    ti2 = lax.broadcasted_iota(jnp.int32, (c, 2 * c), 0)
    tj2 = lax.broadcasted_iota(jnp.int32, (c, 2 * c), 1) & (c - 1)
    ti4 = lax.broadcasted_iota(jnp.int32, (c, 4 * c), 0)
    tj4 = lax.broadcasted_iota(jnp.int32, (c, 4 * c), 1) & (c - 1)
    incl4 = (tj4 >= ti4) if reverse else (tj4 <= ti4)
    strict2 = (tj2 > ti2) if reverse else (tj2 < ti2)
    lane = lax.broadcasted_iota(jnp.int32, (c, LANES), 1)
    head0 = lane < HEAD
    rr = lax.broadcasted_iota(jnp.int32, (LANES, LANES), 0)
    cc = lax.broadcasted_iota(jnp.int32, (LANES, LANES), 1)
    same_head = (rr < HEAD) == (cc < HEAD)

    col4 = lax.broadcasted_iota(jnp.int32, (c, 4 * c), 1)
    eye4 = ((col4 & (c - 1)) == ti4).astype(F32)

    def block_diag4(m):
        return jnp.concatenate([jnp.where((col4 >= h * c) & (col4 < (h + 1) * c), m, 0.0) for h in range(4)], 0)

    def stack_heads(m):
        return jnp.concatenate([jnp.where(head0, m, 0.0), jnp.where(head0, 0.0, m)], 0)

    order = list(range(rb // c - 1, -1, -1) if reverse else range(rb // c))
    each = lambda fn, *lists: [fn(*xs) for xs in zip(*lists)]

    def prepare(chunks, out):
        units = [(pl.ds(ci * c, c), pl.ds(p * LANES, LANES)) for ci in chunks for p in range(npairs)]
        lw = [logw_ref[u] for u in units]
        cum = each(lambda x: _cumsum_rows(x, reverse), lw)
        total = each(lambda x: jnp.sum(x, axis=0, keepdims=True), lw)
        lhs = [jnp.concatenate([r_ref[u] * jnp.exp(cm), kk_ref[u] * jnp.exp(cm - x)], 0).astype(BF16)
               for u, cm, x in zip(units, cum, lw)]
        rhs = []
        kb_tail = []
        for u, cm, tt in zip(units, cum, total):
            kd, bd = kdir_ref[u], bdir_ref[u]
            inv = jnp.exp(-cm)
            tail = jnp.exp(tt - cm)
            rhs.append(jnp.concatenate([stack_heads(kd * inv), stack_heads(bd * inv)], 0).astype(BF16))
            kb_tail.append(jnp.concatenate([kd * tail, -(bd * tail)], 0).astype(BF16))
        yield
        q = each(lambda a, b: _mm(a, b, _NT), lhs, rhs)
        a_out = [jnp.where(incl4, x[0:c], 0.0).astype(BF16) for x in q]
        a_kk = [jnp.where(strict2, x[c:2 * c, 0:2 * c], 0.0).astype(BF16) for x in q]
        low = [jnp.where(strict2, x[c:2 * c, 2 * c:4 * c], 0.0) for x in q]
        pw = [-jnp.concatenate([low[i], low[i + 1]], 1) for i in range(0, len(units), 2)]
        inv_cat = [eye4 + x for x in pw]
        yield
        pw = each(lambda x: _mm(x, block_diag4(x)), pw)
        v_all = [v_ref[u] for u in units]
        v_bd = [stack_heads(x).astype(BF16) for x in v_all]
        akv = each(_mm, a_kk, v_bd)
        for _ in range(int(math.log2(c)) - 2):
            yield
            both = each(lambda t, x: _mm(jnp.concatenate([t, x], 0), block_diag4(x)), inv_cat, pw)
            inv_cat = each(lambda t, b: t + b[0:c], inv_cat, both)
            pw = [b[c:2 * c] for b in both]
        yield
        inv_cat = each(lambda t, x: t + _mm(t, block_diag4(x)), inv_cat, pw)
        inv_l = [stack_heads(t[:, h * 2 * c:(h + 1) * 2 * c]).astype(BF16) for t in inv_cat for h in range(2)]
        out.update(units=units, lhs=lhs, kb_tail=kb_tail, a_out=a_out, akv=akv, inv_l=inv_l, v_all=v_all,
                   v_bd=v_bd, decay=[jnp.exp(x) for x in total])

    def advance(group, k, states):
        pre = {name: vals[k * npairs:(k + 1) * npairs] for name, vals in group.items()}
        sk = each(lambda a, st: _mm(a, st, _NT), pre["lhs"], states)
        yield
        u_bd = each(lambda t, s_, kv: _mm(t, stack_heads(s_[c:2 * c] + kv)), pre["inv_l"], sk, pre["akv"])
        yield
        y = each(lambda s_, a, vb, ub: s_[0:c] + _mm(a, jnp.concatenate([vb, (-ub).astype(BF16)], 0)),
                 sk, pre["a_out"], pre["v_bd"], u_bd)
        for u, yy in zip(pre["units"], y):
            y_ref[u] = yy
        upd = each(lambda vv, ub, kb: _mm(jnp.concatenate([vv, ub[0:c] + ub[c:2 * c]], 0), kb, _TN),
                   pre["v_all"], u_bd, pre["kb_tail"])
        states[:] = each(lambda st, dc, up: st * dc + jnp.where(same_head, up, 0.0), states, pre["decay"], upd)

    def run(*gens):
        live = [g for g in gens if g is not None]
        while live:
            live = [g for g in live if next(g, StopIteration) is not StopIteration]

    states = [state_ref[p] for p in range(npairs)]
    groups = [order[i:i + group_chunks] for i in range(0, len(order), group_chunks)]
    prepared = {}
    run(prepare(groups[0], prepared))
    for gi, chunks in enumerate(groups):
        nxt = {}
        following = prepare(groups[gi + 1], nxt) if gi + 1 < len(groups) else None

        def advance_all(prepared=prepared, chunks=chunks):
            for k in range(len(chunks)):
                yield from advance(prepared, k, states)
                yield

        run(following, advance_all())
        prepared = nxt
    for p in range(npairs):
        state_ref[p] = states[p]


def _wkv(r, v, kk, logw, kdir, bdir, regions, reverse, rb=512, width=1024):
    t, d = r.shape
    z = 1 if reverse else 0
    nsteps = t // rb

    def row_block(s):
        return (nsteps - 1 - s) if reverse else s

    tok = pl.BlockSpec((rb, width), lambda j, s: (row_block(s), j))
    dir_tok = pl.BlockSpec((None, rb, width), lambda j, s: (z, row_block(s), j))
    return pl.pallas_call(
        functools.partial(_wkv_kernel, rb=rb, regions=regions, reverse=reverse),
        grid=(d // width, nsteps),
        in_specs=[tok, tok, tok, dir_tok, dir_tok, dir_tok],
        out_specs=tok,
        out_shape=jax.ShapeDtypeStruct((t, d), F32),
        scratch_shapes=[pltpu.VMEM((width // LANES, LANES, LANES), F32)],
        compiler_params=_cparams("arbitrary", "arbitrary"),
        name="wkv_bwd" if reverse else "wkv_fwd",
    )(r, v, kk, logw, kdir, bdir)


def _rwkv_post_kernel(yf_ref, yb_ref, bonus_ref, g_ref, xa_ref, xb_ref, lng_ref, lnb_ref, wo_ref, e_ref, et_ref,
                      out_ref, *, tiles_a):
    y = yf_ref[...] + yb_ref[...]
    mean = _head_sum(y, e_ref, et_ref) * (1.0 / HEAD)
    dlt = y - mean
    var = _head_sum(dlt * dlt, e_ref, et_ref) * (1.0 / HEAD)
    yn = dlt * lax.rsqrt(var + GN_EPS) * lng_ref[...] + lnb_ref[...]
    mixed = ((yn + bonus_ref[...]) * g_ref[...]).astype(BF16)
    x = _pick(pl.program_id(0), tiles_a, xa_ref, xb_ref)
    out_ref[...] = x + jnp.dot(mixed, wo_ref[...], preferred_element_type=F32)


def _rwkv_post(yf, yb, bonus, g, xa, xb, ln_g, ln_b, w_o, tm=256):
    (ta, d), tb = xa.shape, xb.shape[0]
    t = ta + tb
    e, et = _head_indicator(d)
    tok = pl.BlockSpec((tm, d), lambda i: (i, 0))
    consts = [ln_g.reshape(1, d), ln_b.reshape(1, d), w_o.astype(BF16), e, et]
    return pl.pallas_call(
        functools.partial(_rwkv_post_kernel, tiles_a=ta // tm),
        grid=(t // tm,),
        in_specs=[tok] * 4 + _two_part_specs((tm, d), lambda i: i, ta // tm, tb // tm)
                 + [_const_spec(c.shape) for c in consts],
        out_specs=tok,
        out_shape=jax.ShapeDtypeStruct((t, d), F32),
        compiler_params=_cparams("arbitrary"),
        name="rwkv_post",
    )(yf, yb, bonus, g, xa, xb, *consts)


def _ffn_kernel(x_ref, gain_ref, wg_ref, wu_ref, wd_ref, out_ref, h_ref, acc_ref):
    f = pl.program_id(1)

    @pl.when(f == 0)
    def _():
        h_ref[...] = _rms(x_ref[...], gain_ref[...]).astype(BF16)
        acc_ref[...] = jnp.zeros_like(acc_ref)

    h = h_ref[...]
    gate = jnp.dot(h, wg_ref[...], preferred_element_type=F32)
    up = jnp.dot(h, wu_ref[...], preferred_element_type=F32)
    hidden = (jax.nn.silu(gate) * up).astype(BF16)
    acc_ref[...] += jnp.dot(hidden, wd_ref[...], preferred_element_type=F32)

    @pl.when(f == pl.num_programs(1) - 1)
    def _():
        out_ref[...] = x_ref[...] + acc_ref[...]


def _ffn(x, gain, w_gate, w_up, w_down, tm=512, tf=2816):
    t, d = x.shape
    dff = w_gate.shape[1]
    return pl.pallas_call(
        _ffn_kernel,
        grid=(t // tm, dff // tf),
        in_specs=[pl.BlockSpec((tm, d), lambda i, f: (i, 0)),
                  _const_spec((1, d)),
                  pl.BlockSpec((d, tf), lambda i, f: (0, f)),
                  pl.BlockSpec((d, tf), lambda i, f: (0, f)),
                  pl.BlockSpec((tf, d), lambda i, f: (f, 0))],
        out_specs=pl.BlockSpec((tm, d), lambda i, f: (i, 0)),
        out_shape=jax.ShapeDtypeStruct((t, d), F32),
        scratch_shapes=[pltpu.VMEM((tm, d), BF16), pltpu.VMEM((tm, d), F32)],
        compiler_params=_cparams("arbitrary", "arbitrary"),
        name="ffn",
    )(x, gain.reshape(1, d), w_gate.astype(BF16), w_up.astype(BF16), w_down.astype(BF16))


def _router_kernel(x_ref, gain_ref, wr_ref, tri_ref, h_out, comb_out, rank_out, count_out, carry_ref,
                   *, n_experts):
    @pl.when(pl.program_id(0) == 0)
    def _():
        carry_ref[...] = jnp.zeros_like(carry_ref)

    h = _rms(x_ref[...], gain_ref[...])
    h_out[...] = h
    logits = jnp.dot(h.astype(BF16), wr_ref[...], preferred_element_type=F32)
    lane = lax.broadcasted_iota(jnp.int32, logits.shape, 1)
    logits = jnp.where(lane < n_experts, logits, -jnp.inf)
    m1 = jnp.max(logits, axis=-1, keepdims=True)
    i1 = jnp.min(jnp.where(logits == m1, lane, LANES), axis=-1, keepdims=True)
    rest = jnp.where(lane == i1, -jnp.inf, logits)
    m2 = jnp.max(rest, axis=-1, keepdims=True)
    i2 = jnp.min(jnp.where(rest == m2, lane, LANES), axis=-1, keepdims=True)
    e2 = jnp.exp(m2 - m1)
    g1 = 1.0 / (1.0 + e2)
    g2 = e2 / (1.0 + e2)
    comb_out[...] = jnp.where(lane == i1, g1, 0.0) + jnp.where(lane == i2, g2, 0.0)
    sel = jnp.where((lane == i1) | (lane == i2), 1.0, 0.0)
    earlier = carry_ref[0:1, :] + jnp.dot(tri_ref[...], sel.astype(BF16), preferred_element_type=F32)
    rank_out[...] = sel * (earlier + 1.0)
    carry_ref[...] = carry_ref[...] + jnp.sum(sel, axis=0, keepdims=True)
    count_out[...] = carry_ref[...]


def _router(x, gain, w_router, tm=512):
    t, d = x.shape
    n_experts = w_router.shape[1]
    wr = jnp.zeros((d, LANES), F32).at[:, :n_experts].set(w_router).astype(BF16)
    tri = jnp.asarray(np.tril(np.ones((tm, tm), np.float32), -1), BF16)
    tok = pl.BlockSpec((tm, d), lambda i: (i, 0))
    lanes = pl.BlockSpec((tm, LANES), lambda i: (i, 0))
    return pl.pallas_call(
        functools.partial(_router_kernel, n_experts=n_experts),
        grid=(t // tm,),
        in_specs=[tok, _const_spec((1, d)), _const_spec((d, LANES)), _const_spec((tm, tm))],
        out_specs=[tok, lanes, lanes, pl.BlockSpec((8, LANES), lambda i: (0, 0))],
        out_shape=[jax.ShapeDtypeStruct((t, d), F32), jax.ShapeDtypeStruct((t, LANES), F32),
                   jax.ShapeDtypeStruct((t, LANES), F32), jax.ShapeDtypeStruct((8, LANES), F32)],
        scratch_shapes=[pltpu.VMEM((8, LANES), F32)],
        compiler_params=_cparams("arbitrary"),
        name="moe_router",
    )(x, gain.reshape(1, d), wr, tri)


def _route_plan(comb, rank, counts, n_experts, tile):
    t = comb.shape[0]
    n_tiles = (2 * t) // tile + n_experts
    cnt = counts[0, :n_experts].astype(jnp.int32)
    padded = ((cnt + tile - 1) // tile) * tile
    ends = jnp.cumsum(padded)
    starts = jnp.zeros((LANES,), jnp.int32).at[:n_experts].set(ends - padded)
    slot = jnp.where(rank > 0, starts[None, :] + rank.astype(jnp.int32) - 1, -1)
    slot_hi = jnp.max(slot, axis=1)
    slot_lo = jnp.min(jnp.where(slot >= 0, slot, jnp.iinfo(jnp.int32).max), axis=1)
    gate_lo = jnp.sum(jnp.where(slot == slot_lo[:, None], comb, 0.0), axis=1)
    gate_hi = jnp.sum(jnp.where(slot == slot_hi[:, None], comb, 0.0), axis=1)
    slots = jnp.stack([slot_lo, slot_hi], 1)
    gates = jnp.stack([gate_lo, gate_hi], 1)
    tile_expert = jnp.sum(jnp.arange(n_tiles, dtype=jnp.int32)[:, None] * tile >= ends[None, :], axis=1)
    holes = jnp.concatenate([ends - padded + cnt, ends[-1:], ends, jnp.full((1,), n_tiles * tile, jnp.int32)])
    return slots, gates, jnp.minimum(tile_expert, n_experts - 1).astype(jnp.int32), holes.astype(jnp.int32), n_tiles


def _row_copy(src, src_row, dst, dst_row, sem):
    return pltpu.make_async_copy(src.at[pl.ds(src_row, 1)], dst.at[pl.ds(dst_row, 1)], sem)


def _dispatch_kernel(holes_ref, slots_ref, h_ref, xs_hbm, zero_ref, sem, zero_sem, *, tm, n_holes):
    n_ranges = holes_ref.shape[0] // 2

    @pl.when(pl.program_id(0) == 0)
    def _():
        zero_ref[...] = jnp.zeros_like(zero_ref)
        for e in range(n_ranges):
            def fill(row, carry):
                _row_copy(zero_ref, 0, xs_hbm, row, zero_sem).start()
                return carry
            lax.fori_loop(holes_ref[e], holes_ref[n_ranges + e], fill, 0)
        pltpu.make_async_copy(xs_hbm.at[pl.ds(0, n_holes)], xs_hbm.at[pl.ds(0, n_holes)], zero_sem).wait()

    def issue(i, carry):
        for k in range(2):
            _row_copy(h_ref, i, xs_hbm, slots_ref[0, 2 * i + k], sem).start()
        return carry

    lax.fori_loop(0, tm, issue, 0, unroll=4)
    for _ in range(2):
        pltpu.make_async_copy(h_ref, xs_hbm.at[pl.ds(0, tm)], sem).wait()


def _dispatch(h, slots, holes, n_rows, tm=512):
    t, d = h.shape
    grid_spec = pltpu.PrefetchScalarGridSpec(
        num_scalar_prefetch=1,
        grid=(t // tm,),
        in_specs=[pl.BlockSpec((None, 1, 2 * tm), lambda i, holes: (i, 0, 0), memory_space=pltpu.SMEM),
                  pl.BlockSpec((tm, d), lambda i, holes: (i, 0))],
        out_specs=pl.BlockSpec(memory_space=pl.ANY),
        scratch_shapes=[pltpu.VMEM((8, d), h.dtype), pltpu.SemaphoreType.DMA(()), pltpu.SemaphoreType.DMA(())])
    return pl.pallas_call(
        functools.partial(_dispatch_kernel, tm=tm, n_holes=n_rows - 2 * t),
        grid_spec=grid_spec,
        out_shape=jax.ShapeDtypeStruct((n_rows, d), h.dtype),
        compiler_params=pltpu.CompilerParams(dimension_semantics=("arbitrary",), has_side_effects=True),
        name="moe_dispatch",
    )(holes, slots.reshape(t // tm, 1, 2 * tm), h)


def _experts_kernel(te_ref, xs_ref, wg_ref, wu_ref, wd_ref, ys_ref, xb_ref, acc_ref):
    del te_ref
    f = pl.program_id(1)

    @pl.when(f == 0)
    def _():
        xb_ref[...] = xs_ref[...].astype(BF16)
        acc_ref[...] = jnp.zeros_like(acc_ref)

    xb = xb_ref[...]
    gate = jnp.dot(xb, wg_ref[...], preferred_element_type=F32)
    up = jnp.dot(xb, wu_ref[...], preferred_element_type=F32)
    hidden = (jax.nn.silu(gate) * up).astype(BF16)
    acc_ref[...] += jnp.dot(hidden, wd_ref[...], preferred_element_type=F32)

    @pl.when(f == pl.num_programs(1) - 1)
    def _():
        ys_ref[...] = acc_ref[...]


def _experts(xs, tile_expert, w_gate, w_up, w_down, tile, tf=1792):
    n_rows, d = xs.shape
    dff = w_gate.shape[2]
    grid_spec = pltpu.PrefetchScalarGridSpec(
        num_scalar_prefetch=1,
        grid=(n_rows // tile, dff // tf),
        in_specs=[pl.BlockSpec((tile, d), lambda i, f, te: (i, 0)),
                  pl.BlockSpec((None, d, tf), lambda i, f, te: (te[i], 0, f)),
                  pl.BlockSpec((None, d, tf), lambda i, f, te: (te[i], 0, f)),
                  pl.BlockSpec((None, tf, d), lambda i, f, te: (te[i], f, 0))],
        out_specs=pl.BlockSpec((tile, d), lambda i, f, te: (i, 0)),
        scratch_shapes=[pltpu.VMEM((tile, d), BF16), pltpu.VMEM((tile, d), F32)])
    return pl.pallas_call(
        _experts_kernel,
        grid_spec=grid_spec,
        out_shape=jax.ShapeDtypeStruct((n_rows, d), F32),
        compiler_params=_cparams("arbitrary", "arbitrary"),
        name="moe_experts",
    )(tile_expert, xs, w_gate.astype(BF16), w_up.astype(BF16), w_down.astype(BF16))


def _combine_kernel(slots_ref, next_slots_ref, x_ref, gates_ref, ys_hbm, out_a, out_b, buf_ref, sems, *, tm, tiles_a):
    i = pl.program_id(0)

    def gather(slots, half):
        def issue(t, carry):
            for k in range(2):
                _row_copy(ys_hbm, slots[0, 2 * t + k], buf_ref.at[half, k], t, sems.at[half]).start()
            return carry
        lax.fori_loop(0, tm, issue, 0, unroll=4)

    @pl.when(i == 0)
    def _():
        gather(slots_ref, 0)

    @pl.when(i + 1 < pl.num_programs(0))
    def _():
        gather(next_slots_ref, (i + 1) % 2)

    half = i % 2
    for k in range(2):
        pltpu.make_async_copy(ys_hbm.at[pl.ds(0, tm)], buf_ref.at[half, k], sems.at[half]).wait()
    gates = gates_ref[...]
    out = x_ref[...] + gates[:, 0:1] * buf_ref[half, 0] + gates[:, 1:2] * buf_ref[half, 1]

    @pl.when(i < tiles_a)
    def _():
        out_a[...] = out

    @pl.when(i >= tiles_a)
    def _():
        out_b[...] = out


def _combine(x, ys, slots, gates, rows_a, tm=256):
    t, d = x.shape
    tok = pl.BlockSpec((tm, d), lambda i: (i, 0))
    n_tiles = t // tm
    tiles_a = rows_a // tm
    slots3 = slots.reshape(n_tiles, 1, 2 * tm)
    out_a = pl.BlockSpec((tm, d), lambda i: (jnp.minimum(i, tiles_a - 1), 0))
    out_b = pl.BlockSpec((tm, d), lambda i: (jnp.maximum(i - tiles_a, 0), 0))
    return pl.pallas_call(
        functools.partial(_combine_kernel, tm=tm, tiles_a=tiles_a),
        grid=(n_tiles,),
        in_specs=[pl.BlockSpec((None, 1, 2 * tm), lambda i: (i, 0, 0), memory_space=pltpu.SMEM),
                  pl.BlockSpec((None, 1, 2 * tm), lambda i: (jnp.minimum(i + 1, n_tiles - 1), 0, 0),
                               memory_space=pltpu.SMEM),
                  tok, pl.BlockSpec((tm, 2), lambda i: (i, 0)), pl.BlockSpec(memory_space=pl.ANY)],
        out_specs=[out_a, out_b],
        out_shape=[jax.ShapeDtypeStruct((rows_a, d), F32), jax.ShapeDtypeStruct((t - rows_a, d), F32)],
        scratch_shapes=[pltpu.VMEM((2, 2, tm, d), F32), pltpu.SemaphoreType.DMA((2,))],
        compiler_params=_cparams("arbitrary"),
        name="moe_combine",
    )(slots3, slots3, x, gates, ys)


def _moe(x, gain, w_router, w_gate, w_up, w_down, rows_a, tile=512):
    n_experts = w_gate.shape[0]
    h, comb, rank, counts = _router(x, gain, w_router)
    slots, gates, tile_expert, holes, n_tiles = _route_plan(comb, rank, counts, n_experts, tile)
    xs = _dispatch(h, slots, holes, n_tiles * tile)
    ys = _experts(xs, tile_expert, w_gate, w_up, w_down, tile)
    return _combine(x, ys, slots, gates, rows_a)


def _lane_chunks(width):
    return [slice(c * LANES, (c + 1) * LANES) for c in range(width // LANES)]


def _store_dilated(dst_ref, stage_ref, value, dil):
    rows, width = value.shape
    if dil == 1:
        dst_ref[...] = value.astype(dst_ref.dtype)
        return
    for c, ls in enumerate(_lane_chunks(width)):
        stage_ref[c] = value[:, ls]
    for rho in range(dil):
        for c in range(width // LANES):
            dst_ref[:, pl.ds(rho * width + c * LANES, LANES)] = (
                stage_ref.at[c][pl.ds(rho, rows // dil, stride=dil), :].astype(dst_ref.dtype))


def _load_dilated(src_ref, stage_ref, base, rows, width, dil):
    if dil == 1:
        return src_ref[...].astype(F32)
    for rho in range(dil):
        for c in range(width // LANES):
            stage_ref.at[base + c][pl.ds(rho, rows // dil, stride=dil), :] = (
                src_ref[:, pl.ds(rho * width + c * LANES, LANES)].astype(F32))
    return jnp.concatenate([stage_ref[base + c] for c in range(width // LANES)], 1)


def _attn_in_kernel(x_ref, gain_ref, w_ref, hg_ref, e_ref, et_ref, *rest, dils):
    outs, (h_ref, stage_ref) = rest[:len(dils)], rest[len(dils):]
    g = pl.program_id(1)

    @pl.when(g == 0)
    def _():
        h_ref[...] = _rms(x_ref[...], gain_ref[...]).astype(BF16)

    n_qk = 2 * GROUP_WIDTH
    half = h_ref.shape[0] // 2
    ys = [jnp.dot(h_ref[pl.ds(r * half, half), :], w_ref[...], preferred_element_type=F32) for r in range(2)]
    sq = [_split_dot(yy[:, :n_qk] * yy[:, :n_qk], e_ref[...]) for yy in ys]
    ms = [_split_dot(x, et_ref[...]) * (1.0 / HEAD) for x in sq]
    qk = [yy[:, :n_qk] * lax.rsqrt(m + RMS_EPS) * hg_ref[...] for yy, m in zip(ys, ms)]
    y = jnp.concatenate([jnp.concatenate([a, yy[:, n_qk:]], 1) for a, yy in zip(qk, ys)], 0)
    for gi, dil in enumerate(dils):
        @pl.when(g == gi)
        def _(gi=gi, dil=dil):
            _store_dilated(outs[gi], stage_ref, y, dil)


def _attn_in(x, gain, w_in, q_gain, k_gain, tm=512):
    t, d = x.shape
    n_groups = q_gain.shape[0]
    gw = 3 * GROUP_WIDTH
    dils = tuple(dil for _, dil in ATTN_GROUPS)
    w = jnp.transpose(w_in.reshape(d, 3, n_groups, GROUP_WIDTH), (2, 0, 1, 3)).reshape(n_groups, d, gw).astype(BF16)
    head_gain = jnp.concatenate([jnp.tile(q_gain, (1, GROUP_HEADS)), jnp.tile(k_gain, (1, GROUP_HEADS))],
                                1).reshape(n_groups, 1, 2 * GROUP_WIDTH)
    e, et = _head_indicator(2 * GROUP_WIDTH)
    return pl.pallas_call(
        functools.partial(_attn_in_kernel, dils=dils),
        grid=(t // tm, n_groups),
        in_specs=[pl.BlockSpec((tm, d), lambda i, g: (i, 0)),
                  _const_spec((1, d)),
                  pl.BlockSpec((None, d, gw), lambda i, g: (g, 0, 0)),
                  pl.BlockSpec((None, 1, 2 * GROUP_WIDTH), lambda i, g: (g, 0, 0)),
                  _const_spec(e.shape), _const_spec(et.shape)],
        out_specs=[pl.BlockSpec((tm // dil, dil * gw), lambda i, g: (i, 0)) for dil in dils],
        out_shape=[jax.ShapeDtypeStruct((t // dil, dil * gw), BF16) for dil in dils],
        scratch_shapes=[pltpu.VMEM((tm, d), BF16), pltpu.VMEM((gw // LANES, tm, LANES), F32)],
        compiler_params=_cparams("arbitrary", "arbitrary"),
        name="attn_in",
    )(x, gain.reshape(1, d), w, head_gain, e, et)


def _t5_bucket_np(rel):
    nb = REL_BUCKETS // 2
    max_exact = nb // 2
    n = np.abs(rel)
    large = max_exact + (np.log(np.maximum(n, 1).astype(np.float32) / max_exact)
                         / math.log(REL_MAX_DIST / max_exact) * (nb - max_exact)).astype(np.int32)
    large = np.minimum(large, nb - 1)
    return np.where(rel > 0, nb, 0) + np.where(n < max_exact, n, large)


def _bias_table(rel_bias, group, window, dil):
    side = window // (2 * dil)
    assert side == KV_HALO
    buckets = _t5_bucket_np(dil * np.arange(-side, side + 1))
    table = rel_bias.reshape(REL_BUCKETS, -1, GROUP_HEADS)[buckets, group].T
    full = jnp.full((8, Q_BLOCK + 2 * KV_HALO), NEG_INF, F32)
    return full.at[:GROUP_HEADS, :2 * side + 1].set(table)


def _attn_kernel(q_ref, kp_ref, kc_ref, kn_ref, vp_ref, vc_ref, vn_ref, table_ref, o_ref, lse_ref, bias_ref,
                 *, regions, dil):
    tp, sp, ss = regions
    nk = Q_BLOCK + 2 * KV_HALO

    @pl.when((pl.program_id(0) == 0) & (pl.program_id(1) == 0))
    def _():
        for h in range(GROUP_HEADS):
            row = jnp.broadcast_to(table_ref[h:h + 1, :], (Q_BLOCK, nk))
            bias_ref[h] = pltpu.roll(row, 0, 1, stride=1, stride_axis=0)

    lane = lax.broadcasted_iota(jnp.int32, (Q_BLOCK, LANES), 1)
    mine = [(lane < HEAD), (lane >= HEAD)]
    scale = HEAD ** -0.5
    q_all = q_ref[...]
    k_all = jnp.concatenate([kp_ref[...], kc_ref[...], kn_ref[...]], 0)
    v_all = jnp.concatenate([vp_ref[...], vc_ref[...], vn_ref[...]], 0)
    lanes_of = lambda pair: slice(pair * LANES, (pair + 1) * LANES)

    units = [(blk, pair, sub) for blk in range(Q_STEP // Q_BLOCK) for pair in range(GROUP_HEADS // 2)
             for sub in range(2)]
    valid = []
    for blk in range(Q_STEP // Q_BLOCK):
        row0 = pl.program_id(1) * Q_STEP + blk * Q_BLOCK
        seq = _seq_len_at(row0, tp // dil, sp // dil, ss // dil)
        seq_start = row0 - lax.rem(row0, seq)
        key_row = row0 - KV_HALO + lax.broadcasted_iota(jnp.int32, (1, nk), 1)
        valid.append((key_row >= seq_start) & (key_row < seq_start + seq))
    s_all = []
    for blk, pair, sub in units:
        qp = q_all[blk * Q_BLOCK:(blk + 1) * Q_BLOCK, lanes_of(pair)]
        qm = jnp.where(mine[sub], qp, jnp.zeros_like(qp))
        keys = k_all[blk * Q_BLOCK:blk * Q_BLOCK + nk, lanes_of(pair)]
        s = lax.dot_general(qm, keys, (_NT, ((), ())), preferred_element_type=F32)
        s_all.append(jnp.where(valid[blk], s * scale + bias_ref[2 * pair + sub], NEG_INF))
    m_all = [jnp.max(s, axis=-1, keepdims=True) for s in s_all]
    p_all = [jnp.exp(s - m) for s, m in zip(s_all, m_all)]
    den_all = [jnp.sum(p, axis=-1, keepdims=True) for p in p_all]
    o_all = [jnp.dot((p / den).astype(v_all.dtype), v_all[blk * Q_BLOCK:blk * Q_BLOCK + nk, lanes_of(pair)],
                     preferred_element_type=F32)
             for p, den, (blk, pair, _) in zip(p_all, den_all, units)]
    lse_all = [m + jnp.log(den) for m, den in zip(m_all, den_all)]
    for i in range(0, len(units), 2):
        blk, pair, _ = units[i]
        rows = pl.ds(blk * Q_BLOCK, Q_BLOCK)
        o_ref[rows, lanes_of(pair)] = jnp.where(mine[0], o_all[i], o_all[i + 1]).astype(o_ref.dtype)
        lse_ref[rows, lanes_of(pair)] = jnp.where(mine[0], lse_all[i], lse_all[i + 1])


def _attn_group(qkv_d, rel_bias, group, regions):
    window, dil = ATTN_GROUPS[group]
    rows = qkv_d.shape[0]
    parts = 3
    table = _bias_table(rel_bias, group, window, dil)
    halo_per_step = Q_STEP // KV_HALO
    last_halo = rows // KV_HALO - 1

    def cur(part):
        return pl.BlockSpec((Q_STEP, GROUP_WIDTH), lambda rho, m: (m, rho * parts + part))

    def prev(part):
        return pl.BlockSpec((KV_HALO, GROUP_WIDTH),
                            lambda rho, m: (jnp.maximum(m * halo_per_step - 1, 0), rho * parts + part))

    def nxt(part):
        return pl.BlockSpec((KV_HALO, GROUP_WIDTH),
                            lambda rho, m: (jnp.minimum((m + 1) * halo_per_step, last_halo), rho * parts + part))

    out_spec = pl.BlockSpec((Q_STEP, GROUP_WIDTH), lambda rho, m: (m, rho))
    return pl.pallas_call(
        functools.partial(_attn_kernel, regions=regions, dil=dil),
        grid=(dil, rows // Q_STEP),
        in_specs=[cur(0), prev(1), cur(1), nxt(1), prev(2), cur(2), nxt(2), _const_spec(table.shape)],
        out_specs=[out_spec, out_spec],
        out_shape=[jax.ShapeDtypeStruct((rows, dil * GROUP_WIDTH), BF16),
                   jax.ShapeDtypeStruct((rows, dil * GROUP_WIDTH), F32)],
        scratch_shapes=[pltpu.VMEM((GROUP_HEADS, Q_BLOCK, Q_BLOCK + 2 * KV_HALO), F32)],
        compiler_params=_cparams("arbitrary", "arbitrary"),
        name=f"attn_g{group}",
    )(qkv_d, qkv_d, qkv_d, qkv_d, qkv_d, qkv_d, qkv_d, table)


def _attn_out_kernel(o0, o1, o2, l0, l1, l2, x_ref, wo_ref, out_ref, stage_ref, *, dils):
    tm = x_ref.shape[0]
    chunks = GROUP_WIDTH // LANES
    os_ = [_load_dilated(r, stage_ref, 2 * g * chunks, tm, GROUP_WIDTH, dils[g]) for g, r in enumerate((o0, o1, o2))]
    lses = [_load_dilated(r, stage_ref, (2 * g + 1) * chunks, tm, GROUP_WIDTH, dils[g])
            for g, r in enumerate((l0, l1, l2))]
    mx = jnp.maximum(jnp.maximum(lses[0], lses[1]), lses[2])
    ws = [jnp.exp(l - mx) for l in lses]
    tot = ws[0] + ws[1] + ws[2]
    acc = x_ref[...]
    for g in range(len(dils)):
        scaled = (os_[g] * (ws[g] / tot)).astype(BF16)
        acc = acc + jnp.dot(scaled, wo_ref[g], preferred_element_type=F32)
    out_ref[...] = acc


def _attn_out(os_, lses, x, w_o, tm=512):
    t, d = x.shape
    n_groups = len(os_)
    dils = tuple(dil for _, dil in ATTN_GROUPS)
    grp = [pl.BlockSpec((tm // dil, dil * GROUP_WIDTH), lambda i: (i, 0)) for dil in dils]
    tok = pl.BlockSpec((tm, d), lambda i: (i, 0))
    wo = w_o.reshape(n_groups, GROUP_WIDTH, d).astype(BF16)
    return pl.pallas_call(
        functools.partial(_attn_out_kernel, dils=dils),
        grid=(t // tm,),
        in_specs=grp + grp + [tok, _const_spec(wo.shape)],
        out_specs=tok,
        out_shape=jax.ShapeDtypeStruct((t, d), F32),
        scratch_shapes=[pltpu.VMEM((2 * n_groups * (GROUP_WIDTH // LANES), tm, LANES), F32)],
        compiler_params=_cparams("arbitrary"),
        name="attn_out",
    )(*os_, *lses, x, wo)


def kernel(x_prompt, x_sample, norm_mix, norm_ffn, rwkv_mu, rwkv_w_rkv, rwkv_w0, rwkv_w1, rwkv_w2, rwkv_a0, rwkv_a1, rwkv_a2, rwkv_g1, rwkv_g2, rwkv_k_k, rwkv_k_a, rwkv_r_k, rwkv_ln_g, rwkv_ln_b, rwkv_w_o, attn_w_in, attn_q_gain, attn_k_gain, attn_w_o, rel_bias, ffn_w_gate, ffn_w_up, ffn_w_down, moe_router, moe_w_gate, moe_w_up, moe_w_down):
    d = x_prompt.shape[-1]
    tp = x_prompt.shape[0] * x_prompt.shape[1]
    regions = (tp, x_prompt.shape[1], x_sample.shape[1])
    x = (x_prompt.reshape(-1, d), x_sample.reshape(-1, d))
    depth = norm_mix.shape[0]
    for i in range(depth):
        j = i // 2
        if i % 2 == 0:
            if not isinstance(x, tuple):
                x = (x[:tp], x[tp:])
            r, v, kk, g, bonus, logw, kdir, bdir = _rwkv_pre(
                *x, regions, norm_mix[i], rwkv_mu[j], rwkv_w_rkv[j], rwkv_w0[j], rwkv_w1[j], rwkv_w2[j],
                rwkv_a0[j], rwkv_a1[j], rwkv_a2[j], rwkv_g1[j], rwkv_g2[j], rwkv_k_k[j], rwkv_k_a[j], rwkv_r_k[j])
            yf = _wkv(r, v, kk, logw, kdir, bdir, regions, reverse=False)
            yb = _wkv(r, v, kk, logw, kdir, bdir, regions, reverse=True)
            x = _rwkv_post(yf, yb, bonus, g, *x, rwkv_ln_g[j], rwkv_ln_b[j], rwkv_w_o[j])
            x = _ffn(x, norm_ffn[i], ffn_w_gate[j], ffn_w_up[j], ffn_w_down[j])
        else:
            if isinstance(x, tuple):
                x = jnp.concatenate(x, 0)
            qkv = _attn_in(x, norm_mix[i], attn_w_in[j], attn_q_gain[j], attn_k_gain[j])
            outs = [_attn_group(qkv[g], rel_bias, g, regions) for g in range(len(ATTN_GROUPS))]
            x = _attn_out([o for o, _ in outs], [l for _, l in outs], x, attn_w_o[j])
            x = _moe(x, norm_ffn[i], moe_router[j], moe_w_gate[j], moe_w_up[j], moe_w_down[j], rows_a=tp)
            x = tuple(x)
    if not isinstance(x, tuple):
        x = (x[:tp], x[tp:])
    return (x[0].reshape(x_prompt.shape), x[1].reshape(x_sample.shape))
```
